```python
import jax, jax.numpy as jnp
from jax import lax
import numpy as np

D_MODEL = 1024
BATCH = 16
SEQ = 256
DEPTH = 4
DEC_BATCH = 2
DEC_SEQ = 2048
PAST_LEN = 256

GRID_W = 64
N_MIXERS = 2
N_CONV_LAYERS = (DEPTH + N_MIXERS - 1) // N_MIXERS
N_SSD_LAYERS = DEPTH // N_MIXERS
MOD_CHUNKS = 6
CONV_WIDTH = 3
SSD_EXPAND = 2
D_INNER = SSD_EXPAND * D_MODEL
SSD_HEADDIM = 64
SSD_HEADS = D_INNER // SSD_HEADDIM
SSD_GROUPS = 4
SSD_HPG = SSD_HEADS // SSD_GROUPS
SSD_STATE = 128
SSD_CHUNK = 128
SSD_CONV_DIM = D_INNER + 2 * SSD_GROUPS * SSD_STATE
SSD_IN_DIM = D_INNER + SSD_CONV_DIM + 2 * SSD_HEADS
N_EXPERTS = 16
EC_CAPACITY_FACTOR = 2
EXPERT_FF = D_MODEL
EPS = 1e-6

kernel_name = "bidir_conv_ssd_ec_diffusion_step"


def rmsnorm(x, w):
    xf = x.astype(jnp.float32)
    y = xf * lax.rsqrt(jnp.mean(xf * xf, axis=-1, keepdims=True) + EPS)
    return (y * w.astype(jnp.float32)).astype(x.dtype)


def conv3(x, w, grid_w):
    b, L, ch = x.shape
    if grid_w is None:
        xs = x[:, None]
    else:
        rows = L // grid_w
        xs = x.reshape(b, rows, grid_w, ch)
    n = xs.shape[2]
    xp = jnp.pad(xs, ((0, 0), (0, 0), (1, 1), (0, 0)))
    y = xp[:, :, :n] * w[0] + xp[:, :, 1:n + 1] * w[1] + xp[:, :, 2:] * w[2]
    return y.reshape(b, L, ch)


def modulation(cond, w_mod, b_mod):
    m = jax.nn.silu(cond) @ w_mod + b_mod
    return jnp.split(m, MOD_CHUNKS, axis=-1)


def short_conv_mixer(h, w_in, w_conv, w_out, grid_w):
    gb, gc, v = jnp.split(h @ w_in, 3, axis=-1)
    return (gb * conv3(gc * v, w_conv, grid_w)) @ w_out


def ssd_chunked(x, dt, a, bm, cm, h0):
    b, L = x.shape[:2]
    q = SSD_CHUNK
    nc = L // q
    x = x.reshape(b, nc, q, *x.shape[2:])
    dt = dt.reshape(b, nc, q, *dt.shape[2:])
    bm = bm.reshape(b, nc, q, *bm.shape[2:])
    cm = cm.reshape(b, nc, q, *cm.shape[2:])
    acs = jnp.cumsum(dt * a, axis=2)
    seg = acs[:, :, :, None] - acs[:, :, None, :]
    lower = jnp.tril(jnp.ones((q, q), dtype=bool))[:, :, None, None]
    decay = jnp.exp(jnp.where(lower, seg, -jnp.inf))
    cb = jnp.einsum('bcqgn,bcsgn->bcqsg', cm, bm)
    wts = cb[..., None] * decay * dt[:, :, None]
    y_diag = jnp.einsum('bcqsgh,bcsghp->bcqghp', wts, x)
    a_last = acs[:, :, -1]
    to_end = jnp.exp(a_last[:, :, None] - acs) * dt
    chunk_states = jnp.einsum('bcsgn,bcsghp->bcghpn', bm, to_end[..., None] * x)

    def step(h, inp):
        s_c, da = inp
        return h * jnp.exp(da)[..., None, None] + s_c, h

    h_final, h_enter = lax.scan(step, h0, (jnp.moveaxis(chunk_states, 1, 0), jnp.moveaxis(a_last, 1, 0)))
    h_enter = jnp.moveaxis(h_enter, 0, 1)
    y_off = jnp.einsum('bcqgn,bcghpn->bcqghp', cm, h_enter) * jnp.exp(acs)[..., None]
    return (y_diag + y_off).reshape(b, L, *x.shape[3:]), h_final


def ssd_mixer(h, h0, w_in, conv_w, conv_b, dt_bias, a_log, d_skip, norm_w, w_out, grid_w):
    b, L, _ = h.shape
    f32 = jnp.float32
    z, xbc, dt_raw = jnp.split(h @ w_in, [D_INNER, D_INNER + SSD_CONV_DIM], axis=-1)
    xbc = jax.nn.silu(conv3(xbc, conv_w, grid_w) + conv_b)
    xs, bm, cm = jnp.split(xbc, [D_INNER, D_INNER + SSD_GROUPS * SSD_STATE], axis=-1)
    xs = xs.reshape(b, L, SSD_GROUPS, SSD_HPG, SSD_HEADDIM).astype(f32)
    bm = bm.reshape(b, L, SSD_GROUPS, SSD_STATE).astype(f32)
    cm = cm.reshape(b, L, SSD_GROUPS, SSD_STATE).astype(f32)
    dt = jax.nn.softplus(dt_raw.reshape(b, L, 2, SSD_GROUPS, SSD_HPG).astype(f32)
                         + dt_bias.astype(f32).reshape(2, SSD_GROUPS, SSD_HPG))
    a = -jnp.exp(a_log.astype(f32)).reshape(2, SSD_GROUPS, SSD_HPG)
    dsk = d_skip.astype(f32).reshape(2, SSD_GROUPS, SSD_HPG)
    h0 = h0.astype(f32).reshape(b, 2, SSD_GROUPS, SSD_HPG, SSD_HEADDIM, SSD_STATE)
    y_f, hf = ssd_chunked(xs, dt[:, :, 0], a[0], bm, cm, h0[:, 0])
    flip = lambda t: jnp.flip(t, axis=1)
    y_b, hb = ssd_chunked(flip(xs), flip(dt[:, :, 1]), a[1], flip(bm), flip(cm), h0[:, 1])
    y = y_f + flip(y_b) + (dsk[0] + dsk[1])[..., None] * xs
    y = y.reshape(b, L, D_INNER).astype(h.dtype)
    y = rmsnorm(y * jax.nn.silu(z), norm_w)
    states = jnp.stack([hf, hb], axis=1).reshape(b, 2, SSD_HEADS, SSD_HEADDIM, SSD_STATE)
    return y @ w_out, states


def expert_choice_ffn(h, w_router, w_gate, w_up, w_down):
    b, T, _ = h.shape
    cap = EC_CAPACITY_FACTOR * T // N_EXPERTS
    aff = jax.nn.softmax((h @ w_router).astype(jnp.float32), axis=-1)
    g, idx = lax.top_k(jnp.swapaxes(aff, 1, 2), cap)
    bidx = jnp.arange(b)[:, None, None]
    xe = h[bidx, idx]
    hid = jax.nn.silu(jnp.einsum('becd,edf->becf', xe, w_gate)) * jnp.einsum('becd,edf->becf', xe, w_up)
    ye = jnp.einsum('becf,efd->becd', hid, w_down) * g[..., None].astype(h.dtype)
    return jnp.zeros_like(h).at[bidx, idx].add(ye)


def run_stream(x, cond, ssm_init, grid_w, norm1_w, norm2_w, w_mod, b_mod,
               conv_in_w, conv_w, conv_out_w,
               ssd_in_w, ssd_conv_w, ssd_conv_b, ssd_dt_bias, ssd_a_log, ssd_d, ssd_norm_w, ssd_out_w,
               router_w, exp_w_gate, exp_w_up, exp_w_down, final_norm_w):
    states = []
    for layer in range(DEPTH):
        sh1, sc1, g1, sh2, sc2, g2 = modulation(cond, w_mod[layer], b_mod[layer])
        hn = rmsnorm(x, norm1_w[layer]) * (1 + sc1) + sh1
        j = layer // N_MIXERS
        if layer % N_MIXERS == 0:
            mix = short_conv_mixer(hn, conv_in_w[j], conv_w[j], conv_out_w[j], grid_w)
        else:
            mix, st = ssd_mixer(hn, ssm_init[:, j], ssd_in_w[j], ssd_conv_w[j], ssd_conv_b[j],
                                ssd_dt_bias[j], ssd_a_log[j], ssd_d[j], ssd_norm_w[j], ssd_out_w[j], grid_w)
            states.append(st)
        x = x + g1 * mix
        hn = rmsnorm(x, norm2_w[layer]) * (1 + sc2) + sh2
        x = x + g2 * expert_choice_ffn(hn, router_w[layer], exp_w_gate[layer], exp_w_up[layer], exp_w_down[layer])
    return rmsnorm(x, final_norm_w), jnp.stack(states, axis=1)


def setup_inputs(seed: int = 0) -> dict:
    key = jax.random.key(seed)
    ks = jax.random.split(key, 32)
    f32 = jnp.float32
    nrm = lambda k, shape, s: jax.random.normal(k, shape, f32) * s
    D, E, F = D_MODEL, N_EXPERTS, EXPERT_FF
    dt0 = jnp.exp(jax.random.uniform(ks[12], (N_SSD_LAYERS, 2, SSD_HEADS), f32,
                                     float(np.log(1e-3)), float(np.log(1e-1))))
    return {
        "x_prompt": nrm(ks[0], (BATCH, SEQ, D), 1.0),
        "x_sample": nrm(ks[1], (DEC_BATCH, DEC_SEQ, D), 1.0),
        "state_ssm": nrm(ks[2], (DEC_BATCH, N_SSD_LAYERS, 2, SSD_HEADS, SSD_HEADDIM, SSD_STATE), 0.5),
        "c": nrm(ks[3], (DEC_BATCH, D), 1.0),
        "c_ctx": nrm(ks[4], (D,), 1.0),
        "norm1_w": 1.0 + nrm(ks[5], (DEPTH, D), 0.02),
        "norm2_w": 1.0 + nrm(ks[6], (DEPTH, D), 0.02),
        "w_mod": nrm(ks[7], (DEPTH, D, MOD_CHUNKS * D), 0.5 * D ** -0.5),
        "b_mod": nrm(ks[8], (DEPTH, MOD_CHUNKS * D), 0.02),
        "conv_in_w": nrm(ks[9], (N_CONV_LAYERS, D, 3 * D), D ** -0.5),
        "conv_w": nrm(ks[10], (N_CONV_LAYERS, CONV_WIDTH, D), CONV_WIDTH ** -0.5),
        "conv_out_w": nrm(ks[11], (N_CONV_LAYERS, D, D), D ** -0.5),
        "ssd_in_w": nrm(ks[13], (N_SSD_LAYERS, D, SSD_IN_DIM), D ** -0.5),
        "ssd_conv_w": nrm(ks[14], (N_SSD_LAYERS, CONV_WIDTH, SSD_CONV_DIM), CONV_WIDTH ** -0.5),
        "ssd_conv_b": nrm(ks[15], (N_SSD_LAYERS, SSD_CONV_DIM), 0.02),
        "ssd_dt_bias": dt0 + jnp.log(-jnp.expm1(-dt0)),
        "ssd_a_log": jnp.log(jax.random.uniform(ks[16], (N_SSD_LAYERS, 2, SSD_HEADS), f32, 1.0, 16.0)),
        "ssd_d": 1.0 + nrm(ks[17], (N_SSD_LAYERS, 2, SSD_HEADS), 0.02),
        "ssd_norm_w": 1.0 + nrm(ks[18], (N_SSD_LAYERS, D_INNER), 0.02),
        "ssd_out_w": nrm(ks[19], (N_SSD_LAYERS, D_INNER, D), D_INNER ** -0.5),
        "router_w": nrm(ks[20], (DEPTH, D, E), D ** -0.5),
        "exp_w_gate": nrm(ks[21], (DEPTH, E, D, F), D ** -0.5),
        "exp_w_up": nrm(ks[22], (DEPTH, E, D, F), D ** -0.5),
        "exp_w_down": nrm(ks[23], (DEPTH, E, F, D), F ** -0.5),
        "final_norm_w": 1.0 + nrm(ks[24], (D,), 0.02),
    }


def reference(x_prompt, x_sample, state_ssm, c, c_ctx, norm1_w, norm2_w, w_mod, b_mod,
              conv_in_w, conv_w, conv_out_w, ssd_in_w, ssd_conv_w, ssd_conv_b, ssd_dt_bias,
              ssd_a_log, ssd_d, ssd_norm_w, ssd_out_w, router_w, exp_w_gate, exp_w_up,
              exp_w_down, final_norm_w):
    weights = (norm1_w, norm2_w, w_mod, b_mod, conv_in_w, conv_w, conv_out_w,
               ssd_in_w, ssd_conv_w, ssd_conv_b, ssd_dt_bias, ssd_a_log, ssd_d, ssd_norm_w, ssd_out_w,
               router_w, exp_w_gate, exp_w_up, exp_w_down, final_norm_w)
    ctx_init = jnp.zeros((x_prompt.shape[0], N_SSD_LAYERS, 2, SSD_HEADS, SSD_HEADDIM, SSD_STATE), jnp.float32)
    y_prompt, new_state_ssm = run_stream(x_prompt, c_ctx[None, None, :], ctx_init, None, *weights)
    y_sample, _ = run_stream(x_sample, c[:, None, :], state_ssm, GRID_W, *weights)
    return (y_prompt, y_sample, new_state_ssm)
```

```python
import functools

import jax
import jax.numpy as jnp
from jax import lax
from jax.experimental import pallas as pl
from jax.experimental.pallas import tpu as pltpu

F32 = jnp.float32
BF16 = jnp.bfloat16
HIGHEST = lax.Precision.HIGHEST

D = 1024
DEPTH = 4
N_CTX, L_CTX = 16, 256
N_LAT, L_LAT = 2, 2048
GRID_W = 64
R_CTX = N_CTX * L_CTX
R_LAT = N_LAT * L_LAT
R = R_CTX + R_LAT
N_COND = 1 + N_LAT
MOD_CHUNKS = 6
D_INNER = 2048
HEADDIM = 64
HEADS = 32
GROUPS = 4
HPG = HEADS // GROUPS
GW = HPG * HEADDIM
STATE = 128
CHUNK = 128
CONV_DIM = D_INNER + 2 * GROUPS * STATE
N_EXPERTS = 16
EPS = 1e-6

TM = 512
CTX_TILES = R_CTX // TM
LAT_TILES_PER_REQ = L_LAT // TM
SR_TOKENS = 2048
N_SR = R // SR_TOKENS
SR_SLOTS = 2 * SR_TOKENS // N_EXPERTS
CTX_PER_SR = SR_TOKENS // L_CTX
CAP_CTX = 2 * L_CTX // N_EXPERTS
CAP_LAT = 2 * L_LAT // N_EXPERTS
VMEM_LIMIT = 56 * 1024 * 1024


def _cparams(sem):
    return pltpu.CompilerParams(dimension_semantics=sem, vmem_limit_bytes=VMEM_LIMIT)


def _cond_of_tile(i):
    return jnp.where(i < CTX_TILES, 0, 1 + (i - CTX_TILES) // LAT_TILES_PER_REQ)


def _silu(v):
    return v * (1.0 / (1.0 + jnp.exp(-v)))


def _softplus(v):
    return jnp.maximum(v, 0.0) + jnp.log1p(jnp.exp(-jnp.abs(v)))


def _norm_mod(x, nw, scale, shift):
    y = x * lax.rsqrt(jnp.mean(x * x, axis=-1, keepdims=True) + EPS)
    return y * nw * (1.0 + scale) + shift


def _conv3_rows(p, w_ref, tile_idx):
    n = p.shape[0]
    period = jnp.where(tile_idx < CTX_TILES, L_CTX, GRID_W)
    r = lax.broadcasted_iota(jnp.int32, (n, 1), 0) & (period - 1)
    prev = jnp.where(r == 0, 0.0, pltpu.roll(p, 1, axis=0))
    nxt = jnp.where(r == period - 1, 0.0, pltpu.roll(p, n - 1, axis=0))
    return prev * w_ref[0:1, :] + p * w_ref[1:2, :] + nxt * w_ref[2:3, :]


def _modulation_kernel(condT_ref, w_ref, b_ref, o_ref):
    s = _silu(condT_ref[...])
    w = w_ref[...]
    rows = [jnp.sum(w * s[:, r:r + 1], axis=0, keepdims=True) + b_ref[...] for r in range(N_COND)]
    rows.append(jnp.zeros((8 - N_COND, w.shape[1]), F32))
    o_ref[...] = jnp.concatenate(rows, axis=0)


def _modulation(condT, w_mod, b_mod):
    tn = 1536
    n = MOD_CHUNKS * D
    return pl.pallas_call(
        _modulation_kernel,
        out_shape=jax.ShapeDtypeStruct((DEPTH, 8, n), F32),
        grid=(DEPTH, n // tn),
        in_specs=[
            pl.BlockSpec((D, 8), lambda l, j: (0, 0)),
            pl.BlockSpec((None, D, tn), lambda l, j: (l, 0, j)),
            pl.BlockSpec((None, 1, tn), lambda l, j: (l, 0, j)),
        ],
        out_specs=pl.BlockSpec((None, 8, tn), lambda l, j: (l, 0, j)),
        compiler_params=_cparams(("arbitrary", "arbitrary")),
        name="modulation",
    )(condT, w_mod, b_mod.reshape(DEPTH, 1, n))


def _conv_mixer_kernel(x_ref, mod_ref, nw_ref, win_ref, cw_ref, wout_ref, o_ref):
    i = pl.program_id(0)
    x = x_ref[...]
    hn = _norm_mod(x, nw_ref[...], mod_ref[1:2, :], mod_ref[0:1, :]).astype(BF16)
    gb = jnp.dot(hn, win_ref[:, 0:D], preferred_element_type=F32)
    gc = jnp.dot(hn, win_ref[:, D:2 * D], preferred_element_type=F32)
    v = jnp.dot(hn, win_ref[:, 2 * D:3 * D], preferred_element_type=F32)
    q = (gb * _conv3_rows(gc * v, cw_ref, i)).astype(BF16)
    mix = jnp.dot(q, wout_ref[...], preferred_element_type=F32)
    o_ref[...] = x + mod_ref[2:3, :] * mix


def _conv_mixer(x, mod_l, nw, w_in, cw, w_out):
    return pl.pallas_call(
        _conv_mixer_kernel,
        out_shape=jax.ShapeDtypeStruct((R, D), F32),
        grid=(R // TM,),
        in_specs=[
            pl.BlockSpec((TM, D), lambda i: (i, 0)),
            pl.BlockSpec((None, MOD_CHUNKS, D), lambda i: (_cond_of_tile(i), 0, 0)),
            pl.BlockSpec((1, D), lambda i: (0, 0)),
            pl.BlockSpec((D, 3 * D), lambda i: (0, 0)),
            pl.BlockSpec((3, D), lambda i: (0, 0)),
            pl.BlockSpec((D, D), lambda i: (0, 0)),
        ],
        out_specs=pl.BlockSpec((TM, D), lambda i: (i, 0)),
        compiler_params=_cparams(("arbitrary",)),
        name="conv_mixer",
    )(x, mod_l, nw, w_in, cw, w_out)


def _ssd_in_kernel(x_ref, mod_ref, nw_ref, win_ref, wdtT_ref, cw_ref, cb_ref, dtb_ref, dtbT_ref,
                   z_ref, xs_ref, bm_ref, cm_ref, dt_ref, dtT_ref):
    i = pl.program_id(0)
    hn = _norm_mod(x_ref[...], nw_ref[...], mod_ref[1:2, :], mod_ref[0:1, :]).astype(BF16)
    for k in range(D_INNER // D):
        z_ref[:, k * D:(k + 1) * D] = jnp.dot(hn, win_ref[:, k * D:(k + 1) * D], preferred_element_type=F32)
    for k in range(CONV_DIM // D):
        lo = D_INNER + k * D
        u = jnp.dot(hn, win_ref[:, lo:lo + D], preferred_element_type=F32)
        u = _silu(_conv3_rows(u, cw_ref.at[:, k * D:(k + 1) * D], i) + cb_ref[:, k * D:(k + 1) * D])
        if k < D_INNER // D:
            xs_ref[:, k * D:(k + 1) * D] = u
        else:
            bm_ref[...] = u[:, :GROUPS * STATE]
            cm_ref[...] = u[:, GROUPS * STATE:]
    lo = D_INNER + CONV_DIM
    dt = _softplus(jnp.dot(hn, win_ref[:, lo:lo + 2 * HEADS], preferred_element_type=F32) + dtb_ref[...])
    dtT = _softplus(
        lax.dot_general(wdtT_ref[...], hn, (((1,), (1,)), ((), ())), preferred_element_type=F32) + dtbT_ref[...])
    for g in range(GROUPS):
        dt_ref[g] = dt[:, g * 2 * HPG:(g + 1) * 2 * HPG].reshape(TM // CHUNK, CHUNK, 2 * HPG)
        for k in range(TM // CHUNK):
            dtT_ref[g, k] = dtT[g * 2 * HPG:(g + 1) * 2 * HPG, k * CHUNK:(k + 1) * CHUNK]


def _ssd_in(x, mod_l, nw, w_in, w_dtT, cw, cb, dtb, dtbT):
    n_in = w_in.shape[1]
    nck = TM // CHUNK
    return pl.pallas_call(
        _ssd_in_kernel,
        out_shape=(
            jax.ShapeDtypeStruct((R, D_INNER), F32),
            jax.ShapeDtypeStruct((R, D_INNER), F32),
            jax.ShapeDtypeStruct((R, GROUPS * STATE), F32),
            jax.ShapeDtypeStruct((R, GROUPS * STATE), F32),
            jax.ShapeDtypeStruct((GROUPS, R // CHUNK, CHUNK, 2 * HPG), F32),
            jax.ShapeDtypeStruct((GROUPS, R // CHUNK, 2 * HPG, CHUNK), F32),
        ),
        grid=(R // TM,),
        in_specs=[
            pl.BlockSpec((TM, D), lambda i: (i, 0)),
            pl.BlockSpec((None, MOD_CHUNKS, D), lambda i: (_cond_of_tile(i), 0, 0)),
            pl.BlockSpec((1, D), lambda i: (0, 0)),
            pl.BlockSpec((D, n_in), lambda i: (0, 0)),
            pl.BlockSpec((2 * HEADS, D), lambda i: (0, 0)),
            pl.BlockSpec((3, CONV_DIM), lambda i: (0, 0)),
            pl.BlockSpec((1, CONV_DIM), lambda i: (0, 0)),
            pl.BlockSpec((1, 2 * HEADS), lambda i: (0, 0)),
            pl.BlockSpec((2 * HEADS, 1), lambda i: (0, 0)),
        ],
        out_specs=(
            pl.BlockSpec((TM, D_INNER), lambda i: (i, 0)),
            pl.BlockSpec((TM, D_INNER), lambda i: (i, 0)),
            pl.BlockSpec((TM, GROUPS * STATE), lambda i: (i, 0)),
            pl.BlockSpec((TM, GROUPS * STATE), lambda i: (i, 0)),
            pl.BlockSpec((GROUPS, nck, CHUNK, 2 * HPG), lambda i: (0, i, 0, 0)),
            pl.BlockSpec((GROUPS, nck, 2 * HPG, CHUNK), lambda i: (0, i, 0, 0)),
        ),
        compiler_params=_cparams(("arbitrary",)),
        name="ssd_in",
    )(x, mod_l, nw, w_in, w_dtT, cw, cb, dtb, dtbT)


def _ssd_scan_kernel(*refs, n_chunks, has_h0, emit_state):
    xs_ref, bm_ref, cm_ref, dt_ref, dtT_ref, alog_ref, alogT_ref, dsk_ref = refs[:8]
    k = 8
    h0_ref = None
    if has_h0:
        h0_ref = refs[k]
        k += 1
    y_ref = refs[k]
    k += 1
    st_out_ref = None
    if emit_state:
        st_out_ref = refs[k]
        k += 1
    st_ref = refs[k]

    a_row = -jnp.exp(alog_ref[...])
    a_col = -jnp.exp(alogT_ref[...])
    qi = lax.broadcasted_iota(jnp.int32, (CHUNK, CHUNK), 0)
    si = lax.broadcasted_iota(jnp.int32, (CHUNK, CHUNK), 1)
    lower = si <= qi
    upper = si >= qi
    tri_lo = lower.astype(F32)
    tri_up = upper.astype(F32)
    eh = lax.broadcasted_iota(jnp.int32, (2 * HPG, GW), 0)
    ej = lax.broadcasted_iota(jnp.int32, (2 * HPG, GW), 1) // HEADDIM
    exp_f = (eh == ej).astype(F32)
    exp_b = (eh == ej + HPG).astype(F32)

    def expand(v, e):
        return jnp.dot(v, e, precision=HIGHEST, preferred_element_type=F32)

    for d in range(2):
        if has_h0:
            st_ref[d] = h0_ref[d].T
        else:
            st_ref[d] = jnp.zeros((STATE, GW), F32)

    def load(c):
        x = xs_ref[c]
        b = bm_ref[c]
        cm = cm_ref[c]
        dt = dt_ref[c]
        dta = dt * a_row
        return x, b, cm, dt, dta

    def fwd_body(c, carry):
        x, b, cm, dt, dta = load(c)
        dtT = dtT_ref[c]
        dtaT = dtT * a_col
        acs_lo = jnp.dot(tri_lo, dta, precision=HIGHEST, preferred_element_type=F32)
        acs_up = jnp.dot(tri_up, dta, precision=HIGHEST, preferred_element_type=F32)
        acsT_lo = jnp.dot(dtaT, tri_up, precision=HIGHEST, preferred_element_type=F32)
        acsT_up = jnp.dot(dtaT, tri_lo, precision=HIGHEST, preferred_element_type=F32)
        cb = lax.dot_general(cm.astype(BF16), b.astype(BF16), (((1,), (1,)), ((), ())),
                             preferred_element_type=F32)
        xb = x.astype(BF16)
        parts = []
        for h in range(HPG):
            lf = jnp.exp(jnp.where(lower, acs_lo[:, h:h + 1] - acsT_lo[h:h + 1, :], -jnp.inf))
            lb = jnp.exp(jnp.where(upper, acs_up[:, HPG + h:HPG + h + 1] - acsT_up[HPG + h:HPG + h + 1, :], -jnp.inf))
            w = cb * (lf * dtT[h:h + 1, :] + lb * dtT[HPG + h:HPG + h + 1, :])
            parts.append(jnp.dot(w.astype(BF16), xb[:, h * HEADDIM:(h + 1) * HEADDIM], preferred_element_type=F32))
        y = jnp.concatenate(parts, axis=1) + dsk_ref[...] * x
        st = st_ref[0]
        y = y + jnp.dot(cm.astype(BF16), st.astype(BF16), preferred_element_type=F32) * expand(jnp.exp(acs_lo), exp_f)
        y_ref[c] = y
        a_last = acs_lo[CHUNK - 1:CHUNK, :]
        xw = x * expand(jnp.exp(a_last - acs_lo) * dt, exp_f)
        st_ref[0] = st * expand(jnp.exp(a_last), exp_f) + jnp.dot(
            b.T.astype(BF16), xw.astype(BF16), preferred_element_type=F32)
        return carry

    def bwd_body(k_, carry):
        c = n_chunks - 1 - k_
        x, b, cm, dt, dta = load(c)
        acs_up = jnp.dot(tri_up, dta, precision=HIGHEST, preferred_element_type=F32)
        st = st_ref[1]
        y_off = jnp.dot(cm.astype(BF16), st.astype(BF16), preferred_element_type=F32) * expand(jnp.exp(acs_up), exp_b)
        y_ref[c] = y_ref[c] + y_off
        a_tot = acs_up[0:1, :]
        xw = x * expand(jnp.exp(a_tot - acs_up) * dt, exp_b)
        st_ref[1] = st * expand(jnp.exp(a_tot), exp_b) + jnp.dot(
            b.T.astype(BF16), xw.astype(BF16), preferred_element_type=F32)
        return carry

    lax.fori_loop(0, n_chunks, fwd_body, 0)
    lax.fori_loop(0, n_chunks, bwd_body, 0)
    if emit_state:
        for d in range(2):
            st_out_ref[d] = st_ref[d].T


def _ssd_scan(xs3, bm3, cm3, dt4, dtT4, alog4, alogT4, dskx, h0, *, n_seq, seq_len, row_block0, emit_state):
    nck = seq_len // CHUNK
    has_h0 = h0 is not None
    rb = lambda s: s + row_block0
    in_specs = [
        pl.BlockSpec((nck, CHUNK, GW), lambda s, g: (rb(s), 0, g)),
        pl.BlockSpec((nck, CHUNK, STATE), lambda s, g: (rb(s), 0, g)),
        pl.BlockSpec((nck, CHUNK, STATE), lambda s, g: (rb(s), 0, g)),
        pl.BlockSpec((None, nck, CHUNK, 2 * HPG), lambda s, g: (g, rb(s), 0, 0)),
        pl.BlockSpec((None, nck, 2 * HPG, CHUNK), lambda s, g: (g, rb(s), 0, 0)),
        pl.BlockSpec((None, 1, 2 * HPG), lambda s, g: (g, 0, 0)),
        pl.BlockSpec((None, 2 * HPG, 1), lambda s, g: (g, 0, 0)),
        pl.BlockSpec((1, GW), lambda s, g: (0, g)),
    ]
    args = [xs3, bm3, cm3, dt4, dtT4, alog4, alogT4, dskx]
    if has_h0:
        in_specs.append(pl.BlockSpec((None, 2, GW, STATE), lambda s, g: (s, 0, g, 0)))
        args.append(h0)
    y_shape = jax.ShapeDtypeStruct((n_seq * nck, CHUNK, D_INNER), F32)
    y_spec = pl.BlockSpec((nck, CHUNK, GW), lambda s, g: (s, 0, g))
    if emit_state:
        out_shape = (y_shape, jax.ShapeDtypeStruct((n_seq, 2, HEADS * HEADDIM, STATE), F32))
        out_specs = (y_spec, pl.BlockSpec((None, 2, GW, STATE), lambda s, g: (s, 0, g, 0)))
    else:
        out_shape, out_specs = y_shape, y_spec
    return pl.pallas_call(
        functools.partial(_ssd_scan_kernel, n_chunks=nck, has_h0=has_h0, emit_state=emit_state),
        out_shape=out_shape,
        grid=(n_seq, GROUPS),
        in_specs=in_specs,
        out_specs=out_specs,
        scratch_shapes=[pltpu.VMEM((2, STATE, GW), F32)],
        compiler_params=_cparams(("arbitrary", "arbitrary")),
        name="ssd_scan",
    )(*args)


def _ssd_out_kernel(x_ref, y_ref, z_ref, mod_ref, nw_ref, wout_ref, o_ref):
    v = y_ref[...] * _silu(z_ref[...])
    v = v * lax.rsqrt(jnp.mean(v * v, axis=-1, keepdims=True) + EPS) * nw_ref[...]
    mix = jnp.dot(v.astype(BF16), wout_ref[...], preferred_element_type=F32)
    o_ref[...] = x_ref[...] + mod_ref[2:3, :] * mix


def _ssd_out(x, y, z, mod_l, nw, w_out):
    return pl.pallas_call(
        _ssd_out_kernel,
        out_shape=jax.ShapeDtypeStruct((R, D), F32),
        grid=(R // TM,),
        in_specs=[
            pl.BlockSpec((TM, D), lambda i: (i, 0)),
            pl.BlockSpec((TM, D_INNER), lambda i: (i, 0)),
            pl.BlockSpec((TM, D_INNER), lambda i: (i, 0)),
            pl.BlockSpec((None, MOD_CHUNKS, D), lambda i: (_cond_of_tile(i), 0, 0)),
            pl.BlockSpec((1, D_INNER), lambda i: (0, 0)),
            pl.BlockSpec((D_INNER, D), lambda i: (0, 0)),
        ],
        out_specs=pl.BlockSpec((TM, D), lambda i: (i, 0)),
        compiler_params=_cparams(("arbitrary",)),
        name="ssd_out",
    )(x, y, z, mod_l, nw, w_out)


def _router_kernel(x_ref, mod_ref, nw_ref, wrT_ref, hn_ref, affT_ref):
    hn = _norm_mod(x_ref[...], nw_ref[...], mod_ref[4:5, :], mod_ref[3:4, :])
    hn_ref[...] = hn.astype(BF16)
    logits = lax.dot_general(wrT_ref[...], hn, (((1,), (1,)), ((), ())),
                             precision=HIGHEST, preferred_element_type=F32)
    e = jnp.exp(logits - jnp.max(logits, axis=0, keepdims=True))
    affT_ref[...] = e / jnp.sum(e, axis=0, keepdims=True)


def _router(x, mod_l, nw, wrT):
    return pl.pallas_call(
        _router_kernel,
        out_shape=(jax.ShapeDtypeStruct((R, D), BF16), jax.ShapeDtypeStruct((N_EXPERTS, R), F32)),
        grid=(R // TM,),
        in_specs=[
            pl.BlockSpec((TM, D), lambda i: (i, 0)),
            pl.BlockSpec((None, MOD_CHUNKS, D), lambda i: (_cond_of_tile(i), 0, 0)),
            pl.BlockSpec((1, D), lambda i: (0, 0)),
            pl.BlockSpec((N_EXPERTS, D), lambda i: (0, 0)),
        ],
        out_specs=(pl.BlockSpec((TM, D), lambda i: (i, 0)), pl.BlockSpec((N_EXPERTS, TM), lambda i: (0, i))),
        compiler_params=_cparams(("arbitrary",)),
        name="router",
    )(x, mod_l, nw, wrT)


def _excl_cumsum_lanes(m):
    blk = 256
    t = m.shape[1]
    a = lax.broadcasted_iota(jnp.int32, (blk, blk), 0)
    b = lax.broadcasted_iota(jnp.int32, (blk, blk), 1)
    strict = jnp.where(a < b, 1.0, 0.0).astype(BF16)
    carry = jnp.zeros((m.shape[0], 1), F32)
    outs = []
    for k in range(t // blk):
        mk = m[:, k * blk:(k + 1) * blk]
        outs.append(jnp.dot(mk.astype(BF16), strict, preferred_element_type=F32) + carry)
        carry = carry + jnp.sum(mk, axis=1, keepdims=True)
    return outs[0] if len(outs) == 1 else jnp.concatenate(outs, axis=1)


def _select_kernel(affT_ref, slotT_ref, slot_tok_ref, gate_tok_ref, *, cap, slot_base_period):
    aff = affT_ref[...]
    bits = pltpu.bitcast(aff, jnp.int32)
    thr = jnp.zeros((N_EXPERTS, 1), jnp.int32)
    for k in range(30, -1, -1):
        trial = thr | (1 << k)
        cnt = jnp.sum(jnp.where(bits >= trial, 1.0, 0.0), axis=1, keepdims=True)
        thr = jnp.where(cnt >= cap, trial, thr)
    gt = bits > thr
    eq = jnp.where(bits == thr, 1.0, 0.0)
    need = cap - jnp.sum(jnp.where(gt, 1.0, 0.0), axis=1, keepdims=True)
    sel = gt | ((eq > 0.0) & (_excl_cumsum_lanes(eq) < need))
    pos = _excl_cumsum_lanes(jnp.where(sel, 1.0, 0.0))
    base = 0.0
    if slot_base_period:
        base = ((pl.program_id(0) % slot_base_period) * cap).astype(F32)
    slot = jnp.where(sel, pos + base, -1.0)
    slotT_ref[...] = slot.astype(jnp.int32)
    t = aff.shape[1]
    pad = jnp.zeros((128 - N_EXPERTS, t), F32)
    slot_tok_ref[...] = jnp.concatenate([slot, pad - 1.0], axis=0).T.astype(jnp.int32)
    gate_tok_ref[...] = jnp.concatenate([jnp.where(sel, aff, 0.0), pad], axis=0).T


def _select(affT, *, n_req, t, block0, cap, slot_base_period):
    return pl.pallas_call(
        functools.partial(_select_kernel, cap=cap, slot_base_period=slot_base_period),
        out_shape=(
            jax.ShapeDtypeStruct((N_EXPERTS, n_req * t), jnp.int32),
            jax.ShapeDtypeStruct((n_req * t, 128), jnp.int32),
            jax.ShapeDtypeStruct((n_req * t, 128), F32),
        ),
        grid=(n_req,),
        in_specs=[pl.BlockSpec((N_EXPERTS, t), lambda b: (0, b + block0))],
        out_specs=(
            pl.BlockSpec((N_EXPERTS, t), lambda b: (0, b)),
            pl.BlockSpec((t, 128), lambda b: (b, 0)),
            pl.BlockSpec((t, 128), lambda b: (b, 0)),
        ),
        compiler_params=_cparams(("arbitrary",)),
        name="select",
    )(affT)


def _gather_kernel(hn_ref, slotT_ref, xe_ref):
    e = pl.program_id(1)
    slot_row = slotT_ref[pl.ds(e, 1), :]
    s_iota = lax.broadcasted_iota(jnp.int32, (SR_SLOTS, SR_TOKENS), 0)
    onehot = jnp.where(s_iota == slot_row, 1.0, 0.0).astype(BF16)
    xe_ref[...] = jnp.dot(onehot, hn_ref[...], preferred_element_type=F32).astype(BF16)


def _gather(hn, slotT):
    return pl.pallas_call(
        _gather_kernel,
        out_shape=jax.ShapeDtypeStruct((N_EXPERTS, N_SR * SR_SLOTS, D), BF16),
        grid=(N_SR, N_EXPERTS),
        in_specs=[
            pl.BlockSpec((SR_TOKENS, D), lambda s, e: (s, 0)),
            pl.BlockSpec((N_EXPERTS, SR_TOKENS), lambda s, e: (0, s)),
        ],
        out_specs=pl.BlockSpec((None, SR_SLOTS, D), lambda s, e: (e, s, 0)),
        compiler_params=_cparams(("arbitrary", "arbitrary")),
        name="gather",
    )(hn, slotT)


def _ffn_kernel(xe_ref, wg_ref, wu_ref, wd_ref, ye_ref, acc_ref):
    f = pl.program_id(1)
    xe = xe_ref[...]
    hg = jnp.dot(xe, wg_ref[...].astype(BF16), preferred_element_type=F32)
    hu = jnp.dot(xe, wu_ref[...].astype(BF16), preferred_element_type=F32)
    hid = (_silu(hg) * hu).astype(BF16)
    part = jnp.dot(hid, wd_ref[...].astype(BF16), preferred_element_type=F32)

    @pl.when(f == 0)
    def _():
        acc_ref[...] = part

    @pl.when(f > 0)
    def _():
        acc_ref[...] = acc_ref[...] + part

    @pl.when(f == pl.num_programs(1) - 1)
    def _():
        ye_ref[...] = acc_ref[...].astype(BF16)


def _ffn(xe, wg, wu, wd):
    tf = 512
    m = xe.shape[1]
    return pl.pallas_call(
        _ffn_kernel,
        out_shape=jax.ShapeDtypeStruct((N_EXPERTS, m, D), BF16),
        grid=(N_EXPERTS, D // tf),
        in_specs=[
            pl.BlockSpec((None, m, D), lambda e, f: (e, 0, 0)),
            pl.BlockSpec((None, D, tf), lambda e, f: (e, 0, f)),
            pl.BlockSpec((None, D, tf), lambda e, f: (e, 0, f)),
            pl.BlockSpec((None, tf, D), lambda e, f: (e, f, 0)),
        ],
        out_specs=pl.BlockSpec((None, m, D), lambda e, f: (e, 0, 0)),
        scratch_shapes=[pltpu.VMEM((m, D), F32)],
        compiler_params=_cparams(("arbitrary", "arbitrary")),
        name="expert_ffn",
    )(xe, wg, wu, wd)


def _combine_kernel(x_ref, ye_ref, slot_ref, gate_ref, mod_ref, fnw_ref, o_ref, *, final_norm):
    slot = slot_ref[...]
    gate = gate_ref[...]
    s_iota = lax.broadcasted_iota(jnp.int32, (slot.shape[0], SR_SLOTS), 1)
    acc = jnp.zeros(o_ref.shape, F32)
    for e in range(N_EXPERTS):
        onehot = jnp.where(slot[:, e:e + 1] == s_iota, 1.0, 0.0).astype(BF16)
        acc = acc + gate[:, e:e + 1] * jnp.dot(onehot, ye_ref[e], preferred_element_type=F32)
    out = x_ref[...] + mod_ref[5:6, :] * acc
    if final_norm:
        out = out * lax.rsqrt(jnp.mean(out * out, axis=-1, keepdims=True) + EPS) * fnw_ref[...]
    o_ref[...] = out


def _combine(x, ye, slot_tok, gate_tok, mod_l, fnw, *, final_norm):
    tq = 512
    nq = SR_TOKENS // tq
    return pl.pallas_call(
        functools.partial(_combine_kernel, final_norm=final_norm),
        out_shape=jax.ShapeDtypeStruct((R, D), F32),
        grid=(N_SR, nq),
        in_specs=[
            pl.BlockSpec((tq, D), lambda s, q: (s * nq + q, 0)),
            pl.BlockSpec((N_EXPERTS, SR_SLOTS, D), lambda s, q: (0, s, 0)),
            pl.BlockSpec((tq, 128), lambda s, q: (s * nq + q, 0)),
            pl.BlockSpec((tq, 128), lambda s, q: (s * nq + q, 0)),
            pl.BlockSpec((None, MOD_CHUNKS, D),
                         lambda s, q: (_cond_of_tile((s * nq + q) * (tq // TM)), 0, 0)),
            pl.BlockSpec((1, D), lambda s, q: (0, 0)),
        ],
        out_specs=pl.BlockSpec((tq, D), lambda s, q: (s * nq + q, 0)),
        compiler_params=_cparams(("arbitrary", "arbitrary")),
        name="combine",
    )(x, ye, slot_tok, gate_tok, mod_l, fnw)


def _moe(x, mod_l, nw, wrT, wg, wu, wd, fnw, *, final_norm):
    hn, affT = _router(x, mod_l, nw, wrT)
    s_ctx = _select(affT, n_req=N_CTX, t=L_CTX, block0=0, cap=CAP_CTX, slot_base_period=CTX_PER_SR)
    s_lat = _select(affT, n_req=N_LAT, t=L_LAT, block0=R_CTX // L_LAT, cap=CAP_LAT, slot_base_period=0)
    slotT = jnp.concatenate([s_ctx[0], s_lat[0]], axis=1)
    slot_tok = jnp.concatenate([s_ctx[1], s_lat[1]], axis=0)
    gate_tok = jnp.concatenate([s_ctx[2], s_lat[2]], axis=0)
    xe = _gather(hn, slotT)
    ye = _ffn(xe, wg, wu, wd)
    return _combine(x, ye, slot_tok, gate_tok, mod_l, fnw, final_norm=final_norm)


def _group_major(p):
    return p.reshape(2, GROUPS, HPG).transpose(1, 0, 2).reshape(GROUPS, 2 * HPG)


def kernel(x_prompt, x_sample, state_ssm, c, c_ctx, norm1_w, norm2_w, w_mod, b_mod, conv_in_w, conv_w, conv_out_w, ssd_in_w, ssd_conv_w, ssd_conv_b, ssd_dt_bias, ssd_a_log, ssd_d, ssd_norm_w, ssd_out_w, router_w, exp_w_gate, exp_w_up, exp_w_down, final_norm_w):
    x = jnp.concatenate([x_prompt.reshape(R_CTX, D), x_sample.reshape(R_LAT, D)], axis=0)
    cond = jnp.concatenate([c_ctx[None, :], c, jnp.zeros((8 - N_COND, D), F32)], axis=0)
    mods = _modulation(cond.T, w_mod, b_mod)
    mods = mods[:, :N_COND].reshape(DEPTH, N_COND, MOD_CHUNKS, D)
    fnw = final_norm_w.reshape(1, D)

    states = []
    for layer in range(DEPTH):
        mod_l = mods[layer]
        j = layer // 2
        nw1 = norm1_w[layer].reshape(1, D)
        if layer % 2 == 0:
            x = _conv_mixer(x, mod_l, nw1, conv_in_w[j].astype(BF16), conv_w[j], conv_out_w[j].astype(BF16))
        else:
            w_in = ssd_in_w[j]
            w_dt = w_in[:, D_INNER + CONV_DIM:].reshape(D, 2, GROUPS, HPG).transpose(0, 2, 1, 3).reshape(D, 2 * HEADS)
            w_in_b = jnp.concatenate([w_in[:, :D_INNER + CONV_DIM], w_dt], axis=1).astype(BF16)
            dtb = _group_major(ssd_dt_bias[j]).reshape(1, 2 * HEADS)
            z, xs, bm, cm, dt4, dtT4 = _ssd_in(
                x, mod_l, nw1, w_in_b, w_dt.T.astype(BF16), ssd_conv_w[j], ssd_conv_b[j].reshape(1, CONV_DIM),
                dtb, dtb.reshape(2 * HEADS, 1))
            alog4 = _group_major(ssd_a_log[j])
            dskx = jnp.repeat(ssd_d[j][0] + ssd_d[j][1], HEADDIM).reshape(1, D_INNER)
            scan_args = (xs.reshape(R // CHUNK, CHUNK, D_INNER), bm.reshape(R // CHUNK, CHUNK, GROUPS * STATE),
                         cm.reshape(R // CHUNK, CHUNK, GROUPS * STATE), dt4, dtT4,
                         alog4.reshape(GROUPS, 1, 2 * HPG), alog4.reshape(GROUPS, 2 * HPG, 1), dskx)
            y_ctx, st = _ssd_scan(*scan_args, None, n_seq=N_CTX, seq_len=L_CTX, row_block0=0, emit_state=True)
            h0 = state_ssm[:, j].reshape(N_LAT, 2, HEADS * HEADDIM, STATE)
            y_lat = _ssd_scan(*scan_args, h0, n_seq=N_LAT, seq_len=L_LAT, row_block0=R_CTX // L_LAT,
                              emit_state=False)
            y = jnp.concatenate([y_ctx.reshape(R_CTX, D_INNER), y_lat.reshape(R_LAT, D_INNER)], axis=0)
            x = _ssd_out(x, y, z, mod_l, ssd_norm_w[j].reshape(1, D_INNER), ssd_out_w[j].astype(BF16))
            states.append(st.reshape(N_CTX, 2, HEADS, HEADDIM, STATE))
        x = _moe(x, mod_l, norm2_w[layer].reshape(1, D), router_w[layer].T, exp_w_gate[layer], exp_w_up[layer],
                 exp_w_down[layer], fnw, final_norm=(layer == DEPTH - 1))

    y_prompt = x[:R_CTX].reshape(N_CTX, L_CTX, D)
    y_sample = x[R_CTX:].reshape(N_LAT, L_LAT, D)
    return y_prompt, y_sample, jnp.stack(states, axis=1)
```

```python
import functools

import jax
import jax.numpy as jnp
from jax import lax
from jax.experimental import pallas as pl
from jax.experimental.pallas import tpu as pltpu

F32 = jnp.float32
BF16 = jnp.bfloat16
HIGHEST = lax.Precision.HIGHEST

D = 1024
DEPTH = 4
N_CTX, L_CTX = 16, 256
N_LAT, L_LAT = 2, 2048
GRID_W = 64
R_CTX = N_CTX * L_CTX
R_LAT = N_LAT * L_LAT
R = R_CTX + R_LAT
N_COND = 1 + N_LAT
MOD_CHUNKS = 6
D_INNER = 2048
HEADDIM = 64
HEADS = 32
GROUPS = 4
HPG = HEADS // GROUPS
GW = HPG * HEADDIM
STATE = 128
CHUNK = 128
CONV_DIM = D_INNER + 2 * GROUPS * STATE
N_SPLIT = 3
DT_LANES = N_SPLIT * 2 * HPG
N_EXPERTS = 16
EPS = 1e-6

TM = 512
CTX_TILES = R_CTX // TM
LAT_TILES_PER_REQ = L_LAT // TM
SR_TOKENS = 2048
N_SR = R // SR_TOKENS
N_SR_CTX = R_CTX // SR_TOKENS
SR_SLOTS = 2 * SR_TOKENS // N_EXPERTS
CTX_PER_SR = SR_TOKENS // L_CTX
CAP_CTX = 2 * L_CTX // N_EXPERTS
CAP_LAT = 2 * L_LAT // N_EXPERTS
VMEM_LIMIT = 56 * 1024 * 1024


def _cparams(sem):
    return pltpu.CompilerParams(dimension_semantics=sem, vmem_limit_bytes=VMEM_LIMIT)


def _cond_of_tile(i):
    return jnp.where(i < CTX_TILES, 0, 1 + (i - CTX_TILES) // LAT_TILES_PER_REQ)


def _silu(v):
    return v * (1.0 / (1.0 + jnp.exp(-v)))


def _softplus(v):
    return jnp.maximum(v, 0.0) + jnp.log1p(jnp.exp(-jnp.abs(v)))


def _norm_mod(x, nw, scale, shift):
    y = x * lax.rsqrt(jnp.mean(x * x, axis=-1, keepdims=True) + EPS)
    return y * nw * (1.0 + scale) + shift


def _conv3_rows(p, w_ref, tile_idx):
    n = p.shape[0]
    period = jnp.where(tile_idx < CTX_TILES, L_CTX, GRID_W)
    r = lax.broadcasted_iota(jnp.int32, (n, 1), 0) & (period - 1)
    prev = jnp.where(r == 0, 0.0, pltpu.roll(p, 1, axis=0))
    nxt = jnp.where(r == period - 1, 0.0, pltpu.roll(p, n - 1, axis=0))
    return prev * w_ref[0:1, :] + p * w_ref[1:2, :] + nxt * w_ref[2:3, :]


def _split_pieces(v):
    pieces = []
    r = v
    for _ in range(N_SPLIT):
        p = r.astype(BF16)
        pieces.append(p)
        r = r - p.astype(F32)
    return pieces


def _modulation_kernel(condT_ref, w_ref, b_ref, o_ref):
    s = _silu(condT_ref[...])
    w = w_ref[...]
    rows = [jnp.sum(w * s[:, r:r + 1], axis=0, keepdims=True) + b_ref[...] for r in range(N_COND)]
    rows.append(jnp.zeros((8 - N_COND, w.shape[1]), F32))
    o_ref[...] = jnp.concatenate(rows, axis=0)


def _modulation(condT, w_mod, b_mod):
    tn = 1536
    n = MOD_CHUNKS * D
    return pl.pallas_call(
        _modulation_kernel,
        out_shape=jax.ShapeDtypeStruct((DEPTH, 8, n), F32),
        grid=(DEPTH, n // tn),
        in_specs=[
            pl.BlockSpec((D, 8), lambda l, j: (0, 0)),
            pl.BlockSpec((None, D, tn), lambda l, j: (l, 0, j)),
            pl.BlockSpec((None, 1, tn), lambda l, j: (l, 0, j)),
        ],
        out_specs=pl.BlockSpec((None, 8, tn), lambda l, j: (l, 0, j)),
        compiler_params=_cparams(("arbitrary", "arbitrary")),
        name="modulation",
    )(condT, w_mod, b_mod.reshape(DEPTH, 1, n))


def _conv_mixer_kernel(x_ref, mod_ref, nw_ref, win_ref, cw_ref, wout_ref, o_ref):
    i = pl.program_id(0)
    x = x_ref[...]
    hn = _norm_mod(x, nw_ref[...], mod_ref[1:2, :], mod_ref[0:1, :]).astype(BF16)
    gb = jnp.dot(hn, win_ref[:, 0:D], preferred_element_type=F32)
    gc = jnp.dot(hn, win_ref[:, D:2 * D], preferred_element_type=F32)
    v = jnp.dot(hn, win_ref[:, 2 * D:3 * D], preferred_element_type=F32)
    q = (gb * _conv3_rows(gc * v, cw_ref, i)).astype(BF16)
    mix = jnp.dot(q, wout_ref[...], preferred_element_type=F32)
    o_ref[...] = x + mod_ref[2:3, :] * mix


def _conv_mixer(x, mod_l, nw, w_in, cw, w_out):
    return pl.pallas_call(
        _conv_mixer_kernel,
        out_shape=jax.ShapeDtypeStruct((R, D), F32),
        grid=(R // TM,),
        in_specs=[
            pl.BlockSpec((TM, D), lambda i: (i, 0)),
            pl.BlockSpec((None, MOD_CHUNKS, D), lambda i: (_cond_of_tile(i), 0, 0)),
            pl.BlockSpec((1, D), lambda i: (0, 0)),
            pl.BlockSpec((D, 3 * D), lambda i: (0, 0)),
            pl.BlockSpec((3, D), lambda i: (0, 0)),
            pl.BlockSpec((D, D), lambda i: (0, 0)),
        ],
        out_specs=pl.BlockSpec((TM, D), lambda i: (i, 0)),
        compiler_params=_cparams(("arbitrary",)),
        name="conv_mixer",
    )(x, mod_l, nw, w_in, cw, w_out)


def _ssd_in_kernel(x_ref, mod_ref, nw_ref, win_ref, wdtT_ref, cw_ref, cb_ref, dtb_ref, dtbT_ref,
                   z_ref, xs_ref, bm_ref, cm_ref, dt_ref, dtT_ref):
    i = pl.program_id(0)
    hn = _norm_mod(x_ref[...], nw_ref[...], mod_ref[1:2, :], mod_ref[0:1, :]).astype(BF16)
    for k in range(D_INNER // D):
        z_ref[:, k * D:(k + 1) * D] = jnp.dot(hn, win_ref[:, k * D:(k + 1) * D], preferred_element_type=F32)
    for k in range(CONV_DIM // D):
        lo = D_INNER + k * D
        u = jnp.dot(hn, win_ref[:, lo:lo + D], preferred_element_type=F32)
        u = _silu(_conv3_rows(u, cw_ref.at[:, k * D:(k + 1) * D], i) + cb_ref[:, k * D:(k + 1) * D])
        if k < D_INNER // D:
            xs_ref[:, k * D:(k + 1) * D] = u
        else:
            bm_ref[...] = u[:, :GROUPS * STATE]
            cm_ref[...] = u[:, GROUPS * STATE:]
    lo = D_INNER + CONV_DIM
    dt = _softplus(jnp.dot(hn, win_ref[:, lo:lo + GROUPS * DT_LANES], preferred_element_type=F32) + dtb_ref[...])
    dtT = _softplus(
        lax.dot_general(wdtT_ref[...], hn, (((1,), (1,)), ((), ())), preferred_element_type=F32) + dtbT_ref[...])
    for g in range(GROUPS):
        dt_ref[g] = dt[:, g * DT_LANES:(g + 1) * DT_LANES].reshape(TM // CHUNK, CHUNK, DT_LANES)
        for k in range(TM // CHUNK):
            dtT_ref[g, k] = dtT[g * 2 * HPG:(g + 1) * 2 * HPG, k * CHUNK:(k + 1) * CHUNK]


def _ssd_in(x, mod_l, nw, w_in, w_dtT, cw, cb, dtb, dtbT):
    n_in = w_in.shape[1]
    nck = TM // CHUNK
    return pl.pallas_call(
        _ssd_in_kernel,
        out_shape=(
            jax.ShapeDtypeStruct((R, D_INNER), F32),
            jax.ShapeDtypeStruct((R, D_INNER), F32),
            jax.ShapeDtypeStruct((R, GROUPS * STATE), F32),
            jax.ShapeDtypeStruct((R, GROUPS * STATE), F32),
            jax.ShapeDtypeStruct((GROUPS, R // CHUNK, CHUNK, DT_LANES), F32),
            jax.ShapeDtypeStruct((GROUPS, R // CHUNK, 2 * HPG, CHUNK), F32),
        ),
        grid=(R // TM,),
        in_specs=[
            pl.BlockSpec((TM, D), lambda i: (i, 0)),
            pl.BlockSpec((None, MOD_CHUNKS, D), lambda i: (_cond_of_tile(i), 0, 0)),
            pl.BlockSpec((1, D), lambda i: (0, 0)),
            pl.BlockSpec((D, n_in), lambda i: (0, 0)),
            pl.BlockSpec((2 * HEADS, D), lambda i: (0, 0)),
            pl.BlockSpec((3, CONV_DIM), lambda i: (0, 0)),
            pl.BlockSpec((1, CONV_DIM), lambda i: (0, 0)),
            pl.BlockSpec((1, GROUPS * DT_LANES), lambda i: (0, 0)),
            pl.BlockSpec((2 * HEADS, 1), lambda i: (0, 0)),
        ],
        out_specs=(
            pl.BlockSpec((TM, D_INNER), lambda i: (i, 0)),
            pl.BlockSpec((TM, D_INNER), lambda i: (i, 0)),
            pl.BlockSpec((TM, GROUPS * STATE), lambda i: (i, 0)),
            pl.BlockSpec((TM, GROUPS * STATE), lambda i: (i, 0)),
            pl.BlockSpec((GROUPS, nck, CHUNK, DT_LANES), lambda i: (0, i, 0, 0)),
            pl.BlockSpec((GROUPS, nck, 2 * HPG, CHUNK), lambda i: (0, i, 0, 0)),
        ),
        compiler_params=_cparams(("arbitrary",)),
        name="ssd_in",
    )(x, mod_l, nw, w_in, w_dtT, cw, cb, dtb, dtbT)


def _ssd_scan_kernel(*refs, n_chunks, has_h0, has_prev, emit_state):
    xs_ref, bm_ref, cm_ref, dt_ref, dtT_ref, alog_ref, alogT_ref, dsk_ref = refs[:8]
    k = 8
    h0_ref = None
    if has_h0:
        h0_ref = refs[k]
        k += 1
    if has_prev:
        k += 1
    y_ref = refs[k]
    k += 1
    st_out_ref = None
    if emit_state:
        st_out_ref = refs[k]
        k += 1
    st_ref, acs_ref = refs[k], refs[k + 1]

    a_row = -jnp.exp(alog_ref[...])
    a_col = -jnp.exp(alogT_ref[...])
    qi = lax.broadcasted_iota(jnp.int32, (CHUNK, CHUNK), 0)
    si = lax.broadcasted_iota(jnp.int32, (CHUNK, CHUNK), 1)
    lower = si <= qi
    upper = si >= qi
    tri_lo = jnp.where(lower, 1.0, 0.0).astype(BF16)
    tri_up = jnp.where(upper, 1.0, 0.0).astype(BF16)
    tri_lo_k = jnp.concatenate([tri_lo] * N_SPLIT, axis=1)
    tri_up_k = jnp.concatenate([tri_up] * N_SPLIT, axis=1)
    tri_lo_r = jnp.concatenate([tri_lo] * N_SPLIT, axis=0)
    tri_up_r = jnp.concatenate([tri_up] * N_SPLIT, axis=0)
    lane = lax.broadcasted_iota(jnp.int32, (1, DT_LANES), 1)
    piece_of_lane = lane // (2 * HPG)
    eh = lax.broadcasted_iota(jnp.int32, (DT_LANES, GW), 0) % (2 * HPG)
    ej = lax.broadcasted_iota(jnp.int32, (DT_LANES, GW), 1) // HEADDIM
    exp_f = jnp.where(eh == ej, 1.0, 0.0).astype(BF16)
    exp_b = jnp.where(eh == ej + HPG, 1.0, 0.0).astype(BF16)
    lane_c = lax.broadcasted_iota(jnp.int32, (1, CHUNK), 1)
    first_half = lane_c < HEADDIM

    def lane_pieces(v):
        pieces = _split_pieces(v)
        out = pieces[N_SPLIT - 1]
        for r in range(N_SPLIT - 2, -1, -1):
            out = jnp.where(piece_of_lane == r, pieces[r], out)
        return out

    def expand(v, e):
        return jnp.dot(lane_pieces(v), e, preferred_element_type=F32)

    for d in range(2):
        if has_h0:
            st_ref[d] = h0_ref[d].T
        else:
            st_ref[d] = jnp.zeros((STATE, GW), F32)

    def fwd_body(c, carry):
        x = xs_ref[c]
        b = bm_ref[c]
        cm = cm_ref[c].astype(BF16)
        dt = dt_ref[c]
        dtT = dtT_ref[c]
        dta_rows = jnp.concatenate(_split_pieces(dt * a_row), axis=0)
        dtaT_lanes = jnp.concatenate(_split_pieces(dtT * a_col), axis=1)
        acs_lo = jnp.dot(tri_lo_k, dta_rows, preferred_element_type=F32)
        acs_up = jnp.dot(tri_up_k, dta_rows, preferred_element_type=F32)
        acsT_lo = jnp.dot(dtaT_lanes, tri_up_r, preferred_element_type=F32)
        acsT_up = jnp.dot(dtaT_lanes, tri_lo_r, preferred_element_type=F32)
        acs_ref[c] = acs_up
        cb = lax.dot_general(cm, b.astype(BF16), (((1,), (1,)), ((), ())), preferred_element_type=F32)
        parts = []
        for hp in range(HPG // 2):
            ws = []
            for h in (2 * hp, 2 * hp + 1):
                lf = jnp.exp(jnp.where(lower, acs_lo[:, h:h + 1] - acsT_lo[h:h + 1, :], -jnp.inf))
                lb = jnp.exp(jnp.where(upper, acs_up[:, HPG + h:HPG + h + 1] - acsT_up[HPG + h:HPG + h + 1, :],
                                       -jnp.inf))
                ws.append((cb * (lf * dtT[h:h + 1, :] + lb * dtT[HPG + h:HPG + h + 1, :])).astype(BF16))
            xp = x[:, hp * CHUNK:(hp + 1) * CHUNK]
            x2 = jnp.concatenate([jnp.where(first_half, xp, 0.0), jnp.where(first_half, 0.0, xp)], axis=0)
            parts.append(jnp.dot(jnp.concatenate(ws, axis=1), x2.astype(BF16), preferred_element_type=F32))
        y = jnp.concatenate(parts, axis=1) + dsk_ref[...] * x
        st = st_ref[0]
        y = y + jnp.dot(cm, st.astype(BF16), preferred_element_type=F32) * expand(jnp.exp(acs_lo), exp_f)
        y_ref[c] = y
        a_last = acs_lo[CHUNK - 1:CHUNK, :]
        xw = x * expand(jnp.exp(a_last - acs_lo) * dt, exp_f)
        st_ref[0] = st * expand(jnp.exp(jnp.broadcast_to(a_last, (8, DT_LANES))), exp_f)[0:1, :] + jnp.dot(
            b.T.astype(BF16), xw.astype(BF16), preferred_element_type=F32)
        return carry

    def bwd_body(k_, carry):
        c = n_chunks - 1 - k_
        x = xs_ref[c]
        b = bm_ref[c]
        cm = cm_ref[c].astype(BF16)
        dt = dt_ref[c]
        acs_up = acs_ref[c]
        st = st_ref[1]
        y_off = jnp.dot(cm, st.astype(BF16), preferred_element_type=F32) * expand(jnp.exp(acs_up), exp_b)
        y_ref[c] = y_ref[c] + y_off
        a_tot = acs_up[0:1, :]
        xw = x * expand(jnp.exp(a_tot - acs_up) * dt, exp_b)
        st_ref[1] = st * expand(jnp.exp(jnp.broadcast_to(a_tot, (8, DT_LANES))), exp_b)[0:1, :] + jnp.dot(
            b.T.astype(BF16), xw.astype(BF16), preferred_element_type=F32)
        return carry

    lax.fori_loop(0, n_chunks, fwd_body, 0)
    lax.fori_loop(0, n_chunks, bwd_body, 0)
    if emit_state:
        for d in range(2):
            st_out_ref[d] = st_ref[d].T


def _ssd_scan(xs3, bm3, cm3, dt4, dtT4, alog4, alogT4, dskx, h0, y_prev, *, n_seq, seq_len, row_block0, emit_state):
    nck = seq_len // CHUNK
    has_h0 = h0 is not None
    has_prev = y_prev is not None
    rb = lambda s: s + row_block0
    in_specs = [
        pl.BlockSpec((nck, CHUNK, GW), lambda s, g: (rb(s), 0, g)),
        pl.BlockSpec((nck, CHUNK, STATE), lambda s, g: (rb(s), 0, g)),
        pl.BlockSpec((nck, CHUNK, STATE), lambda s, g: (rb(s), 0, g)),
        pl.BlockSpec((None, nck, CHUNK, DT_LANES), lambda s, g: (g, rb(s), 0, 0)),
        pl.BlockSpec((None, nck, 2 * HPG, CHUNK), lambda s, g: (g, rb(s), 0, 0)),
        pl.BlockSpec((None, 1, DT_LANES), lambda s, g: (g, 0, 0)),
        pl.BlockSpec((None, 2 * HPG, 1), lambda s, g: (g, 0, 0)),
        pl.BlockSpec((1, GW), lambda s, g: (0, g)),
    ]
    args = [xs3, bm3, cm3, dt4, dtT4, alog4, alogT4, dskx]
    if has_h0:
        in_specs.append(pl.BlockSpec((None, 2, GW, STATE), lambda s, g: (s, 0, g, 0)))
        args.append(h0)
    aliases = {}
    if has_prev:
        in_specs.append(pl.BlockSpec(memory_space=pl.ANY))
        aliases = {len(args): 0}
        args.append(y_prev)
    y_shape = jax.ShapeDtypeStruct((R // CHUNK, CHUNK, D_INNER), F32)
    y_spec = pl.BlockSpec((nck, CHUNK, GW), lambda s, g: (rb(s), 0, g))
    if emit_state:
        out_shape = (y_shape, jax.ShapeDtypeStruct((n_seq, 2, HEADS * HEADDIM, STATE), F32))
        out_specs = (y_spec, pl.BlockSpec((None, 2, GW, STATE), lambda s, g: (s, 0, g, 0)))
    else:
        out_shape, out_specs = y_shape, y_spec
    return pl.pallas_call(
        functools.partial(_ssd_scan_kernel, n_chunks=nck, has_h0=has_h0, has_prev=has_prev, emit_state=emit_state),
        out_shape=out_shape,
        grid=(n_seq, GROUPS),
        in_specs=in_specs,
        out_specs=out_specs,
        scratch_shapes=[pltpu.VMEM((2, STATE, GW), F32), pltpu.VMEM((nck, CHUNK, DT_LANES), F32)],
        input_output_aliases=aliases,
        compiler_params=_cparams(("arbitrary", "arbitrary")),
        name="ssd_scan",
    )(*args)


def _ssd_out_kernel(x_ref, y_ref, z_ref, mod_ref, nw_ref, wout_ref, o_ref):
    v = y_ref[...] * _silu(z_ref[...])
    v = v * lax.rsqrt(jnp.mean(v * v, axis=-1, keepdims=True) + EPS) * nw_ref[...]
    mix = jnp.dot(v.astype(BF16), wout_ref[...], preferred_element_type=F32)
    o_ref[...] = x_ref[...] + mod_ref[2:3, :] * mix


def _ssd_out(x, y, z, mod_l, nw, w_out):
    return pl.pallas_call(
        _ssd_out_kernel,
        out_shape=jax.ShapeDtypeStruct((R, D), F32),
        grid=(R // TM,),
        in_specs=[
            pl.BlockSpec((TM, D), lambda i: (i, 0)),
            pl.BlockSpec((TM, D_INNER), lambda i: (i, 0)),
            pl.BlockSpec((TM, D_INNER), lambda i: (i, 0)),
            pl.BlockSpec((None, MOD_CHUNKS, D), lambda i: (_cond_of_tile(i), 0, 0)),
            pl.BlockSpec((1, D_INNER), lambda i: (0, 0)),
            pl.BlockSpec((D_INNER, D), lambda i: (0, 0)),
        ],
        out_specs=pl.BlockSpec((TM, D), lambda i: (i, 0)),
        compiler_params=_cparams(("arbitrary",)),
        name="ssd_out",
    )(x, y, z, mod_l, nw, w_out)


def _router_kernel(x_ref, mod_ref, nw_ref, wrT_ref, hn_ref, affT_ref):
    hn = _norm_mod(x_ref[...], nw_ref[...], mod_ref[4:5, :], mod_ref[3:4, :])
    hn_ref[...] = hn.astype(BF16)
    logits = lax.dot_general(wrT_ref[...], hn, (((1,), (1,)), ((), ())),
                             precision=HIGHEST, preferred_element_type=F32)
    e = jnp.exp(logits - jnp.max(logits, axis=0, keepdims=True))
    affT_ref[...] = e / jnp.sum(e, axis=0, keepdims=True)


def _router(x, mod_l, nw, wrT):
    return pl.pallas_call(
        _router_kernel,
        out_shape=(jax.ShapeDtypeStruct((R, D), BF16), jax.ShapeDtypeStruct((N_EXPERTS, R), F32)),
        grid=(R // TM,),
        in_specs=[
            pl.BlockSpec((TM, D), lambda i: (i, 0)),
            pl.BlockSpec((None, MOD_CHUNKS, D), lambda i: (_cond_of_tile(i), 0, 0)),
            pl.BlockSpec((1, D), lambda i: (0, 0)),
            pl.BlockSpec((N_EXPERTS, D), lambda i: (0, 0)),
        ],
        out_specs=(pl.BlockSpec((TM, D), lambda i: (i, 0)), pl.BlockSpec((N_EXPERTS, TM), lambda i: (0, i))),
        compiler_params=_cparams(("arbitrary",)),
        name="router",
    )(x, mod_l, nw, wrT)


def _excl_cumsum_lanes(m):
    blk = 256
    t = m.shape[1]
    a = lax.broadcasted_iota(jnp.int32, (blk, blk), 0)
    b = lax.broadcasted_iota(jnp.int32, (blk, blk), 1)
    strict = jnp.where(a < b, 1.0, 0.0).astype(BF16)
    carry = jnp.zeros((m.shape[0], 1), F32)
    outs = []
    for k in range(t // blk):
        mk = m[:, k * blk:(k + 1) * blk]
        outs.append(jnp.dot(mk.astype(BF16), strict, preferred_element_type=F32) + carry)
        carry = carry + jnp.sum(mk, axis=1, keepdims=True)
    return outs[0] if len(outs) == 1 else jnp.concatenate(outs, axis=1)


def _select_request(aff, cap, base):
    bits = pltpu.bitcast(aff, jnp.int32)
    thr = jnp.zeros((N_EXPERTS, 1), jnp.int32)
    for k in range(30, -1, -1):
        trial = thr | (1 << k)
        cnt = jnp.sum(jnp.where(bits >= trial, 1.0, 0.0), axis=1, keepdims=True)
        thr = jnp.where(cnt >= cap, trial, thr)
    gt = bits > thr
    eq = jnp.where(bits == thr, 1.0, 0.0)
    need = cap - jnp.sum(jnp.where(gt, 1.0, 0.0), axis=1, keepdims=True)
    sel = gt | ((eq > 0.0) & (_excl_cumsum_lanes(eq) < need))
    pos = _excl_cumsum_lanes(jnp.where(sel, 1.0, 0.0))
    return jnp.where(sel, pos + base, -1.0), jnp.where(sel, aff, 0.0)


def _select_kernel(affT_ref, slotT_ref, slot_tok_ref, gate_tok_ref):
    s = pl.program_id(0)

    def emit(slot, gate):
        slotT_ref[...] = slot.astype(jnp.int32)
        pad = jnp.zeros((128 - N_EXPERTS, SR_TOKENS), F32)
        slot_tok_ref[...] = jnp.concatenate([slot, pad - 1.0], axis=0).T.astype(jnp.int32)
        gate_tok_ref[...] = jnp.concatenate([gate, pad], axis=0).T

    @pl.when(s < N_SR_CTX)
    def _():
        outs = [_select_request(affT_ref[:, r * L_CTX:(r + 1) * L_CTX], CAP_CTX, float(r * CAP_CTX))
                for r in range(CTX_PER_SR)]
        emit(jnp.concatenate([o[0] for o in outs], axis=1), jnp.concatenate([o[1] for o in outs], axis=1))

    @pl.when(s >= N_SR_CTX)
    def _():
        emit(*_select_request(affT_ref[...], CAP_LAT, 0.0))


def _select(affT):
    return pl.pallas_call(
        _select_kernel,
        out_shape=(
            jax.ShapeDtypeStruct((N_EXPERTS, R), jnp.int32),
            jax.ShapeDtypeStruct((R, 128), jnp.int32),
            jax.ShapeDtypeStruct((R, 128), F32),
        ),
        grid=(N_SR,),
        in_specs=[pl.BlockSpec((N_EXPERTS, SR_TOKENS), lambda s: (0, s))],
        out_specs=(
            pl.BlockSpec((N_EXPERTS, SR_TOKENS), lambda s: (0, s)),
            pl.BlockSpec((SR_TOKENS, 128), lambda s: (s, 0)),
            pl.BlockSpec((SR_TOKENS, 128), lambda s: (s, 0)),
        ),
        compiler_params=_cparams(("arbitrary",)),
        name="select",
    )(affT)


def _gather_kernel(hn_ref, slotT_ref, xe_ref):
    e = pl.program_id(1)
    slot_row = slotT_ref[pl.ds(e, 1), :]
    s_iota = lax.broadcasted_iota(jnp.int32, (SR_SLOTS, SR_TOKENS), 0)
    onehot = jnp.where(s_iota == slot_row, 1.0, 0.0).astype(BF16)
    xe_ref[...] = jnp.dot(onehot, hn_ref[...], preferred_element_type=F32).astype(BF16)


def _gather(hn, slotT):
    return pl.pallas_call(
        _gather_kernel,
        out_shape=jax.ShapeDtypeStruct((N_EXPERTS, N_SR * SR_SLOTS, D), BF16),
        grid=(N_SR, N_EXPERTS),
        in_specs=[
            pl.BlockSpec((SR_TOKENS, D), lambda s, e: (s, 0)),
            pl.BlockSpec((N_EXPERTS, SR_TOKENS), lambda s, e: (0, s)),
        ],
        out_specs=pl.BlockSpec((None, SR_SLOTS, D), lambda s, e: (e, s, 0)),
        compiler_params=_cparams(("arbitrary", "arbitrary")),
        name="gather",
    )(hn, slotT)


def _ffn_kernel(xe_ref, wg_ref, wu_ref, wd_ref, ye_ref, acc_ref):
    f = pl.program_id(1)
    xe = xe_ref[...]
    hg = jnp.dot(xe, wg_ref[...].astype(BF16), preferred_element_type=F32)
    hu = jnp.dot(xe, wu_ref[...].astype(BF16), preferred_element_type=F32)
    hid = (_silu(hg) * hu).astype(BF16)
    part = jnp.dot(hid, wd_ref[...].astype(BF16), preferred_element_type=F32)

    @pl.when(f == 0)
    def _():
        acc_ref[...] = part

    @pl.when(f > 0)
    def _():
        acc_ref[...] = acc_ref[...] + part

    @pl.when(f == pl.num_programs(1) - 1)
    def _():
        ye_ref[...] = acc_ref[...].astype(BF16)


def _ffn(xe, wg, wu, wd, layer):
    tf = 512
    m = xe.shape[1]
    return pl.pallas_call(
        _ffn_kernel,
        out_shape=jax.ShapeDtypeStruct((N_EXPERTS, m, D), BF16),
        grid=(N_EXPERTS, D // tf),
        in_specs=[
            pl.BlockSpec((None, m, D), lambda e, f: (e, 0, 0)),
            pl.BlockSpec((None, None, D, tf), lambda e, f: (layer, e, 0, f)),
            pl.BlockSpec((None, None, D, tf), lambda e, f: (layer, e, 0, f)),
            pl.BlockSpec((None, None, tf, D), lambda e, f: (layer, e, f, 0)),
        ],
        out_specs=pl.BlockSpec((None, m, D), lambda e, f: (e, 0, 0)),
        scratch_shapes=[pltpu.VMEM((m, D), F32)],
        compiler_params=_cparams(("arbitrary", "arbitrary")),
        name="expert_ffn",
    )(xe, wg, wu, wd)


def _combine_kernel(x_ref, ye_ref, slot_ref, gate_ref, mod_ref, fnw_ref, o_ref, *, final_norm):
    slot = slot_ref[...]
    gate = gate_ref[...]
    s_iota = lax.broadcasted_iota(jnp.int32, (slot.shape[0], SR_SLOTS), 1)
    acc = jnp.zeros(o_ref.shape, F32)
    for e in range(N_EXPERTS):
        onehot = jnp.where(slot[:, e:e + 1] == s_iota, 1.0, 0.0).astype(BF16)
        acc = acc + gate[:, e:e + 1] * jnp.dot(onehot, ye_ref[e], preferred_element_type=F32)
    out = x_ref[...] + mod_ref[5:6, :] * acc
    if final_norm:
        out = out * lax.rsqrt(jnp.mean(out * out, axis=-1, keepdims=True) + EPS) * fnw_ref[...]
    o_ref[...] = out


def _combine(x, ye, slot_tok, gate_tok, mod_l, fnw, *, final_norm):
    tq = 512
    nq = SR_TOKENS // tq
    return pl.pallas_call(
        functools.partial(_combine_kernel, final_norm=final_norm),
        out_shape=jax.ShapeDtypeStruct((R, D), F32),
        grid=(N_SR, nq),
        in_specs=[
            pl.BlockSpec((tq, D), lambda s, q: (s * nq + q, 0)),
            pl.BlockSpec((N_EXPERTS, SR_SLOTS, D), lambda s, q: (0, s, 0)),
            pl.BlockSpec((tq, 128), lambda s, q: (s * nq + q, 0)),
            pl.BlockSpec((tq, 128), lambda s, q: (s * nq + q, 0)),
            pl.BlockSpec((None, MOD_CHUNKS, D),
                         lambda s, q: (_cond_of_tile((s * nq + q) * (tq // TM)), 0, 0)),
            pl.BlockSpec((1, D), lambda s, q: (0, 0)),
        ],
        out_specs=pl.BlockSpec((tq, D), lambda s, q: (s * nq + q, 0)),
        compiler_params=_cparams(("arbitrary", "arbitrary")),
        name="combine",
    )(x, ye, slot_tok, gate_tok, mod_l, fnw)


def _moe(x, mod_l, nw, wrT, wg, wu, wd, fnw, layer, *, final_norm):
    hn, affT = _router(x, mod_l, nw, wrT)
    slotT, slot_tok, gate_tok = _select(affT)
    xe = _gather(hn, slotT)
    ye = _ffn(xe, wg, wu, wd, layer)
    return _combine(x, ye, slot_tok, gate_tok, mod_l, fnw, final_norm=final_norm)


def _group_major(p):
    return p.reshape(2, GROUPS, HPG).transpose(1, 0, 2).reshape(GROUPS, 2 * HPG)


def kernel(x_prompt, x_sample, state_ssm, c, c_ctx, norm1_w, norm2_w, w_mod, b_mod, conv_in_w, conv_w, conv_out_w, ssd_in_w, ssd_conv_w, ssd_conv_b, ssd_dt_bias, ssd_a_log, ssd_d, ssd_norm_w, ssd_out_w, router_w, exp_w_gate, exp_w_up, exp_w_down, final_norm_w):
    x = jnp.concatenate([x_prompt.reshape(R_CTX, D), x_sample.reshape(R_LAT, D)], axis=0)
    cond = jnp.concatenate([c_ctx[None, :], c, jnp.zeros((8 - N_COND, D), F32)], axis=0)
    mods = _modulation(cond.T, w_mod, b_mod)
    mods = mods[:, :N_COND].reshape(DEPTH, N_COND, MOD_CHUNKS, D)
    fnw = final_norm_w.reshape(1, D)

    states = []
    for layer in range(DEPTH):
        mod_l = mods[layer]
        j = layer // 2
        nw1 = norm1_w[layer].reshape(1, D)
        if layer % 2 == 0:
            x = _conv_mixer(x, mod_l, nw1, conv_in_w[j].astype(BF16), conv_w[j], conv_out_w[j].astype(BF16))
        else:
            w_in = ssd_in_w[j]
            w_dt = w_in[:, D_INNER + CONV_DIM:].reshape(D, 2, GROUPS, HPG).transpose(0, 2, 1, 3)
            w_dt_rep = jnp.broadcast_to(w_dt.reshape(D, GROUPS, 1, 2 * HPG), (D, GROUPS, N_SPLIT, 2 * HPG))
            w_in_b = jnp.concatenate([w_in[:, :D_INNER + CONV_DIM], w_dt_rep.reshape(D, GROUPS * DT_LANES)],
                                     axis=1).astype(BF16)
            dtb = _group_major(ssd_dt_bias[j])
            dtb_rep = jnp.tile(dtb, (1, N_SPLIT)).reshape(1, GROUPS * DT_LANES)
            z, xs, bm, cm, dt4, dtT4 = _ssd_in(
                x, mod_l, nw1, w_in_b, w_dt.reshape(D, 2 * HEADS).T.astype(BF16), ssd_conv_w[j],
                ssd_conv_b[j].reshape(1, CONV_DIM), dtb_rep, dtb.reshape(2 * HEADS, 1))
            alog4 = _group_major(ssd_a_log[j])
            dskx = jnp.repeat(ssd_d[j][0] + ssd_d[j][1], HEADDIM).reshape(1, D_INNER)
            scan_args = (xs.reshape(R // CHUNK, CHUNK, D_INNER), bm.reshape(R // CHUNK, CHUNK, GROUPS * STATE),
                         cm.reshape(R // CHUNK, CHUNK, GROUPS * STATE), dt4, dtT4,
                         jnp.tile(alog4, (1, N_SPLIT)).reshape(GROUPS, 1, DT_LANES),
                         alog4.reshape(GROUPS, 2 * HPG, 1), dskx)
            y, st = _ssd_scan(*scan_args, None, None, n_seq=N_CTX, seq_len=L_CTX, row_block0=0, emit_state=True)
            h0 = state_ssm[:, j].reshape(N_LAT, 2, HEADS * HEADDIM, STATE)
            y = _ssd_scan(*scan_args, h0, y, n_seq=N_LAT, seq_len=L_LAT, row_block0=R_CTX // L_LAT,
                          emit_state=False)
            x = _ssd_out(x, y.reshape(R, D_INNER), z, mod_l, ssd_norm_w[j].reshape(1, D_INNER),
                         ssd_out_w[j].astype(BF16))
            states.append(st.reshape(N_CTX, 2, HEADS, HEADDIM, STATE))
        x = _moe(x, mod_l, norm2_w[layer].reshape(1, D), router_w[layer].T, exp_w_gate, exp_w_up, exp_w_down, fnw,
                 layer, final_norm=(layer == DEPTH - 1))

    y_prompt = x[:R_CTX].reshape(N_CTX, L_CTX, D)
    y_sample = x[R_CTX:].reshape(N_LAT, L_LAT, D)
    return y_prompt, y_sample, jnp.stack(states, axis=1)
```

```python
import functools

import jax
import jax.numpy as jnp
from jax import lax
from jax.experimental import pallas as pl
from jax.experimental.pallas import tpu as pltpu

F32 = jnp.float32
BF16 = jnp.bfloat16
HIGHEST = lax.Precision.HIGHEST
LOG2E = 1.4426950408889634

D = 1024
DEPTH = 4
N_CTX, L_CTX = 16, 256
N_LAT, L_LAT = 2, 2048
GRID_W = 64
R_CTX = N_CTX * L_CTX
R_LAT = N_LAT * L_LAT
R = R_CTX + R_LAT
N_COND = 1 + N_LAT
MOD_CHUNKS = 6
D_INNER = 2048
HEADDIM = 64
HEADS = 32
GROUPS = 4
HPG = HEADS // GROUPS
GW = HPG * HEADDIM
STATE = 128
CHUNK = 128
CONV_DIM = D_INNER + 2 * GROUPS * STATE
N_SPLIT = 3
DT_LANES = N_SPLIT * 2 * HPG
N_EXPERTS = 16
EPS = 1e-6

TM = 512
CTX_TILES = R_CTX // TM
LAT_TILES_PER_REQ = L_LAT // TM
SR_TOKENS = 2048
N_SR = R // SR_TOKENS
N_SR_CTX = R_CTX // SR_TOKENS
SR_SLOTS = 2 * SR_TOKENS // N_EXPERTS
CTX_PER_SR = SR_TOKENS // L_CTX
CAP_CTX = 2 * L_CTX // N_EXPERTS
CAP_LAT = 2 * L_LAT // N_EXPERTS
VMEM_LIMIT = 56 * 1024 * 1024


def _cparams(sem):
    return pltpu.CompilerParams(dimension_semantics=sem, vmem_limit_bytes=VMEM_LIMIT)


def _cond_of_tile(i):
    return jnp.where(i < CTX_TILES, 0, 1 + (i - CTX_TILES) // LAT_TILES_PER_REQ)


def _silu(v):
    return v * (1.0 / (1.0 + jnp.exp(-v)))


def _softplus(v):
    return jnp.maximum(v, 0.0) + jnp.log1p(jnp.exp(-jnp.abs(v)))


def _norm_mod(x, nw, scale, shift):
    y = x * lax.rsqrt(jnp.mean(x * x, axis=-1, keepdims=True) + EPS)
    return y * nw * (1.0 + scale) + shift


def _conv3_rows(p, w_ref, tile_idx):
    n = p.shape[0]
    period = jnp.where(tile_idx < CTX_TILES, L_CTX, GRID_W)
    r = lax.broadcasted_iota(jnp.int32, (n, 1), 0) & (period - 1)
    prev = jnp.where(r == 0, 0.0, pltpu.roll(p, 1, axis=0))
    nxt = jnp.where(r == period - 1, 0.0, pltpu.roll(p, n - 1, axis=0))
    return prev * w_ref[0:1, :] + p * w_ref[1:2, :] + nxt * w_ref[2:3, :]


def _split_pieces(v):
    pieces = []
    r = v
    for _ in range(N_SPLIT):
        p = r.astype(BF16)
        pieces.append(p)
        r = r - p.astype(F32)
    return pieces


def _modulation_kernel(condT_ref, w_ref, b_ref, o_ref):
    s = _silu(condT_ref[...])
    w = w_ref[...]
    rows = [jnp.sum(w * s[:, r:r + 1], axis=0, keepdims=True) + b_ref[...] for r in range(N_COND)]
    rows.append(jnp.zeros((8 - N_COND, w.shape[1]), F32))
    o_ref[...] = jnp.concatenate(rows, axis=0)


def _modulation(condT, w_mod, b_mod):
    tn = 1536
    n = MOD_CHUNKS * D
    return pl.pallas_call(
        _modulation_kernel,
        out_shape=jax.ShapeDtypeStruct((DEPTH, 8, n), F32),
        grid=(DEPTH, n // tn),
        in_specs=[
            pl.BlockSpec((D, 8), lambda l, j: (0, 0)),
            pl.BlockSpec((None, D, tn), lambda l, j: (l, 0, j)),
            pl.BlockSpec((None, 1, tn), lambda l, j: (l, 0, j)),
        ],
        out_specs=pl.BlockSpec((None, 8, tn), lambda l, j: (l, 0, j)),
        compiler_params=_cparams(("arbitrary", "arbitrary")),
        name="modulation",
    )(condT, w_mod, b_mod.reshape(DEPTH, 1, n))


def _conv_mixer_kernel(x_ref, mod_ref, nw_ref, win_ref, cw_ref, wout_ref, o_ref):
    i = pl.program_id(0)
    x = x_ref[...]
    hn = _norm_mod(x, nw_ref[...], mod_ref[1:2, :], mod_ref[0:1, :]).astype(BF16)
    gb = jnp.dot(hn, win_ref[:, 0:D], preferred_element_type=F32)
    gc = jnp.dot(hn, win_ref[:, D:2 * D], preferred_element_type=F32)
    v = jnp.dot(hn, win_ref[:, 2 * D:3 * D], preferred_element_type=F32)
    q = (gb * _conv3_rows(gc * v, cw_ref, i)).astype(BF16)
    mix = jnp.dot(q, wout_ref[...], preferred_element_type=F32)
    o_ref[...] = x + mod_ref[2:3, :] * mix


def _conv_mixer(x, mod_l, nw, w_in, cw, w_out):
    return pl.pallas_call(
        _conv_mixer_kernel,
        out_shape=jax.ShapeDtypeStruct((R, D), F32),
        grid=(R // TM,),
        in_specs=[
            pl.BlockSpec((TM, D), lambda i: (i, 0)),
            pl.BlockSpec((None, MOD_CHUNKS, D), lambda i: (_cond_of_tile(i), 0, 0)),
            pl.BlockSpec((1, D), lambda i: (0, 0)),
            pl.BlockSpec((D, 3 * D), lambda i: (0, 0)),
            pl.BlockSpec((3, D), lambda i: (0, 0)),
            pl.BlockSpec((D, D), lambda i: (0, 0)),
        ],
        out_specs=pl.BlockSpec((TM, D), lambda i: (i, 0)),
        compiler_params=_cparams(("arbitrary",)),
        name="conv_mixer",
    )(x, mod_l, nw, w_in, cw, w_out)


def _ssd_in_kernel(x_ref, mod_ref, nw_ref, win_ref, wdtT_ref, cw_ref, cb_ref, dtb_ref, dtbT_ref,
                   z_ref, xs_ref, bm_ref, cm_ref, dt_ref, dtT_ref):
    i = pl.program_id(0)
    hn = _norm_mod(x_ref[...], nw_ref[...], mod_ref[1:2, :], mod_ref[0:1, :]).astype(BF16)
    for k in range(D_INNER // D):
        z_ref[:, k * D:(k + 1) * D] = jnp.dot(hn, win_ref[:, k * D:(k + 1) * D], preferred_element_type=F32)
    for k in range(CONV_DIM // D):
        lo = D_INNER + k * D
        u = jnp.dot(hn, win_ref[:, lo:lo + D], preferred_element_type=F32)
        u = _silu(_conv3_rows(u, cw_ref.at[:, k * D:(k + 1) * D], i) + cb_ref[:, k * D:(k + 1) * D])
        if k < D_INNER // D:
            xs_ref[:, k * D:(k + 1) * D] = u
        else:
            bm_ref[...] = u[:, :GROUPS * STATE]
            cm_ref[...] = u[:, GROUPS * STATE:]
    lo = D_INNER + CONV_DIM
    dt = _softplus(jnp.dot(hn, win_ref[:, lo:lo + GROUPS * DT_LANES], preferred_element_type=F32) + dtb_ref[...])
    dtT = _softplus(
        lax.dot_general(wdtT_ref[...], hn, (((1,), (1,)), ((), ())), preferred_element_type=F32) + dtbT_ref[...])
    for g in range(GROUPS):
        dt_ref[g] = dt[:, g * DT_LANES:(g + 1) * DT_LANES].reshape(TM // CHUNK, CHUNK, DT_LANES)
        for k in range(TM // CHUNK):
            dtT_ref[g, k] = dtT[g * 2 * HPG:(g + 1) * 2 * HPG, k * CHUNK:(k + 1) * CHUNK]


def _ssd_in(x, mod_l, nw, w_in, w_dtT, cw, cb, dtb, dtbT):
    n_in = w_in.shape[1]
    nck = TM // CHUNK
    return pl.pallas_call(
        _ssd_in_kernel,
        out_shape=(
            jax.ShapeDtypeStruct((R, D_INNER), F32),
            jax.ShapeDtypeStruct((R, D_INNER), F32),
            jax.ShapeDtypeStruct((R, GROUPS * STATE), F32),
            jax.ShapeDtypeStruct((R, GROUPS * STATE), F32),
            jax.ShapeDtypeStruct((GROUPS, R // CHUNK, CHUNK, DT_LANES), F32),
            jax.ShapeDtypeStruct((GROUPS, R // CHUNK, 2 * HPG, CHUNK), F32),
        ),
        grid=(R // TM,),
        in_specs=[
            pl.BlockSpec((TM, D), lambda i: (i, 0)),
            pl.BlockSpec((None, MOD_CHUNKS, D), lambda i: (_cond_of_tile(i), 0, 0)),
            pl.BlockSpec((1, D), lambda i: (0, 0)),
            pl.BlockSpec((D, n_in), lambda i: (0, 0)),
            pl.BlockSpec((2 * HEADS, D), lambda i: (0, 0)),
            pl.BlockSpec((3, CONV_DIM), lambda i: (0, 0)),
            pl.BlockSpec((1, CONV_DIM), lambda i: (0, 0)),
            pl.BlockSpec((1, GROUPS * DT_LANES), lambda i: (0, 0)),
            pl.BlockSpec((2 * HEADS, 1), lambda i: (0, 0)),
        ],
        out_specs=(
            pl.BlockSpec((TM, D_INNER), lambda i: (i, 0)),
            pl.BlockSpec((TM, D_INNER), lambda i: (i, 0)),
            pl.BlockSpec((TM, GROUPS * STATE), lambda i: (i, 0)),
            pl.BlockSpec((TM, GROUPS * STATE), lambda i: (i, 0)),
            pl.BlockSpec((GROUPS, nck, CHUNK, DT_LANES), lambda i: (0, i, 0, 0)),
            pl.BlockSpec((GROUPS, nck, 2 * HPG, CHUNK), lambda i: (0, i, 0, 0)),
        ),
        compiler_params=_cparams(("arbitrary",)),
        name="ssd_in",
    )(x, mod_l, nw, w_in, w_dtT, cw, cb, dtb, dtbT)


def _ssd_scan_kernel(*refs, n_chunks, has_h0, n_kept, emit_state):
    xs_ref, bm_ref, cm_ref, dt_ref, dtT_ref, alog_ref, alogT_ref, dsk_ref = refs[:8]
    k = 8
    h0_ref = None
    if has_h0:
        h0_ref = refs[k]
        k += 1
    k += n_kept
    y_ref = refs[k]
    k += 1
    st_out_ref = None
    if emit_state:
        st_out_ref = refs[k]
        k += 1
    st_ref = refs[k]

    a_row = -jnp.exp(alog_ref[...]) * LOG2E
    a_col = -jnp.exp(alogT_ref[...]) * LOG2E
    row_is_fwd = lax.broadcasted_iota(jnp.int32, (2 * HPG, 1), 0) < HPG
    qi = lax.broadcasted_iota(jnp.int32, (CHUNK, CHUNK), 0)
    si = lax.broadcasted_iota(jnp.int32, (CHUNK, CHUNK), 1)
    lower = si <= qi
    upper = si >= qi
    tri_lo = jnp.where(lower, 1.0, 0.0).astype(BF16)
    tri_up = jnp.where(upper, 1.0, 0.0).astype(BF16)
    tri_lo_k = jnp.concatenate([tri_lo] * N_SPLIT, axis=1)
    tri_up_k = jnp.concatenate([tri_up] * N_SPLIT, axis=1)
    tri_lo_r = jnp.concatenate([tri_lo] * N_SPLIT, axis=0)
    tri_up_r = jnp.concatenate([tri_up] * N_SPLIT, axis=0)
    lane = lax.broadcasted_iota(jnp.int32, (1, DT_LANES), 1)
    piece_of_lane = lane // (2 * HPG)
    eh = lax.broadcasted_iota(jnp.int32, (DT_LANES, GW), 0) % (2 * HPG)
    ej = lax.broadcasted_iota(jnp.int32, (DT_LANES, GW), 1) // HEADDIM
    exp_f = jnp.where(eh == ej, 1.0, 0.0).astype(BF16)
    exp_b = jnp.where(eh == ej + HPG, 1.0, 0.0).astype(BF16)
    lane_is_fwd = lane % (2 * HPG) < HPG
    sh = lax.broadcasted_iota(jnp.int32, (DT_LANES, 2 * HPG * CHUNK), 0) % (2 * HPG)
    sj = lax.broadcasted_iota(jnp.int32, (DT_LANES, 2 * HPG * CHUNK), 1) // CHUNK
    col_sel = jnp.where(sh == sj, 1.0, 0.0).astype(BF16)
    tri_both_k = jnp.concatenate([tri_lo_k, tri_up_k], axis=0)
    tri_both_r = jnp.concatenate([tri_up_r, tri_lo_r], axis=1)
    lane_c = lax.broadcasted_iota(jnp.int32, (1, CHUNK), 1)
    first_half = lane_c < HEADDIM

    def lane_pieces(v):
        pieces = _split_pieces(v)
        out = pieces[N_SPLIT - 1]
        for r in range(N_SPLIT - 2, -1, -1):
            out = jnp.where(piece_of_lane == r, pieces[r], out)
        return out

    def expand_many(vs, e):
        out = jnp.dot(jnp.concatenate([lane_pieces(v) for v in vs], axis=0), e, preferred_element_type=F32)
        res, r0 = [], 0
        for v in vs:
            res.append(out[r0:r0 + v.shape[0], :])
            r0 += v.shape[0]
        return res

    def rows8(v):
        return jnp.broadcast_to(v, (8, DT_LANES))

    for d in range(2):
        if has_h0:
            st_ref[d] = h0_ref[d].T
        else:
            st_ref[d] = jnp.zeros((STATE, GW), F32)

    y_ref[...] = jnp.zeros(y_ref.shape, F32)

    def body(k_, carry):
        cf = k_
        x = xs_ref[cf]
        b = bm_ref[cf]
        cm = cm_ref[cf]
        dt = dt_ref[cf]
        dtT = dtT_ref[cf]
        cr = n_chunks - 1 - k_
        xr = xs_ref[cr]
        br = bm_ref[cr]
        dtr = dt_ref[cr]

        dta_rows = jnp.concatenate(_split_pieces(dt * a_row), axis=0)
        dtaT_lanes = jnp.concatenate(_split_pieces(dtT * a_col), axis=1)
        acs2 = jnp.dot(tri_both_k, dta_rows, preferred_element_type=F32)
        acs_lo, acs_up = acs2[:CHUNK, :], acs2[CHUNK:, :]
        acs_t2 = jnp.dot(dtaT_lanes, tri_both_r, preferred_element_type=F32)
        acs_up_r = jnp.dot(tri_up_k, jnp.concatenate(_split_pieces(dtr * a_row), axis=0),
                           preferred_element_type=F32)
        r_t = jnp.log(dtT) * LOG2E - jnp.where(row_is_fwd, acs_t2[:, :CHUNK], acs_t2[:, CHUNK:])
        col_b = jnp.dot(lane_pieces(jnp.where(lane_is_fwd, acs_lo, acs_up)), col_sel, preferred_element_type=F32)

        a_last = acs_lo[CHUNK - 1:CHUNK, :]
        a_tot = acs_up_r[0:1, :]
        cb_diag = jnp.sum(cm * b, axis=1, keepdims=True)
        e_out_f, e_in_f, e_keep_f = expand_many(
            [jnp.exp2(acs_lo), jnp.exp2(a_last - acs_lo) * dt, rows8(jnp.exp2(a_last))], exp_f)
        e_self, e_out_b, e_in_b, e_keep_b = expand_many(
            [dsk_ref[...] + cb_diag * dt, jnp.exp2(acs_up_r), jnp.exp2(a_tot - acs_up_r) * dtr,
             rows8(jnp.exp2(a_tot))], exp_b)

        cmb = cm.astype(BF16)
        cb = lax.dot_general(cmb, b.astype(BF16), (((1,), (1,)), ((), ())), preferred_element_type=F32)
        parts = []
        for hp in range(HPG // 2):
            ws = []
            for h in (2 * hp, 2 * hp + 1):
                arg = jnp.where(lower, col_b[:, h * CHUNK:(h + 1) * CHUNK] + r_t[h:h + 1, :],
                                col_b[:, (HPG + h) * CHUNK:(HPG + h + 1) * CHUNK] + r_t[HPG + h:HPG + h + 1, :])
                ws.append((cb * jnp.exp2(arg)).astype(BF16))
            xp = x[:, hp * CHUNK:(hp + 1) * CHUNK]
            x2 = jnp.concatenate([jnp.where(first_half, xp, 0.0), jnp.where(first_half, 0.0, xp)], axis=0)
            parts.append(jnp.dot(jnp.concatenate(ws, axis=1), x2.astype(BF16), preferred_element_type=F32))
        st_f = st_ref[0]
        y = (jnp.concatenate(parts, axis=1) + e_self * x
             + jnp.dot(cmb, st_f.astype(BF16), preferred_element_type=F32) * e_out_f)
        y_ref[cf] = y_ref[cf] + y
        st_ref[0] = st_f * e_keep_f[0:1, :] + jnp.dot(
            b.T.astype(BF16), (x * e_in_f).astype(BF16), preferred_element_type=F32)

        st_b = st_ref[1]
        y_ref[cr] = y_ref[cr] + jnp.dot(cm_ref[cr].astype(BF16), st_b.astype(BF16),
                                        preferred_element_type=F32) * e_out_b
        st_ref[1] = st_b * e_keep_b[0:1, :] + jnp.dot(
            br.T.astype(BF16), (xr * e_in_b).astype(BF16), preferred_element_type=F32)
        return carry

    lax.fori_loop(0, n_chunks, body, 0)
    if emit_state:
        for d in range(2):
            st_out_ref[d] = st_ref[d].T


def _ssd_scan(xs3, bm3, cm3, dt4, dtT4, alog4, alogT4, dskx, h0, y_prev, st_prev, *, n_seq, seq_len, row_block0,
              state_layer):
    nck = seq_len // CHUNK
    has_h0 = h0 is not None
    emit_state = state_layer is not None
    rb = lambda s: s + row_block0
    in_specs = [
        pl.BlockSpec((nck, CHUNK, GW), lambda s, g: (rb(s), 0, g)),
        pl.BlockSpec((nck, CHUNK, STATE), lambda s, g: (rb(s), 0, g)),
        pl.BlockSpec((nck, CHUNK, STATE), lambda s, g: (rb(s), 0, g)),
        pl.BlockSpec((None, nck, CHUNK, DT_LANES), lambda s, g: (g, rb(s), 0, 0)),
        pl.BlockSpec((None, nck, 2 * HPG, CHUNK), lambda s, g: (g, rb(s), 0, 0)),
        pl.BlockSpec((None, 1, DT_LANES), lambda s, g: (g, 0, 0)),
        pl.BlockSpec((None, 2 * HPG, 1), lambda s, g: (g, 0, 0)),
        pl.BlockSpec((None, 1, DT_LANES), lambda s, g: (g, 0, 0)),
    ]
    args = [xs3, bm3, cm3, dt4, dtT4, alog4, alogT4, dskx]
    if has_h0:
        in_specs.append(pl.BlockSpec((None, 2, GW, STATE), lambda s, g: (s, 0, g, 0)))
        args.append(h0)
    aliases = {}
    for out_idx, kept in enumerate((y_prev, st_prev)):
        if kept is not None:
            in_specs.append(pl.BlockSpec(memory_space=pl.ANY))
            aliases[len(args)] = out_idx
            args.append(kept)
    y_shape = jax.ShapeDtypeStruct((R // CHUNK, CHUNK, D_INNER), F32)
    y_spec = pl.BlockSpec((nck, CHUNK, GW), lambda s, g: (rb(s), 0, g))
    if emit_state:
        out_shape = (y_shape, jax.ShapeDtypeStruct((n_seq, DEPTH // 2, 2, HEADS * HEADDIM, STATE), F32))
        out_specs = (y_spec, pl.BlockSpec((None, None, 2, GW, STATE), lambda s, g: (s, state_layer, 0, g, 0)))
    else:
        out_shape, out_specs = y_shape, y_spec
    return pl.pallas_call(
        functools.partial(_ssd_scan_kernel, n_chunks=nck, has_h0=has_h0, n_kept=len(aliases),
                          emit_state=emit_state),
        out_shape=out_shape,
        grid=(n_seq, GROUPS),
        in_specs=in_specs,
        out_specs=out_specs,
        scratch_shapes=[pltpu.VMEM((2, STATE, GW), F32)],
        input_output_aliases=aliases,
        compiler_params=_cparams(("arbitrary", "arbitrary")),
        name="ssd_scan",
    )(*args)


def _ssd_out_kernel(x_ref, y_ref, z_ref, mod_ref, nw_ref, wout_ref, o_ref):
    v = y_ref[...] * _silu(z_ref[...])
    v = v * lax.rsqrt(jnp.mean(v * v, axis=-1, keepdims=True) + EPS) * nw_ref[...]
    mix = jnp.dot(v.astype(BF16), wout_ref[...], preferred_element_type=F32)
    o_ref[...] = x_ref[...] + mod_ref[2:3, :] * mix


def _ssd_out(x, y, z, mod_l, nw, w_out):
    return pl.pallas_call(
        _ssd_out_kernel,
        out_shape=jax.ShapeDtypeStruct((R, D), F32),
        grid=(R // TM,),
        in_specs=[
            pl.BlockSpec((TM, D), lambda i: (i, 0)),
            pl.BlockSpec((TM, D_INNER), lambda i: (i, 0)),
            pl.BlockSpec((TM, D_INNER), lambda i: (i, 0)),
            pl.BlockSpec((None, MOD_CHUNKS, D), lambda i: (_cond_of_tile(i), 0, 0)),
            pl.BlockSpec((1, D_INNER), lambda i: (0, 0)),
            pl.BlockSpec((D_INNER, D), lambda i: (0, 0)),
        ],
        out_specs=pl.BlockSpec((TM, D), lambda i: (i, 0)),
        compiler_params=_cparams(("arbitrary",)),
        name="ssd_out",
    )(x, y, z, mod_l, nw, w_out)


def _router_kernel(x_ref, mod_ref, nw_ref, wrT_ref, hn_ref, affT_ref):
    hn = _norm_mod(x_ref[...], nw_ref[...], mod_ref[4:5, :], mod_ref[3:4, :])
    hn_hi = hn.astype(BF16)
    hn_ref[...] = hn_hi
    hn_lo = (hn - hn_hi.astype(F32)).astype(BF16)
    w = wrT_ref[...]
    w_hi = w.astype(BF16)
    w_lo = (w - w_hi.astype(F32)).astype(BF16)
    logits = lax.dot_general(jnp.concatenate([w_hi, w_lo, w_hi], axis=1), jnp.concatenate([hn_hi, hn_hi, hn_lo], axis=1),
                             (((1,), (1,)), ((), ())), preferred_element_type=F32)
    e = jnp.exp(logits - jnp.max(logits, axis=0, keepdims=True))
    affT_ref[...] = e / jnp.sum(e, axis=0, keepdims=True)


def _router(x, mod_l, nw, wrT):
    return pl.pallas_call(
        _router_kernel,
        out_shape=(jax.ShapeDtypeStruct((R, D), BF16), jax.ShapeDtypeStruct((N_EXPERTS, R), F32)),
        grid=(R // TM,),
        in_specs=[
            pl.BlockSpec((TM, D), lambda i: (i, 0)),
            pl.BlockSpec((None, MOD_CHUNKS, D), lambda i: (_cond_of_tile(i), 0, 0)),
            pl.BlockSpec((1, D), lambda i: (0, 0)),
            pl.BlockSpec((N_EXPERTS, D), lambda i: (0, 0)),
        ],
        out_specs=(pl.BlockSpec((TM, D), lambda i: (i, 0)), pl.BlockSpec((N_EXPERTS, TM), lambda i: (0, i))),
        compiler_params=_cparams(("arbitrary",)),
        name="router",
    )(x, mod_l, nw, wrT)


def _excl_cumsum_lanes(m):
    blk = 256
    t = m.shape[1]
    a = lax.broadcasted_iota(jnp.int32, (blk, blk), 0)
    b = lax.broadcasted_iota(jnp.int32, (blk, blk), 1)
    strict = jnp.where(a < b, 1.0, 0.0).astype(BF16)
    carry = jnp.zeros((m.shape[0], 1), F32)
    outs = []
    for k in range(t // blk):
        mk = m[:, k * blk:(k + 1) * blk]
        outs.append(jnp.dot(mk.astype(BF16), strict, preferred_element_type=F32) + carry)
        carry = carry + jnp.sum(mk, axis=1, keepdims=True)
    return outs[0] if len(outs) == 1 else jnp.concatenate(outs, axis=1)


def _select_request(aff, cap, base):
    bits = pltpu.bitcast(aff, jnp.int32)
    thr = jnp.zeros((N_EXPERTS, 1), jnp.int32)
    for k in range(30, -1, -1):
        trial = thr | (1 << k)
        cnt = jnp.sum(jnp.where(bits >= trial, 1.0, 0.0), axis=1, keepdims=True)
        thr = jnp.where(cnt >= cap, trial, thr)
    gt = bits > thr
    eq = jnp.where(bits == thr, 1.0, 0.0)
    need = cap - jnp.sum(jnp.where(gt, 1.0, 0.0), axis=1, keepdims=True)
    sel = gt | ((eq > 0.0) & (_excl_cumsum_lanes(eq) < need))
    pos = _excl_cumsum_lanes(jnp.where(sel, 1.0, 0.0))
    return jnp.where(sel, pos + base, -1.0), jnp.where(sel, aff, 0.0)


def _select_kernel(affT_ref, slotT_ref, slot_tok_ref, gate_tok_ref):
    s = pl.program_id(0)

    def emit(slot, gate):
        slotT_ref[...] = slot.astype(jnp.int32)
        pad = jnp.zeros((128 - N_EXPERTS, SR_TOKENS), F32)
        slot_tok_ref[...] = jnp.concatenate([slot, pad - 1.0], axis=0).T.astype(jnp.int32)
        gate_tok_ref[...] = jnp.concatenate([gate, pad], axis=0).T

    @pl.when(s < N_SR_CTX)
    def _():
        outs = [_select_request(affT_ref[:, r * L_CTX:(r + 1) * L_CTX], CAP_CTX, float(r * CAP_CTX))
                for r in range(CTX_PER_SR)]
        emit(jnp.concatenate([o[0] for o in outs], axis=1), jnp.concatenate([o[1] for o in outs], axis=1))

    @pl.when(s >= N_SR_CTX)
    def _():
        emit(*_select_request(affT_ref[...], CAP_LAT, 0.0))


def _select(affT):
    return pl.pallas_call(
        _select_kernel,
        out_shape=(
            jax.ShapeDtypeStruct((N_EXPERTS, R), jnp.int32),
            jax.ShapeDtypeStruct((R, 128), jnp.int32),
            jax.ShapeDtypeStruct((R, 128), F32),
        ),
        grid=(N_SR,),
        in_specs=[pl.BlockSpec((N_EXPERTS, SR_TOKENS), lambda s: (0, s))],
        out_specs=(
            pl.BlockSpec((N_EXPERTS, SR_TOKENS), lambda s: (0, s)),
            pl.BlockSpec((SR_TOKENS, 128), lambda s: (s, 0)),
            pl.BlockSpec((SR_TOKENS, 128), lambda s: (s, 0)),
        ),
        compiler_params=_cparams(("arbitrary",)),
        name="select",
    )(affT)


def _gather_kernel(hn_ref, slotT_ref, xe_ref):
    e = pl.program_id(1)
    slot_row = slotT_ref[pl.ds(e, 1), :]
    s_iota = lax.broadcasted_iota(jnp.int32, (SR_SLOTS, SR_TOKENS), 0)
    onehot = jnp.where(s_iota == slot_row, 1.0, 0.0).astype(BF16)
    xe_ref[...] = jnp.dot(onehot, hn_ref[...], preferred_element_type=F32).astype(BF16)


def _gather(hn, slotT):
    return pl.pallas_call(
        _gather_kernel,
        out_shape=jax.ShapeDtypeStruct((N_EXPERTS, N_SR * SR_SLOTS, D), BF16),
        grid=(N_SR, N_EXPERTS),
        in_specs=[
            pl.BlockSpec((SR_TOKENS, D), lambda s, e: (s, 0)),
            pl.BlockSpec((N_EXPERTS, SR_TOKENS), lambda s, e: (0, s)),
        ],
        out_specs=pl.BlockSpec((None, SR_SLOTS, D), lambda s, e: (e, s, 0)),
        compiler_params=_cparams(("arbitrary", "arbitrary")),
        name="gather",
    )(hn, slotT)


def _ffn_kernel(xe_ref, wg_ref, wu_ref, wd_ref, ye_ref, acc_ref):
    f = pl.program_id(1)
    xe = xe_ref[...]
    hg = jnp.dot(xe, wg_ref[...].astype(BF16), preferred_element_type=F32)
    hu = jnp.dot(xe, wu_ref[...].astype(BF16), preferred_element_type=F32)
    hid = (_silu(hg) * hu).astype(BF16)
    part = jnp.dot(hid, wd_ref[...].astype(BF16), preferred_element_type=F32)

    @pl.when(f == 0)
    def _():
        acc_ref[...] = part

    @pl.when(f > 0)
    def _():
        acc_ref[...] = acc_ref[...] + part

    @pl.when(f == pl.num_programs(1) - 1)
    def _():
        ye_ref[...] = acc_ref[...].astype(BF16)


def _ffn(xe, wg, wu, wd, layer):
    tf = 512
    m = xe.shape[1]
    return pl.pallas_call(
        _ffn_kernel,
        out_shape=jax.ShapeDtypeStruct((N_EXPERTS, m, D), BF16),
        grid=(N_EXPERTS, D // tf),
        in_specs=[
            pl.BlockSpec((None, m, D), lambda e, f: (e, 0, 0)),
            pl.BlockSpec((None, None, D, tf), lambda e, f: (layer, e, 0, f)),
            pl.BlockSpec((None, None, D, tf), lambda e, f: (layer, e, 0, f)),
            pl.BlockSpec((None, None, tf, D), lambda e, f: (layer, e, f, 0)),
        ],
        out_specs=pl.BlockSpec((None, m, D), lambda e, f: (e, 0, 0)),
        scratch_shapes=[pltpu.VMEM((m, D), F32)],
        compiler_params=_cparams(("arbitrary", "arbitrary")),
        name="expert_ffn",
    )(xe, wg, wu, wd)


def _combine_kernel(x_ref, ye_ref, slot_ref, gate_ref, mod_ref, fnw_ref, o_ref, *, final_norm):
    slot = slot_ref[...]
    gate = gate_ref[...]
    s_iota = lax.broadcasted_iota(jnp.int32, (slot.shape[0], SR_SLOTS), 1)
    acc = jnp.zeros(o_ref.shape, F32)
    for e in range(N_EXPERTS):
        onehot = jnp.where(slot[:, e:e + 1] == s_iota, 1.0, 0.0).astype(BF16)
        acc = acc + gate[:, e:e + 1] * jnp.dot(onehot, ye_ref[e], preferred_element_type=F32)
    out = x_ref[...] + mod_ref[5:6, :] * acc
    if final_norm:
        out = out * lax.rsqrt(jnp.mean(out * out, axis=-1, keepdims=True) + EPS) * fnw_ref[...]
    o_ref[...] = out


def _combine(x, ye, slot_tok, gate_tok, mod_l, fnw, *, final_norm):
    tq = 512
    nq = SR_TOKENS // tq
    return pl.pallas_call(
        functools.partial(_combine_kernel, final_norm=final_norm),
        out_shape=jax.ShapeDtypeStruct((R, D), F32),
        grid=(N_SR, nq),
        in_specs=[
            pl.BlockSpec((tq, D), lambda s, q: (s * nq + q, 0)),
            pl.BlockSpec((N_EXPERTS, SR_SLOTS, D), lambda s, q: (0, s, 0)),
            pl.BlockSpec((tq, 128), lambda s, q: (s * nq + q, 0)),
            pl.BlockSpec((tq, 128), lambda s, q: (s * nq + q, 0)),
            pl.BlockSpec((None, MOD_CHUNKS, D),
                         lambda s, q: (_cond_of_tile((s * nq + q) * (tq // TM)), 0, 0)),
            pl.BlockSpec((1, D), lambda s, q: (0, 0)),
        ],
        out_specs=pl.BlockSpec((tq, D), lambda s, q: (s * nq + q, 0)),
        compiler_params=_cparams(("arbitrary", "arbitrary")),
        name="combine",
    )(x, ye, slot_tok, gate_tok, mod_l, fnw)


def _moe(x, mod_l, nw, wrT, wg, wu, wd, fnw, layer, *, final_norm):
    hn, affT = _router(x, mod_l, nw, wrT)
    slotT, slot_tok, gate_tok = _select(affT)
    xe = _gather(hn, slotT)
    ye = _ffn(xe, wg, wu, wd, layer)
    return _combine(x, ye, slot_tok, gate_tok, mod_l, fnw, final_norm=final_norm)


def _group_major(p):
    return p.reshape(2, GROUPS, HPG).transpose(1, 0, 2).reshape(GROUPS, 2 * HPG)


def kernel(x_prompt, x_sample, state_ssm, c, c_ctx, norm1_w, norm2_w, w_mod, b_mod, conv_in_w, conv_w, conv_out_w, ssd_in_w, ssd_conv_w, ssd_conv_b, ssd_dt_bias, ssd_a_log, ssd_d, ssd_norm_w, ssd_out_w, router_w, exp_w_gate, exp_w_up, exp_w_down, final_norm_w):
    x = jnp.concatenate([x_prompt.reshape(R_CTX, D), x_sample.reshape(R_LAT, D)], axis=0)
    cond = jnp.concatenate([c_ctx[None, :], c, jnp.zeros((8 - N_COND, D), F32)], axis=0)
    mods = _modulation(cond.T, w_mod, b_mod)
    mods = mods[:, :N_COND].reshape(DEPTH, N_COND, MOD_CHUNKS, D)
    fnw = final_norm_w.reshape(1, D)

    states = None
    for layer in range(DEPTH):
        mod_l = mods[layer]
        j = layer // 2
        nw1 = norm1_w[layer].reshape(1, D)
        if layer % 2 == 0:
            x = _conv_mixer(x, mod_l, nw1, conv_in_w[j].astype(BF16), conv_w[j], conv_out_w[j].astype(BF16))
        else:
            w_in = ssd_in_w[j]
            w_dt = w_in[:, D_INNER + CONV_DIM:].reshape(D, 2, GROUPS, HPG).transpose(0, 2, 1, 3)
            w_dt_rep = jnp.broadcast_to(w_dt.reshape(D, GROUPS, 1, 2 * HPG), (D, GROUPS, N_SPLIT, 2 * HPG))
            w_in_b = jnp.concatenate([w_in[:, :D_INNER + CONV_DIM], w_dt_rep.reshape(D, GROUPS * DT_LANES)],
                                     axis=1).astype(BF16)
            dtb = _group_major(ssd_dt_bias[j])
            dtb_rep = jnp.tile(dtb, (1, N_SPLIT)).reshape(1, GROUPS * DT_LANES)
            z, xs, bm, cm, dt4, dtT4 = _ssd_in(
                x, mod_l, nw1, w_in_b, w_dt.reshape(D, 2 * HEADS).T.astype(BF16), ssd_conv_w[j],
                ssd_conv_b[j].reshape(1, CONV_DIM), dtb_rep, dtb.reshape(2 * HEADS, 1))
            alog4 = _group_major(ssd_a_log[j])
            dsum = (ssd_d[j][0] + ssd_d[j][1]).reshape(GROUPS, HPG)
            dsk4 = jnp.tile(jnp.concatenate([jnp.zeros_like(dsum), dsum], axis=1), (1, N_SPLIT))
            scan_args = (xs.reshape(R // CHUNK, CHUNK, D_INNER), bm.reshape(R // CHUNK, CHUNK, GROUPS * STATE),
                         cm.reshape(R // CHUNK, CHUNK, GROUPS * STATE), dt4, dtT4,
                         jnp.tile(alog4, (1, N_SPLIT)).reshape(GROUPS, 1, DT_LANES),
                         alog4.reshape(GROUPS, 2 * HPG, 1), dsk4.reshape(GROUPS, 1, DT_LANES))
            y, states = _ssd_scan(*scan_args, None, None, states, n_seq=N_CTX, seq_len=L_CTX, row_block0=0,
                                  state_layer=j)
            h0 = state_ssm[:, j].reshape(N_LAT, 2, HEADS * HEADDIM, STATE)
            y = _ssd_scan(*scan_args, h0, y, None, n_seq=N_LAT, seq_len=L_LAT, row_block0=R_CTX // L_LAT,
                          state_layer=None)
            x = _ssd_out(x, y.reshape(R, D_INNER), z, mod_l, ssd_norm_w[j].reshape(1, D_INNER),
                         ssd_out_w[j].astype(BF16))
        x = _moe(x, mod_l, norm2_w[layer].reshape(1, D), router_w[layer].T, exp_w_gate, exp_w_up, exp_w_down, fnw,
                 layer, final_norm=(layer == DEPTH - 1))

    y_prompt = x[:R_CTX].reshape(N_CTX, L_CTX, D)
    y_sample = x[R_CTX:].reshape(N_LAT, L_LAT, D)
    return y_prompt, y_sample, states.reshape(N_CTX, DEPTH // 2, 2, HEADS, HEADDIM, STATE)
```

```python
import functools

import jax
import jax.numpy as jnp
from jax import lax
from jax.experimental import pallas as pl
from jax.experimental.pallas import tpu as pltpu

F32 = jnp.float32
BF16 = jnp.bfloat16
HIGHEST = lax.Precision.HIGHEST
LOG2E = 1.4426950408889634

D = 1024
DEPTH = 4
N_CTX, L_CTX = 16, 256
N_LAT, L_LAT = 2, 2048
GRID_W = 64
R_CTX = N_CTX * L_CTX
R_LAT = N_LAT * L_LAT
R = R_CTX + R_LAT
N_COND = 1 + N_LAT
MOD_CHUNKS = 6
D_INNER = 2048
HEADDIM = 64
HEADS = 32
GROUPS = 4
HPG = HEADS // GROUPS
GW = HPG * HEADDIM
STATE = 128
CHUNK = 128
CONV_DIM = D_INNER + 2 * GROUPS * STATE
N_SPLIT = 3
DT_LANES = N_SPLIT * 2 * HPG
N_EXPERTS = 16
EPS = 1e-6

TM = 512
CTX_TILES = R_CTX // TM
LAT_TILES_PER_REQ = L_LAT // TM
SR_TOKENS = 2048
N_SR = R // SR_TOKENS
N_SR_CTX = R_CTX // SR_TOKENS
SR_SLOTS = 2 * SR_TOKENS // N_EXPERTS
CTX_PER_SR = SR_TOKENS // L_CTX
CAP_CTX = 2 * L_CTX // N_EXPERTS
CAP_LAT = 2 * L_LAT // N_EXPERTS
BLK = 256
N_BLK = SR_TOKENS // BLK
LO_COLS = 16
BF16_ROWS = 16
GW_ROWS = 80
CW_ROWS = 64
E_HALF = N_EXPERTS // 2
VMEM_LIMIT = 56 * 1024 * 1024


def _cparams(sem):
    return pltpu.CompilerParams(dimension_semantics=sem, vmem_limit_bytes=VMEM_LIMIT)


def _cond_of_tile(i):
    return jnp.where(i < CTX_TILES, 0, 1 + (i - CTX_TILES) // LAT_TILES_PER_REQ)


def _silu(v):
    return v * (1.0 / (1.0 + jnp.exp(-v)))


def _softplus(v):
    return jnp.maximum(v, 0.0) + jnp.log1p(jnp.exp(-jnp.abs(v)))


def _norm_mod(x, nw, scale, shift):
    y = x * lax.rsqrt(jnp.mean(x * x, axis=-1, keepdims=True) + EPS)
    return y * nw * (1.0 + scale) + shift


def _conv3_rows(p, w_ref, tile_idx):
    n = p.shape[0]
    period = jnp.where(tile_idx < CTX_TILES, L_CTX, GRID_W)
    r = lax.broadcasted_iota(jnp.int32, (n, 1), 0) & (period - 1)
    prev = jnp.where(r == 0, 0.0, pltpu.roll(p, 1, axis=0))
    nxt = jnp.where(r == period - 1, 0.0, pltpu.roll(p, n - 1, axis=0))
    return prev * w_ref[0:1, :] + p * w_ref[1:2, :] + nxt * w_ref[2:3, :]


def _split_pieces(v):
    pieces = []
    r = v
    for _ in range(N_SPLIT):
        p = r.astype(BF16)
        pieces.append(p)
        r = r - p.astype(F32)
    return pieces


def _modulation_kernel(condT_ref, w_ref, b_ref, o_ref):
    s = _silu(condT_ref[...])
    w = w_ref[...]
    rows = [jnp.sum(w * s[:, r:r + 1], axis=0, keepdims=True) + b_ref[...] for r in range(N_COND)]
    rows.append(jnp.zeros((8 - N_COND, w.shape[1]), F32))
    o_ref[...] = jnp.concatenate(rows, axis=0)


def _modulation(condT, w_mod, b_mod):
    tn = 1536
    n = MOD_CHUNKS * D
    return pl.pallas_call(
        _modulation_kernel,
        out_shape=jax.ShapeDtypeStruct((DEPTH, 8, n), F32),
        grid=(DEPTH, n // tn),
        in_specs=[
            pl.BlockSpec((D, 8), lambda l, j: (0, 0)),
            pl.BlockSpec((None, D, tn), lambda l, j: (l, 0, j)),
            pl.BlockSpec((None, 1, tn), lambda l, j: (l, 0, j)),
        ],
        out_specs=pl.BlockSpec((None, 8, tn), lambda l, j: (l, 0, j)),
        compiler_params=_cparams(("arbitrary", "arbitrary")),
        name="modulation",
    )(condT, w_mod, b_mod.reshape(DEPTH, 1, n))


def _conv_mixer_kernel(x_ref, mod_ref, nw_ref, win_ref, cw_ref, wout_ref, o_ref):
    i = pl.program_id(0)
    x = x_ref[...]
    hn = _norm_mod(x, nw_ref[...], mod_ref[1:2, :], mod_ref[0:1, :]).astype(BF16)
    gb = jnp.dot(hn, win_ref[:, 0:D], preferred_element_type=F32)
    gc = jnp.dot(hn, win_ref[:, D:2 * D], preferred_element_type=F32)
    v = jnp.dot(hn, win_ref[:, 2 * D:3 * D], preferred_element_type=F32)
    q = (gb * _conv3_rows(gc * v, cw_ref, i)).astype(BF16)
    mix = jnp.dot(q, wout_ref[...], preferred_element_type=F32)
    o_ref[...] = x + mod_ref[2:3, :] * mix


def _conv_mixer(x, mod_l, nw, w_in, cw, w_out):
    return pl.pallas_call(
        _conv_mixer_kernel,
        out_shape=jax.ShapeDtypeStruct((R, D), F32),
        grid=(R // TM,),
        in_specs=[
            pl.BlockSpec((TM, D), lambda i: (i, 0)),
            pl.BlockSpec((None, MOD_CHUNKS, D), lambda i: (_cond_of_tile(i), 0, 0)),
            pl.BlockSpec((1, D), lambda i: (0, 0)),
            pl.BlockSpec((D, 3 * D), lambda i: (0, 0)),
            pl.BlockSpec((3, D), lambda i: (0, 0)),
            pl.BlockSpec((D, D), lambda i: (0, 0)),
        ],
        out_specs=pl.BlockSpec((TM, D), lambda i: (i, 0)),
        compiler_params=_cparams(("arbitrary",)),
        name="conv_mixer",
    )(x, mod_l, nw, w_in, cw, w_out)


def _ssd_in_kernel(x_ref, mod_ref, nw_ref, win_ref, wdtT_ref, cw_ref, cb_ref, dtb_ref, dtbT_ref,
                   z_ref, xs_ref, bm_ref, cm_ref, dt_ref, dtT_ref):
    i = pl.program_id(0)
    hn = _norm_mod(x_ref[...], nw_ref[...], mod_ref[1:2, :], mod_ref[0:1, :]).astype(BF16)
    for k in range(D_INNER // D):
        z_ref[:, k * D:(k + 1) * D] = jnp.dot(hn, win_ref[:, k * D:(k + 1) * D], preferred_element_type=F32)
    for k in range(CONV_DIM // D):
        lo = D_INNER + k * D
        u = jnp.dot(hn, win_ref[:, lo:lo + D], preferred_element_type=F32)
        u = _silu(_conv3_rows(u, cw_ref.at[:, k * D:(k + 1) * D], i) + cb_ref[:, k * D:(k + 1) * D])
        if k < D_INNER // D:
            xs_ref[:, k * D:(k + 1) * D] = u
        else:
            bm_ref[...] = u[:, :GROUPS * STATE]
            cm_ref[...] = u[:, GROUPS * STATE:]
    lo = D_INNER + CONV_DIM
    dt = _softplus(jnp.dot(hn, win_ref[:, lo:lo + GROUPS * DT_LANES], preferred_element_type=F32) + dtb_ref[...])
    dtT = _softplus(
        lax.dot_general(wdtT_ref[...], hn, (((1,), (1,)), ((), ())), preferred_element_type=F32) + dtbT_ref[...])
    for g in range(GROUPS):
        dt_ref[g] = dt[:, g * DT_LANES:(g + 1) * DT_LANES].reshape(TM // CHUNK, CHUNK, DT_LANES)
        for k in range(TM // CHUNK):
            dtT_ref[g, k] = dtT[g * 2 * HPG:(g + 1) * 2 * HPG, k * CHUNK:(k + 1) * CHUNK]


def _ssd_in(x, mod_l, nw, w_in, w_dtT, cw, cb, dtb, dtbT):
    n_in = w_in.shape[1]
    nck = TM // CHUNK
    return pl.pallas_call(
        _ssd_in_kernel,
        out_shape=(
            jax.ShapeDtypeStruct((R, D_INNER), F32),
            jax.ShapeDtypeStruct((R, D_INNER), F32),
            jax.ShapeDtypeStruct((R, GROUPS * STATE), F32),
            jax.ShapeDtypeStruct((R, GROUPS * STATE), F32),
            jax.ShapeDtypeStruct((GROUPS, R // CHUNK, CHUNK, DT_LANES), F32),
            jax.ShapeDtypeStruct((GROUPS, R // CHUNK, 2 * HPG, CHUNK), F32),
        ),
        grid=(R // TM,),
        in_specs=[
            pl.BlockSpec((TM, D), lambda i: (i, 0)),
            pl.BlockSpec((None, MOD_CHUNKS, D), lambda i: (_cond_of_tile(i), 0, 0)),
            pl.BlockSpec((1, D), lambda i: (0, 0)),
            pl.BlockSpec((D, n_in), lambda i: (0, 0)),
            pl.BlockSpec((2 * HEADS, D), lambda i: (0, 0)),
            pl.BlockSpec((3, CONV_DIM), lambda i: (0, 0)),
            pl.BlockSpec((1, CONV_DIM), lambda i: (0, 0)),
            pl.BlockSpec((1, GROUPS * DT_LANES), lambda i: (0, 0)),
            pl.BlockSpec((2 * HEADS, 1), lambda i: (0, 0)),
        ],
        out_specs=(
            pl.BlockSpec((TM, D_INNER), lambda i: (i, 0)),
            pl.BlockSpec((TM, D_INNER), lambda i: (i, 0)),
            pl.BlockSpec((TM, GROUPS * STATE), lambda i: (i, 0)),
            pl.BlockSpec((TM, GROUPS * STATE), lambda i: (i, 0)),
            pl.BlockSpec((GROUPS, nck, CHUNK, DT_LANES), lambda i: (0, i, 0, 0)),
            pl.BlockSpec((GROUPS, nck, 2 * HPG, CHUNK), lambda i: (0, i, 0, 0)),
        ),
        compiler_params=_cparams(("arbitrary",)),
        name="ssd_in",
    )(x, mod_l, nw, w_in, w_dtT, cw, cb, dtb, dtbT)


def _ssd_scan_kernel(*refs, n_chunks, has_h0, n_kept, emit_state):
    xs_ref, bm_ref, cm_ref, dt_ref, dtT_ref, alog_ref, alogT_ref, dsk_ref = refs[:8]
    k = 8
    h0_ref = None
    if has_h0:
        h0_ref = refs[k]
        k += 1
    k += n_kept
    y_ref = refs[k]
    k += 1
    st_out_ref = None
    if emit_state:
        st_out_ref = refs[k]
        k += 1
    st_ref = refs[k]

    a_row = -jnp.exp(alog_ref[...]) * LOG2E
    a_col = -jnp.exp(alogT_ref[...]) * LOG2E
    row_is_fwd = lax.broadcasted_iota(jnp.int32, (2 * HPG, 1), 0) < HPG
    qi = lax.broadcasted_iota(jnp.int32, (CHUNK, CHUNK), 0)
    si = lax.broadcasted_iota(jnp.int32, (CHUNK, CHUNK), 1)
    lower = si <= qi
    upper = si >= qi
    tri_lo = jnp.where(lower, 1.0, 0.0).astype(BF16)
    tri_up = jnp.where(upper, 1.0, 0.0).astype(BF16)
    tri_lo_k = jnp.concatenate([tri_lo] * N_SPLIT, axis=1)
    tri_up_k = jnp.concatenate([tri_up] * N_SPLIT, axis=1)
    tri_lo_r = jnp.concatenate([tri_lo] * N_SPLIT, axis=0)
    tri_up_r = jnp.concatenate([tri_up] * N_SPLIT, axis=0)
    lane = lax.broadcasted_iota(jnp.int32, (1, DT_LANES), 1)
    piece_of_lane = lane // (2 * HPG)
    eh = lax.broadcasted_iota(jnp.int32, (DT_LANES, GW), 0) % (2 * HPG)
    ej = lax.broadcasted_iota(jnp.int32, (DT_LANES, GW), 1) // HEADDIM
    exp_f = jnp.where(eh == ej, 1.0, 0.0).astype(BF16)
    exp_b = jnp.where(eh == ej + HPG, 1.0, 0.0).astype(BF16)
    lane_is_fwd = lane % (2 * HPG) < HPG
    sh = lax.broadcasted_iota(jnp.int32, (DT_LANES, 2 * HPG * CHUNK), 0) % (2 * HPG)
    sj = lax.broadcasted_iota(jnp.int32, (DT_LANES, 2 * HPG * CHUNK), 1) // CHUNK
    col_sel = jnp.where(sh == sj, 1.0, 0.0).astype(BF16)
    tri_both_k = jnp.concatenate([tri_lo_k, tri_up_k], axis=0)
    tri_both_r = jnp.concatenate([tri_up_r, tri_lo_r], axis=1)
    lane_c = lax.broadcasted_iota(jnp.int32, (1, CHUNK), 1)
    first_half = lane_c < HEADDIM

    def lane_pieces(v):
        pieces = _split_pieces(v)
        out = pieces[N_SPLIT - 1]
        for r in range(N_SPLIT - 2, -1, -1):
            out = jnp.where(piece_of_lane == r, pieces[r], out)
        return out

    def expand_many(vs, e):
        out = jnp.dot(jnp.concatenate([lane_pieces(v) for v in vs], axis=0), e, preferred_element_type=F32)
        res, r0 = [], 0
        for v in vs:
            res.append(out[r0:r0 + v.shape[0], :])
            r0 += v.shape[0]
        return res

    def rows8(v):
        return jnp.broadcast_to(v, (8, DT_LANES))

    for d in range(2):
        if has_h0:
            st_ref[d] = h0_ref[d].T
        else:
            st_ref[d] = jnp.zeros((STATE, GW), F32)

    y_ref[...] = jnp.zeros(y_ref.shape, F32)

    def body(k_, carry):
        cf = k_
        x = xs_ref[cf]
        b = bm_ref[cf]
        cm = cm_ref[cf]
        dt = dt_ref[cf]
        dtT = dtT_ref[cf]
        cr = n_chunks - 1 - k_
        xr = xs_ref[cr]
        br = bm_ref[cr]
        dtr = dt_ref[cr]

        dta_rows = jnp.concatenate(_split_pieces(dt * a_row), axis=0)
        dtaT_lanes = jnp.concatenate(_split_pieces(dtT * a_col), axis=1)
        acs2 = jnp.dot(tri_both_k, dta_rows, preferred_element_type=F32)
        acs_lo, acs_up = acs2[:CHUNK, :], acs2[CHUNK:, :]
        acs_t2 = jnp.dot(dtaT_lanes, tri_both_r, preferred_element_type=F32)
        acs_up_r = jnp.dot(tri_up_k, jnp.concatenate(_split_pieces(dtr * a_row), axis=0),
                           preferred_element_type=F32)
        r_t = jnp.log(dtT) * LOG2E - jnp.where(row_is_fwd, acs_t2[:, :CHUNK], acs_t2[:, CHUNK:])
        col_b = jnp.dot(lane_pieces(jnp.where(lane_is_fwd, acs_lo, acs_up)), col_sel, preferred_element_type=F32)

        a_last = acs_lo[CHUNK - 1:CHUNK, :]
        a_tot = acs_up_r[0:1, :]
        cb_diag = jnp.sum(cm * b, axis=1, keepdims=True)
        e_out_f, e_in_f, e_keep_f = expand_many(
            [jnp.exp2(acs_lo), jnp.exp2(a_last - acs_lo) * dt, rows8(jnp.exp2(a_last))], exp_f)
        e_self, e_out_b, e_in_b, e_keep_b = expand_many(
            [dsk_ref[...] + cb_diag * dt, jnp.exp2(acs_up_r), jnp.exp2(a_tot - acs_up_r) * dtr,
             rows8(jnp.exp2(a_tot))], exp_b)

        cmb = cm.astype(BF16)
        cb = lax.dot_general(cmb, b.astype(BF16), (((1,), (1,)), ((), ())), preferred_element_type=F32)
        parts = []
        for hp in range(HPG // 2):
            ws = []
            for h in (2 * hp, 2 * hp + 1):
                arg = jnp.where(lower, col_b[:, h * CHUNK:(h + 1) * CHUNK] + r_t[h:h + 1, :],
                                col_b[:, (HPG + h) * CHUNK:(HPG + h + 1) * CHUNK] + r_t[HPG + h:HPG + h + 1, :])
                ws.append((cb * jnp.exp2(arg)).astype(BF16))
            xp = x[:, hp * CHUNK:(hp + 1) * CHUNK]
            x2 = jnp.concatenate([jnp.where(first_half, xp, 0.0), jnp.where(first_half, 0.0, xp)], axis=0)
            parts.append(jnp.dot(jnp.concatenate(ws, axis=1), x2.astype(BF16), preferred_element_type=F32))
        st_f = st_ref[0]
        y = (jnp.concatenate(parts, axis=1) + e_self * x
             + jnp.dot(cmb, st_f.astype(BF16), preferred_element_type=F32) * e_out_f)
        y_ref[cf] = y_ref[cf] + y
        st_ref[0] = st_f * e_keep_f[0:1, :] + jnp.dot(
            b.T.astype(BF16), (x * e_in_f).astype(BF16), preferred_element_type=F32)

        st_b = st_ref[1]
        y_ref[cr] = y_ref[cr] + jnp.dot(cm_ref[cr].astype(BF16), st_b.astype(BF16),
                                        preferred_element_type=F32) * e_out_b
        st_ref[1] = st_b * e_keep_b[0:1, :] + jnp.dot(
            br.T.astype(BF16), (xr * e_in_b).astype(BF16), preferred_element_type=F32)
        return carry

    lax.fori_loop(0, n_chunks, body, 0)
    if emit_state:
        for d in range(2):
            st_out_ref[d] = st_ref[d].T


def _ssd_scan(xs3, bm3, cm3, dt4, dtT4, alog4, alogT4, dskx, h0, y_prev, st_prev, *, n_seq, seq_len, row_block0,
              state_layer):
    nck = seq_len // CHUNK
    has_h0 = h0 is not None
    emit_state = state_layer is not None
    rb = lambda s: s + row_block0
    in_specs = [
        pl.BlockSpec((nck, CHUNK, GW), lambda s, g: (rb(s), 0, g)),
        pl.BlockSpec((nck, CHUNK, STATE), lambda s, g: (rb(s), 0, g)),
        pl.BlockSpec((nck, CHUNK, STATE), lambda s, g: (rb(s), 0, g)),
        pl.BlockSpec((None, nck, CHUNK, DT_LANES), lambda s, g: (g, rb(s), 0, 0)),
        pl.BlockSpec((None, nck, 2 * HPG, CHUNK), lambda s, g: (g, rb(s), 0, 0)),
        pl.BlockSpec((None, 1, DT_LANES), lambda s, g: (g, 0, 0)),
        pl.BlockSpec((None, 2 * HPG, 1), lambda s, g: (g, 0, 0)),
        pl.BlockSpec((None, 1, DT_LANES), lambda s, g: (g, 0, 0)),
    ]
    args = [xs3, bm3, cm3, dt4, dtT4, alog4, alogT4, dskx]
    if has_h0:
        in_specs.append(pl.BlockSpec((None, 2, GW, STATE), lambda s, g: (s, 0, g, 0)))
        args.append(h0)
    aliases = {}
    for out_idx, kept in enumerate((y_prev, st_prev)):
        if kept is not None:
            in_specs.append(pl.BlockSpec(memory_space=pl.ANY))
            aliases[len(args)] = out_idx
            args.append(kept)
    y_shape = jax.ShapeDtypeStruct((R // CHUNK, CHUNK, D_INNER), F32)
    y_spec = pl.BlockSpec((nck, CHUNK, GW), lambda s, g: (rb(s), 0, g))
    if emit_state:
        out_shape = (y_shape, jax.ShapeDtypeStruct((n_seq, DEPTH // 2, 2, HEADS * HEADDIM, STATE), F32))
        out_specs = (y_spec, pl.BlockSpec((None, None, 2, GW, STATE), lambda s, g: (s, state_layer, 0, g, 0)))
    else:
        out_shape, out_specs = y_shape, y_spec
    return pl.pallas_call(
        functools.partial(_ssd_scan_kernel, n_chunks=nck, has_h0=has_h0, n_kept=len(aliases),
                          emit_state=emit_state),
        out_shape=out_shape,
        grid=(n_seq, GROUPS),
        in_specs=in_specs,
        out_specs=out_specs,
        scratch_shapes=[pltpu.VMEM((2, STATE, GW), F32)],
        input_output_aliases=aliases,
        compiler_params=_cparams(("arbitrary", "arbitrary")),
        name="ssd_scan",
    )(*args)


def _ssd_out_kernel(x_ref, y_ref, z_ref, mod_ref, nw_ref, wout_ref, o_ref):
    v = y_ref[...] * _silu(z_ref[...])
    v = v * lax.rsqrt(jnp.mean(v * v, axis=-1, keepdims=True) + EPS) * nw_ref[...]
    mix = jnp.dot(v.astype(BF16), wout_ref[...], preferred_element_type=F32)
    o_ref[...] = x_ref[...] + mod_ref[2:3, :] * mix


def _ssd_out(x, y, z, mod_l, nw, w_out):
    return pl.pallas_call(
        _ssd_out_kernel,
        out_shape=jax.ShapeDtypeStruct((R, D), F32),
        grid=(R // TM,),
        in_specs=[
            pl.BlockSpec((TM, D), lambda i: (i, 0)),
            pl.BlockSpec((TM, D_INNER), lambda i: (i, 0)),
            pl.BlockSpec((TM, D_INNER), lambda i: (i, 0)),
            pl.BlockSpec((None, MOD_CHUNKS, D), lambda i: (_cond_of_tile(i), 0, 0)),
            pl.BlockSpec((1, D_INNER), lambda i: (0, 0)),
            pl.BlockSpec((D_INNER, D), lambda i: (0, 0)),
        ],
        out_specs=pl.BlockSpec((TM, D), lambda i: (i, 0)),
        compiler_params=_cparams(("arbitrary",)),
        name="ssd_out",
    )(x, y, z, mod_l, nw, w_out)


def _router_kernel(x_ref, mod_ref, nw_ref, wrT_ref, hn_ref, affT_ref):
    hn = _norm_mod(x_ref[...], nw_ref[...], mod_ref[4:5, :], mod_ref[3:4, :])
    hn_hi = hn.astype(BF16)
    hn_ref[...] = hn_hi
    hn_lo = (hn - hn_hi.astype(F32)).astype(BF16)
    w = wrT_ref[...]
    w_hi = w.astype(BF16)
    w_lo = (w - w_hi.astype(F32)).astype(BF16)
    logits = lax.dot_general(jnp.concatenate([w_hi, w_lo, w_hi], axis=1), jnp.concatenate([hn_hi, hn_hi, hn_lo], axis=1),
                             (((1,), (1,)), ((), ())), preferred_element_type=F32)
    e = jnp.exp(logits - jnp.max(logits, axis=0, keepdims=True))
    affT_ref[...] = e / jnp.sum(e, axis=0, keepdims=True)


def _router(x, mod_l, nw, wrT):
    return pl.pallas_call(
        _router_kernel,
        out_shape=(jax.ShapeDtypeStruct((R, D), BF16), jax.ShapeDtypeStruct((N_EXPERTS, R), F32)),
        grid=(R // TM,),
        in_specs=[
            pl.BlockSpec((TM, D), lambda i: (i, 0)),
            pl.BlockSpec((None, MOD_CHUNKS, D), lambda i: (_cond_of_tile(i), 0, 0)),
            pl.BlockSpec((1, D), lambda i: (0, 0)),
            pl.BlockSpec((N_EXPERTS, D), lambda i: (0, 0)),
        ],
        out_specs=(pl.BlockSpec((TM, D), lambda i: (i, 0)), pl.BlockSpec((N_EXPERTS, TM), lambda i: (0, i))),
        compiler_params=_cparams(("arbitrary",)),
        name="router",
    )(x, mod_l, nw, wrT)


def _excl_cumsum_lanes(m):
    blk = 256
    t = m.shape[1]
    a = lax.broadcasted_iota(jnp.int32, (blk, blk), 0)
    b = lax.broadcasted_iota(jnp.int32, (blk, blk), 1)
    strict = jnp.where(a < b, 1.0, 0.0).astype(BF16)
    carry = jnp.zeros((m.shape[0], 1), F32)
    outs = []
    for k in range(t // blk):
        mk = m[:, k * blk:(k + 1) * blk]
        outs.append(jnp.dot(mk.astype(BF16), strict, preferred_element_type=F32) + carry)
        carry = carry + jnp.sum(mk, axis=1, keepdims=True)
    return outs[0] if len(outs) == 1 else jnp.concatenate(outs, axis=1)


def _select_request(aff, cap, base):
    bits = pltpu.bitcast(aff, jnp.int32)
    thr = jnp.zeros((N_EXPERTS, 1), jnp.int32)
    for k in range(30, -1, -1):
        trial = thr | (1 << k)
        cnt = jnp.sum(jnp.where(bits >= trial, 1.0, 0.0), axis=1, keepdims=True)
        thr = jnp.where(cnt >= cap, trial, thr)
    gt = bits > thr
    eq = jnp.where(bits == thr, 1.0, 0.0)
    need = cap - jnp.sum(jnp.where(gt, 1.0, 0.0), axis=1, keepdims=True)
    sel = gt | ((eq > 0.0) & (_excl_cumsum_lanes(eq) < need))
    pos = _excl_cumsum_lanes(jnp.where(sel, 1.0, 0.0))
    return jnp.where(sel, pos + base, -1.0), jnp.where(sel, aff, 0.0)


def _select_kernel(affT_ref, slotT_ref, slot_tok_ref, gate_tok_ref, lo_ref):
    s = pl.program_id(0)

    def emit(slot, gate):
        slotT_ref[...] = slot.astype(jnp.int32)
        pad = jnp.zeros((128 - N_EXPERTS, SR_TOKENS), F32)
        slot_tok_ref[...] = jnp.concatenate([slot, pad - 1.0], axis=0).T.astype(jnp.int32)
        gate_tok_ref[...] = jnp.concatenate([gate, pad], axis=0).T
        t = lax.broadcasted_iota(jnp.int32, (SR_TOKENS, 128), 0)
        jcol = lax.broadcasted_iota(jnp.int32, (SR_TOKENS, 128), 1)
        before = jnp.where(t < jcol * BLK, 1.0, 0.0).astype(BF16)
        chosen = jnp.where(slot >= 0.0, 1.0, 0.0).astype(BF16)
        lo_ref[...] = jnp.dot(chosen, before, preferred_element_type=F32).astype(jnp.int32)

    @pl.when(s < N_SR_CTX)
    def _():
        outs = [_select_request(affT_ref[:, r * L_CTX:(r + 1) * L_CTX], CAP_CTX, float(r * CAP_CTX))
                for r in range(CTX_PER_SR)]
        emit(jnp.concatenate([o[0] for o in outs], axis=1), jnp.concatenate([o[1] for o in outs], axis=1))

    @pl.when(s >= N_SR_CTX)
    def _():
        emit(*_select_request(affT_ref[...], CAP_LAT, 0.0))


def _select(affT):
    return pl.pallas_call(
        _select_kernel,
        out_shape=(
            jax.ShapeDtypeStruct((N_EXPERTS, R), jnp.int32),
            jax.ShapeDtypeStruct((R, 128), jnp.int32),
            jax.ShapeDtypeStruct((R, 128), F32),
            jax.ShapeDtypeStruct((N_SR, N_EXPERTS, 128), jnp.int32),
        ),
        grid=(N_SR,),
        in_specs=[pl.BlockSpec((N_EXPERTS, SR_TOKENS), lambda s: (0, s))],
        out_specs=(
            pl.BlockSpec((N_EXPERTS, SR_TOKENS), lambda s: (0, s)),
            pl.BlockSpec((SR_TOKENS, 128), lambda s: (s, 0)),
            pl.BlockSpec((SR_TOKENS, 128), lambda s: (s, 0)),
            pl.BlockSpec((None, N_EXPERTS, 128), lambda s: (s, 0, 0)),
        ),
        compiler_params=_cparams(("arbitrary",)),
        name="select",
    )(affT)


def _lo_at(lo_ref, sr, e, j):
    return lo_ref[(sr * N_EXPERTS + e) * LO_COLS + j]


def _gather_kernel(lo_ref, hn_ref, slotT_ref, xe_ref, acc_ref):
    sr = pl.program_id(0)
    e0 = pl.program_id(1) * E_HALF
    starts = {}
    fits = None
    for e in range(E_HALF):
        for j in range(N_BLK):
            start = (_lo_at(lo_ref, sr, e0 + e, j) // BF16_ROWS) * BF16_ROWS
            ok = _lo_at(lo_ref, sr, e0 + e, j + 1) - start <= GW_ROWS
            fits = ok if fits is None else jnp.logical_and(fits, ok)
            starts[e, j] = start

    @pl.when(fits)
    def _():
        acc_ref[...] = jnp.zeros(acc_ref.shape, BF16)
        w_iota = lax.broadcasted_iota(jnp.int32, (GW_ROWS, BLK), 0)
        for j in range(N_BLK):
            rows = [jnp.where(w_iota == slotT_ref[e:e + 1, j * BLK:(j + 1) * BLK] - starts[e, j], 1.0, 0.0).astype(BF16)
                    for e in range(E_HALF)]
            part = jnp.dot(jnp.concatenate(rows, axis=0), hn_ref[j * BLK:(j + 1) * BLK, :],
                           preferred_element_type=F32).astype(BF16)
            for e in range(E_HALF):
                win = pl.ds(pl.multiple_of(starts[e, j], BF16_ROWS), GW_ROWS)
                acc_ref[e, win, :] = acc_ref[e, win, :] + part[e * GW_ROWS:(e + 1) * GW_ROWS, :]
        xe_ref[...] = acc_ref[:, :SR_SLOTS, :]

    @pl.when(jnp.logical_not(fits))
    def _():
        s_iota = lax.broadcasted_iota(jnp.int32, (SR_SLOTS, SR_TOKENS), 0)
        for e in range(E_HALF):
            onehot = jnp.where(s_iota == slotT_ref[e:e + 1, :], 1.0, 0.0).astype(BF16)
            xe_ref[e] = jnp.dot(onehot, hn_ref[...], preferred_element_type=F32).astype(BF16)


def _gather(lo_flat, hn, slotT):
    return pl.pallas_call(
        _gather_kernel,
        out_shape=jax.ShapeDtypeStruct((N_EXPERTS, N_SR * SR_SLOTS, D), BF16),
        grid_spec=pltpu.PrefetchScalarGridSpec(
            num_scalar_prefetch=1,
            grid=(N_SR, N_EXPERTS // E_HALF),
            in_specs=[
                pl.BlockSpec((SR_TOKENS, D), lambda s, h, lo: (s, 0)),
                pl.BlockSpec((E_HALF, SR_TOKENS), lambda s, h, lo: (h, s)),
            ],
            out_specs=pl.BlockSpec((E_HALF, SR_SLOTS, D), lambda s, h, lo: (h, s, 0)),
            scratch_shapes=[pltpu.VMEM((E_HALF, SR_SLOTS + GW_ROWS, D), BF16)],
        ),
        compiler_params=_cparams(("arbitrary", "arbitrary")),
        name="gather",
    )(lo_flat, hn, slotT)


def _ffn_kernel(xe_ref, wg_ref, wu_ref, wd_ref, ye_ref, acc_ref):
    f = pl.program_id(1)
    xe = xe_ref[...]
    hg = jnp.dot(xe, wg_ref[...].astype(BF16), preferred_element_type=F32)
    hu = jnp.dot(xe, wu_ref[...].astype(BF16), preferred_element_type=F32)
    hid = (_silu(hg) * hu).astype(BF16)
    part = jnp.dot(hid, wd_ref[...].astype(BF16), preferred_element_type=F32)

    @pl.when(f == 0)
    def _():
        acc_ref[...] = part

    @pl.when(f > 0)
    def _():
        acc_ref[...] = acc_ref[...] + part

    @pl.when(f == pl.num_programs(1) - 1)
    def _():
        ye_ref[...] = acc_ref[...].astype(BF16)


def _ffn(xe, wg, wu, wd, layer):
    tf = 512
    m = xe.shape[1]
    return pl.pallas_call(
        _ffn_kernel,
        out_shape=jax.ShapeDtypeStruct((N_EXPERTS, m, D), BF16),
        grid=(N_EXPERTS, D // tf),
        in_specs=[
            pl.BlockSpec((None, m, D), lambda e, f: (e, 0, 0)),
            pl.BlockSpec((None, None, D, tf), lambda e, f: (layer, e, 0, f)),
            pl.BlockSpec((None, None, D, tf), lambda e, f: (layer, e, 0, f)),
            pl.BlockSpec((None, None, tf, D), lambda e, f: (layer, e, f, 0)),
        ],
        out_specs=pl.BlockSpec((None, m, D), lambda e, f: (e, 0, 0)),
        scratch_shapes=[pltpu.VMEM((m, D), F32)],
        compiler_params=_cparams(("arbitrary", "arbitrary")),
        name="expert_ffn",
    )(xe, wg, wu, wd)


def _combine_kernel(lo_ref, x_ref, ye_ref, slot_ref, gate_ref, mod_ref, fnw_ref, o_ref, yw_ref, acc_ref, *,
                    final_norm):
    sr = pl.program_id(0)
    j = pl.program_id(1)
    starts = []
    fits = None
    for e in range(N_EXPERTS):
        start = jnp.minimum((_lo_at(lo_ref, sr, e, j) // BF16_ROWS) * BF16_ROWS, SR_SLOTS - CW_ROWS)
        ok = _lo_at(lo_ref, sr, e, j + 1) - start <= CW_ROWS
        fits = ok if fits is None else jnp.logical_and(fits, ok)
        starts.append(start)
    slot = slot_ref[...]
    gate = gate_ref[...]

    @pl.when(fits)
    def _():
        for e in range(N_EXPERTS):
            yw_ref[e * CW_ROWS:(e + 1) * CW_ROWS, :] = ye_ref[e, pl.ds(pl.multiple_of(starts[e], BF16_ROWS), CW_ROWS), :]
        lane = lax.broadcasted_iota(jnp.int32, (1, 128), 1)
        low = lane < CW_ROWS
        his, los = [], []
        for p in range(N_EXPERTS // 2):
            ea, eb = 2 * p, 2 * p + 1
            rel = jnp.where(low, slot[:, ea:ea + 1] - starts[ea], slot[:, eb:eb + 1] - starts[eb] + CW_ROWS)
            g = jnp.where(rel == lane, jnp.where(low, gate[:, ea:ea + 1], gate[:, eb:eb + 1]), 0.0)
            g_hi = g.astype(BF16)
            his.append(g_hi)
            los.append((g - g_hi.astype(F32)).astype(BF16))
        yw = yw_ref[...]
        acc_ref[...] = (jnp.dot(jnp.concatenate(his, axis=1), yw, preferred_element_type=F32)
                        + jnp.dot(jnp.concatenate(los, axis=1), yw, preferred_element_type=F32))

    @pl.when(jnp.logical_not(fits))
    def _():
        s_iota = lax.broadcasted_iota(jnp.int32, (BLK, SR_SLOTS), 1)
        acc = jnp.zeros((BLK, D), F32)
        for e in range(N_EXPERTS):
            onehot = jnp.where(slot[:, e:e + 1] == s_iota, 1.0, 0.0).astype(BF16)
            acc = acc + gate[:, e:e + 1] * jnp.dot(onehot, ye_ref[e], preferred_element_type=F32)
        acc_ref[...] = acc

    out = x_ref[...] + mod_ref[5:6, :] * acc_ref[...]
    if final_norm:
        out = out * lax.rsqrt(jnp.mean(out * out, axis=-1, keepdims=True) + EPS) * fnw_ref[...]
    o_ref[...] = out


def _combine(lo_flat, x, ye, slot_tok, gate_tok, mod_l, fnw, *, final_norm):
    blk_of = lambda s, j: s * N_BLK + j
    return pl.pallas_call(
        functools.partial(_combine_kernel, final_norm=final_norm),
        out_shape=jax.ShapeDtypeStruct((R, D), F32),
        grid_spec=pltpu.PrefetchScalarGridSpec(
            num_scalar_prefetch=1,
            grid=(N_SR, N_BLK),
            in_specs=[
                pl.BlockSpec((BLK, D), lambda s, j, lo: (blk_of(s, j), 0)),
                pl.BlockSpec((N_EXPERTS, SR_SLOTS, D), lambda s, j, lo: (0, s, 0)),
                pl.BlockSpec((BLK, 128), lambda s, j, lo: (blk_of(s, j), 0)),
                pl.BlockSpec((BLK, 128), lambda s, j, lo: (blk_of(s, j), 0)),
                pl.BlockSpec((None, MOD_CHUNKS, D),
                             lambda s, j, lo: (_cond_of_tile(blk_of(s, j) * BLK // TM), 0, 0)),
                pl.BlockSpec((1, D), lambda s, j, lo: (0, 0)),
            ],
            out_specs=pl.BlockSpec((BLK, D), lambda s, j, lo: (blk_of(s, j), 0)),
            scratch_shapes=[pltpu.VMEM((N_EXPERTS * CW_ROWS, D), BF16), pltpu.VMEM((BLK, D), F32)],
        ),
        compiler_params=_cparams(("arbitrary", "arbitrary")),
        name="combine",
    )(lo_flat, x, ye, slot_tok, gate_tok, mod_l, fnw)


def _moe(x, mod_l, nw, wrT, wg, wu, wd, fnw, layer, *, final_norm):
    hn, affT = _router(x, mod_l, nw, wrT)
    slotT, slot_tok, gate_tok, lo = _select(affT)
    lo_flat = lo[:, :, :LO_COLS].reshape(-1)
    xe = _gather(lo_flat, hn, slotT)
    ye = _ffn(xe, wg, wu, wd, layer)
    return _combine(lo_flat, x, ye, slot_tok, gate_tok, mod_l, fnw, final_norm=final_norm)


def _group_major(p):
    return p.reshape(2, GROUPS, HPG).transpose(1, 0, 2).reshape(GROUPS, 2 * HPG)


def kernel(x_prompt, x_sample, state_ssm, c, c_ctx, norm1_w, norm2_w, w_mod, b_mod, conv_in_w, conv_w, conv_out_w, ssd_in_w, ssd_conv_w, ssd_conv_b, ssd_dt_bias, ssd_a_log, ssd_d, ssd_norm_w, ssd_out_w, router_w, exp_w_gate, exp_w_up, exp_w_down, final_norm_w):
    x = jnp.concatenate([x_prompt.reshape(R_CTX, D), x_sample.reshape(R_LAT, D)], axis=0)
    cond = jnp.concatenate([c_ctx[None, :], c, jnp.zeros((8 - N_COND, D), F32)], axis=0)
    mods = _modulation(cond.T, w_mod, b_mod)
    mods = mods[:, :N_COND].reshape(DEPTH, N_COND, MOD_CHUNKS, D)
    fnw = final_norm_w.reshape(1, D)

    states = None
    for layer in range(DEPTH):
        mod_l = mods[layer]
        j = layer // 2
        nw1 = norm1_w[layer].reshape(1, D)
        if layer % 2 == 0:
            x = _conv_mixer(x, mod_l, nw1, conv_in_w[j].astype(BF16), conv_w[j], conv_out_w[j].astype(BF16))
        else:
            w_in = ssd_in_w[j]
            w_dt = w_in[:, D_INNER + CONV_DIM:].reshape(D, 2, GROUPS, HPG).transpose(0, 2, 1, 3)
            w_dt_rep = jnp.broadcast_to(w_dt.reshape(D, GROUPS, 1, 2 * HPG), (D, GROUPS, N_SPLIT, 2 * HPG))
            w_in_b = jnp.concatenate([w_in[:, :D_INNER + CONV_DIM], w_dt_rep.reshape(D, GROUPS * DT_LANES)],
                                     axis=1).astype(BF16)
            dtb = _group_major(ssd_dt_bias[j])
            dtb_rep = jnp.tile(dtb, (1, N_SPLIT)).reshape(1, GROUPS * DT_LANES)
            z, xs, bm, cm, dt4, dtT4 = _ssd_in(
                x, mod_l, nw1, w_in_b, w_dt.reshape(D, 2 * HEADS).T.astype(BF16), ssd_conv_w[j],
                ssd_conv_b[j].reshape(1, CONV_DIM), dtb_rep, dtb.reshape(2 * HEADS, 1))
            alog4 = _group_major(ssd_a_log[j])
            dsum = (ssd_d[j][0] + ssd_d[j][1]).reshape(GROUPS, HPG)
            dsk4 = jnp.tile(jnp.concatenate([jnp.zeros_like(dsum), dsum], axis=1), (1, N_SPLIT))
            scan_args = (xs.reshape(R // CHUNK, CHUNK, D_INNER), bm.reshape(R // CHUNK, CHUNK, GROUPS * STATE),
                         cm.reshape(R // CHUNK, CHUNK, GROUPS * STATE), dt4, dtT4,
                         jnp.tile(alog4, (1, N_SPLIT)).reshape(GROUPS, 1, DT_LANES),
                         alog4.reshape(GROUPS, 2 * HPG, 1), dsk4.reshape(GROUPS, 1, DT_LANES))
            y, states = _ssd_scan(*scan_args, None, None, states, n_seq=N_CTX, seq_len=L_CTX, row_block0=0,
                                  state_layer=j)
            h0 = state_ssm[:, j].reshape(N_LAT, 2, HEADS * HEADDIM, STATE)
            y = _ssd_scan(*scan_args, h0, y, None, n_seq=N_LAT, seq_len=L_LAT, row_block0=R_CTX // L_LAT,
                          state_layer=None)
            x = _ssd_out(x, y.reshape(R, D_INNER), z, mod_l, ssd_norm_w[j].reshape(1, D_INNER),
                         ssd_out_w[j].astype(BF16))
        x = _moe(x, mod_l, norm2_w[layer].reshape(1, D), router_w[layer].T, exp_w_gate, exp_w_up, exp_w_down, fnw,
                 layer, final_norm=(layer == DEPTH - 1))

    y_prompt = x[:R_CTX].reshape(N_CTX, L_CTX, D)
    y_sample = x[R_CTX:].reshape(N_LAT, L_LAT, D)
    return y_prompt, y_sample, states.reshape(N_CTX, DEPTH // 2, 2, HEADS, HEADDIM, STATE)
```

```python
import functools

import jax
import jax.numpy as jnp
from jax import lax
from jax.experimental import pallas as pl
from jax.experimental.pallas import tpu as pltpu

F32 = jnp.float32
BF16 = jnp.bfloat16
HIGHEST = lax.Precision.HIGHEST
LOG2E = 1.4426950408889634

D = 1024
DEPTH = 4
N_CTX, L_CTX = 16, 256
N_LAT, L_LAT = 2, 2048
GRID_W = 64
R_CTX = N_CTX * L_CTX
R_LAT = N_LAT * L_LAT
R = R_CTX + R_LAT
N_COND = 1 + N_LAT
MOD_CHUNKS = 6
D_INNER = 2048
HEADDIM = 64
HEADS = 32
GROUPS = 4
HPG = HEADS // GROUPS
GW = HPG * HEADDIM
STATE = 128
CHUNK = 128
CONV_DIM = D_INNER + 2 * GROUPS * STATE
N_SPLIT = 3
DT_LANES = N_SPLIT * 2 * HPG
N_EXPERTS = 16
EPS = 1e-6

TM = 512
TM_OUT = 256
CTX_TILES = R_CTX // TM
LAT_TILES_PER_REQ = L_LAT // TM
SR_TOKENS = 2048
N_SR = R // SR_TOKENS
N_SR_CTX = R_CTX // SR_TOKENS
SR_SLOTS = 2 * SR_TOKENS // N_EXPERTS
CTX_PER_SR = SR_TOKENS // L_CTX
CAP_CTX = 2 * L_CTX // N_EXPERTS
CAP_LAT = 2 * L_LAT // N_EXPERTS
BLK = 256
N_BLK = SR_TOKENS // BLK
LO_COLS = 16
BF16_ROWS = 16
GW_ROWS = 80
CW_ROWS = 64
E_HALF = N_EXPERTS // 2
VMEM_LIMIT = 56 * 1024 * 1024


def _cparams(sem):
    return pltpu.CompilerParams(dimension_semantics=sem, vmem_limit_bytes=VMEM_LIMIT)


def _cond_of_tile(i):
    return jnp.where(i < CTX_TILES, 0, 1 + (i - CTX_TILES) // LAT_TILES_PER_REQ)


def _silu(v):
    return v * (1.0 / (1.0 + jnp.exp(-v)))


def _softplus(v):
    return jnp.maximum(v, 0.0) + jnp.log1p(jnp.exp(-jnp.abs(v)))


def _norm_mod(x, nw, scale, shift):
    y = x * lax.rsqrt(jnp.mean(x * x, axis=-1, keepdims=True) + EPS)
    return y * nw * (1.0 + scale) + shift


def _conv3_rows(p, w_ref, tile_idx):
    n = p.shape[0]
    period = jnp.where(tile_idx < CTX_TILES, L_CTX, GRID_W)
    r = lax.broadcasted_iota(jnp.int32, (n, 1), 0) & (period - 1)
    prev = jnp.where(r == 0, 0.0, pltpu.roll(p, 1, axis=0))
    nxt = jnp.where(r == period - 1, 0.0, pltpu.roll(p, n - 1, axis=0))
    return prev * w_ref[0:1, :] + p * w_ref[1:2, :] + nxt * w_ref[2:3, :]


def _split_pieces(v):
    pieces = []
    r = v
    for _ in range(N_SPLIT):
        p = r.astype(BF16)
        pieces.append(p)
        r = r - p.astype(F32)
    return pieces


def _modulation_kernel(condT_ref, w_ref, b_ref, o_ref):
    s = _silu(condT_ref[...])
    w = w_ref[...]
    rows = [jnp.sum(w * s[:, r:r + 1], axis=0, keepdims=True) + b_ref[...] for r in range(N_COND)]
    rows.append(jnp.zeros((8 - N_COND, w.shape[1]), F32))
    o_ref[...] = jnp.concatenate(rows, axis=0)


def _modulation(condT, w_mod, b_mod):
    tn = 1536
    n = MOD_CHUNKS * D
    return pl.pallas_call(
        _modulation_kernel,
        out_shape=jax.ShapeDtypeStruct((DEPTH, 8, n), F32),
        grid=(DEPTH, n // tn),
        in_specs=[
            pl.BlockSpec((D, 8), lambda l, j: (0, 0)),
            pl.BlockSpec((None, D, tn), lambda l, j: (l, 0, j)),
            pl.BlockSpec((None, 1, tn), lambda l, j: (l, 0, j)),
        ],
        out_specs=pl.BlockSpec((None, 8, tn), lambda l, j: (l, 0, j)),
        compiler_params=_cparams(("arbitrary", "arbitrary")),
        name="modulation",
    )(condT, w_mod, b_mod.reshape(DEPTH, 1, n))


def _conv_mixer_kernel(x_ref, mod_ref, nw_ref, win_ref, cw_ref, wout_ref, o_ref):
    i = pl.program_id(0)
    x = x_ref[...]
    hn = _norm_mod(x, nw_ref[...], mod_ref[1:2, :], mod_ref[0:1, :]).astype(BF16)
    gb = jnp.dot(hn, win_ref[:, 0:D], preferred_element_type=F32)
    gc = jnp.dot(hn, win_ref[:, D:2 * D], preferred_element_type=F32)
    v = jnp.dot(hn, win_ref[:, 2 * D:3 * D], preferred_element_type=F32)
    q = (gb * _conv3_rows(gc * v, cw_ref, i)).astype(BF16)
    mix = jnp.dot(q, wout_ref[...], preferred_element_type=F32)
    o_ref[...] = x + mod_ref[2:3, :] * mix


def _conv_mixer(x, mod_l, nw, w_in, cw, w_out):
    return pl.pallas_call(
        _conv_mixer_kernel,
        out_shape=jax.ShapeDtypeStruct((R, D), F32),
        grid=(R // TM,),
        in_specs=[
            pl.BlockSpec((TM, D), lambda i: (i, 0)),
            pl.BlockSpec((None, MOD_CHUNKS, D), lambda i: (_cond_of_tile(i), 0, 0)),
            pl.BlockSpec((1, D), lambda i: (0, 0)),
            pl.BlockSpec((D, 3 * D), lambda i: (0, 0)),
            pl.BlockSpec((3, D), lambda i: (0, 0)),
            pl.BlockSpec((D, D), lambda i: (0, 0)),
        ],
        out_specs=pl.BlockSpec((TM, D), lambda i: (i, 0)),
        compiler_params=_cparams(("arbitrary",)),
        name="conv_mixer",
    )(x, mod_l, nw, w_in, cw, w_out)


def _ssd_in_kernel(x_ref, mod_ref, nw_ref, win_ref, wdtT_ref, cw_ref, cb_ref, dtb_ref, dtbT_ref,
                   z_ref, xs_ref, bm_ref, cm_ref, dt_ref, dtT_ref):
    i = pl.program_id(0)
    hn = _norm_mod(x_ref[...], nw_ref[...], mod_ref[1:2, :], mod_ref[0:1, :]).astype(BF16)
    for k in range(D_INNER // D):
        z_ref[:, k * D:(k + 1) * D] = jnp.dot(hn, win_ref[:, k * D:(k + 1) * D], preferred_element_type=F32)
    for k in range(CONV_DIM // D):
        lo = D_INNER + k * D
        u = jnp.dot(hn, win_ref[:, lo:lo + D], preferred_element_type=F32)
        u = _silu(_conv3_rows(u, cw_ref.at[:, k * D:(k + 1) * D], i) + cb_ref[:, k * D:(k + 1) * D])
        if k < D_INNER // D:
            xs_ref[:, k * D:(k + 1) * D] = u
        else:
            bm_ref[...] = u[:, :GROUPS * STATE]
            cm_ref[...] = u[:, GROUPS * STATE:]
    lo = D_INNER + CONV_DIM
    dt = _softplus(jnp.dot(hn, win_ref[:, lo:lo + GROUPS * DT_LANES], preferred_element_type=F32) + dtb_ref[...])
    dtT = _softplus(
        lax.dot_general(wdtT_ref[...], hn, (((1,), (1,)), ((), ())), preferred_element_type=F32) + dtbT_ref[...])
    for g in range(GROUPS):
        dt_ref[g] = dt[:, g * DT_LANES:(g + 1) * DT_LANES].reshape(TM // CHUNK, CHUNK, DT_LANES)
        for k in range(TM // CHUNK):
            dtT_ref[g, k] = dtT[g * 2 * HPG:(g + 1) * 2 * HPG, k * CHUNK:(k + 1) * CHUNK]


def _ssd_in(x, mod_l, nw, w_in, w_dtT, cw, cb, dtb, dtbT):
    n_in = w_in.shape[1]
    nck = TM // CHUNK
    return pl.pallas_call(
        _ssd_in_kernel,
        out_shape=(
            jax.ShapeDtypeStruct((R, D_INNER), F32),
            jax.ShapeDtypeStruct((R, D_INNER), F32),
            jax.ShapeDtypeStruct((R, GROUPS * STATE), F32),
            jax.ShapeDtypeStruct((R, GROUPS * STATE), F32),
            jax.ShapeDtypeStruct((GROUPS, R // CHUNK, CHUNK, DT_LANES), F32),
            jax.ShapeDtypeStruct((GROUPS, R // CHUNK, 2 * HPG, CHUNK), F32),
        ),
        grid=(R // TM,),
        in_specs=[
            pl.BlockSpec((TM, D), lambda i: (i, 0)),
            pl.BlockSpec((None, MOD_CHUNKS, D), lambda i: (_cond_of_tile(i), 0, 0)),
            pl.BlockSpec((1, D), lambda i: (0, 0)),
            pl.BlockSpec((D, n_in), lambda i: (0, 0)),
            pl.BlockSpec((2 * HEADS, D), lambda i: (0, 0)),
            pl.BlockSpec((3, CONV_DIM), lambda i: (0, 0)),
            pl.BlockSpec((1, CONV_DIM), lambda i: (0, 0)),
            pl.BlockSpec((1, GROUPS * DT_LANES), lambda i: (0, 0)),
            pl.BlockSpec((2 * HEADS, 1), lambda i: (0, 0)),
        ],
        out_specs=(
            pl.BlockSpec((TM, D_INNER), lambda i: (i, 0)),
            pl.BlockSpec((TM, D_INNER), lambda i: (i, 0)),
            pl.BlockSpec((TM, GROUPS * STATE), lambda i: (i, 0)),
            pl.BlockSpec((TM, GROUPS * STATE), lambda i: (i, 0)),
            pl.BlockSpec((GROUPS, nck, CHUNK, DT_LANES), lambda i: (0, i, 0, 0)),
            pl.BlockSpec((GROUPS, nck, 2 * HPG, CHUNK), lambda i: (0, i, 0, 0)),
        ),
        compiler_params=_cparams(("arbitrary",)),
        name="ssd_in",
    )(x, mod_l, nw, w_in, w_dtT, cw, cb, dtb, dtbT)


def _ssd_scan_kernel(*refs, n_chunks, has_h0, n_kept, state_slots):
    emit_state = state_slots is not None
    xs_ref, bm_ref, cm_ref, dt_ref, dtT_ref, alog_ref, alogT_ref, dsk_ref = refs[:8]
    k = 8
    h0_ref = None
    if has_h0:
        h0_ref = refs[k]
        k += 1
    k += n_kept
    y_ref = refs[k]
    k += 1
    st_out_ref = None
    if emit_state:
        st_out_ref = refs[k]
        k += 1
    st_ref = refs[k]

    a_row = -jnp.exp(alog_ref[...]) * LOG2E
    a_col = -jnp.exp(alogT_ref[...]) * LOG2E
    row_is_fwd = lax.broadcasted_iota(jnp.int32, (2 * HPG, 1), 0) < HPG
    qi = lax.broadcasted_iota(jnp.int32, (CHUNK, CHUNK), 0)
    si = lax.broadcasted_iota(jnp.int32, (CHUNK, CHUNK), 1)
    lower = si <= qi
    upper = si >= qi
    tri_lo = jnp.where(lower, 1.0, 0.0).astype(BF16)
    tri_up = jnp.where(upper, 1.0, 0.0).astype(BF16)
    tri_lo_k = jnp.concatenate([tri_lo] * N_SPLIT, axis=1)
    tri_up_k = jnp.concatenate([tri_up] * N_SPLIT, axis=1)
    tri_lo_r = jnp.concatenate([tri_lo] * N_SPLIT, axis=0)
    tri_up_r = jnp.concatenate([tri_up] * N_SPLIT, axis=0)
    lane = lax.broadcasted_iota(jnp.int32, (1, DT_LANES), 1)
    piece_of_lane = lane // (2 * HPG)
    eh = lax.broadcasted_iota(jnp.int32, (DT_LANES, GW), 0) % (2 * HPG)
    ej = lax.broadcasted_iota(jnp.int32, (DT_LANES, GW), 1) // HEADDIM
    exp_f = jnp.where(eh == ej, 1.0, 0.0).astype(BF16)
    exp_b = jnp.where(eh == ej + HPG, 1.0, 0.0).astype(BF16)
    lane_is_fwd = lane % (2 * HPG) < HPG
    sh = lax.broadcasted_iota(jnp.int32, (DT_LANES, 2 * HPG * CHUNK), 0) % (2 * HPG)
    sj = lax.broadcasted_iota(jnp.int32, (DT_LANES, 2 * HPG * CHUNK), 1) // CHUNK
    col_sel = jnp.where(sh == sj, 1.0, 0.0).astype(BF16)
    tri_both_k = jnp.concatenate([tri_lo_k, tri_up_k], axis=0)
    tri_both_r = jnp.concatenate([tri_up_r, tri_lo_r], axis=1)
    lane_c = lax.broadcasted_iota(jnp.int32, (1, CHUNK), 1)
    first_half = lane_c < HEADDIM

    def lane_pieces(v):
        pieces = _split_pieces(v)
        out = pieces[N_SPLIT - 1]
        for r in range(N_SPLIT - 2, -1, -1):
            out = jnp.where(piece_of_lane == r, pieces[r], out)
        return out

    def expand_many(vs, e):
        out = jnp.dot(jnp.concatenate([lane_pieces(v) for v in vs], axis=0), e, preferred_element_type=F32)
        res, r0 = [], 0
        for v in vs:
            res.append(out[r0:r0 + v.shape[0], :])
            r0 += v.shape[0]
        return res

    for d in range(2):
        if has_h0:
            st_ref[d] = h0_ref[d].T
        else:
            st_ref[d] = jnp.zeros((STATE, GW), F32)

    y_ref[...] = jnp.zeros(y_ref.shape, F32)

    def body(k_, carry):
        cf = k_
        x = xs_ref[cf]
        b = bm_ref[cf]
        cm = cm_ref[cf]
        dt = dt_ref[cf]
        dtT = dtT_ref[cf]
        cr = n_chunks - 1 - k_
        xr = xs_ref[cr]
        br = bm_ref[cr]
        dtr = dt_ref[cr]

        dta_rows = jnp.concatenate(_split_pieces(dt * a_row), axis=0)
        dtaT_lanes = jnp.concatenate(_split_pieces(dtT * a_col), axis=1)
        acs2 = jnp.dot(tri_both_k, dta_rows, preferred_element_type=F32)
        acs_lo, acs_up = acs2[:CHUNK, :], acs2[CHUNK:, :]
        acs_t2 = jnp.dot(dtaT_lanes, tri_both_r, preferred_element_type=F32)
        acs_up_r = jnp.dot(tri_up_k, jnp.concatenate(_split_pieces(dtr * a_row), axis=0),
                           preferred_element_type=F32)
        r_t = jnp.log(dtT) * LOG2E - jnp.where(row_is_fwd, acs_t2[:, :CHUNK], acs_t2[:, CHUNK:])
        col_b = jnp.dot(lane_pieces(jnp.where(lane_is_fwd, acs_lo, acs_up)), col_sel, preferred_element_type=F32)

        a_last = acs_lo[CHUNK - 1:CHUNK, :]
        a_tot = acs_up_r[0:1, :]
        cb_diag = jnp.sum(cm * b, axis=1, keepdims=True)
        e_out_f, e_in_f = expand_many([jnp.exp2(acs_lo), jnp.exp2(a_last - acs_lo) * dt], exp_f)
        e_self, e_out_b, e_in_b = expand_many(
            [dsk_ref[...] + cb_diag * dt, jnp.exp2(acs_up_r), jnp.exp2(a_tot - acs_up_r) * dtr], exp_b)
        e_keep_f = e_out_f[CHUNK - 1:CHUNK, :]
        e_keep_b = e_out_b[0:1, :]

        cmb = cm.astype(BF16)
        cb = lax.dot_general(cmb, b.astype(BF16), (((1,), (1,)), ((), ())), preferred_element_type=F32)
        parts = []
        for hp in range(HPG // 2):
            ws = []
            for h in (2 * hp, 2 * hp + 1):
                arg = jnp.where(lower, col_b[:, h * CHUNK:(h + 1) * CHUNK] + r_t[h:h + 1, :],
                                col_b[:, (HPG + h) * CHUNK:(HPG + h + 1) * CHUNK] + r_t[HPG + h:HPG + h + 1, :])
                ws.append((cb * jnp.exp2(arg)).astype(BF16))
            xp = x[:, hp * CHUNK:(hp + 1) * CHUNK]
            x2 = jnp.concatenate([jnp.where(first_half, xp, 0.0), jnp.where(first_half, 0.0, xp)], axis=0)
            parts.append(jnp.dot(jnp.concatenate(ws, axis=1), x2.astype(BF16), preferred_element_type=F32))
        st_f = st_ref[0]
        y = (jnp.concatenate(parts, axis=1) + e_self * x
             + jnp.dot(cmb, st_f.astype(BF16), preferred_element_type=F32) * e_out_f)
        y_ref[cf] = y_ref[cf] + y
        st_ref[0] = st_f * e_keep_f + jnp.dot(
            b.T.astype(BF16), (x * e_in_f).astype(BF16), preferred_element_type=F32)

        st_b = st_ref[1]
        y_ref[cr] = y_ref[cr] + jnp.dot(cm_ref[cr].astype(BF16), st_b.astype(BF16),
                                        preferred_element_type=F32) * e_out_b
        st_ref[1] = st_b * e_keep_b + jnp.dot(
            br.T.astype(BF16), (xr * e_in_b).astype(BF16), preferred_element_type=F32)
        return carry

    lax.fori_loop(0, n_chunks, body, 0)
    if emit_state:
        for slot, own in enumerate(state_slots):
            for d in range(2):
                st_out_ref[slot, d] = st_ref[d].T if own else jnp.zeros((GW, STATE), F32)


def _ssd_scan(xs3, bm3, cm3, dt4, dtT4, alog4, alogT4, dskx, h0, st_prev, *, n_seq, seq_len, row_block0, state_layer):
    nck = seq_len // CHUNK
    has_h0 = h0 is not None
    emit_state = state_layer is not None
    keep_state = st_prev is not None
    rb = lambda s: s + row_block0
    in_specs = [
        pl.BlockSpec((nck, CHUNK, GW), lambda s, g: (rb(s), 0, g)),
        pl.BlockSpec((nck, CHUNK, STATE), lambda s, g: (rb(s), 0, g)),
        pl.BlockSpec((nck, CHUNK, STATE), lambda s, g: (rb(s), 0, g)),
        pl.BlockSpec((None, nck, CHUNK, DT_LANES), lambda s, g: (g, rb(s), 0, 0)),
        pl.BlockSpec((None, nck, 2 * HPG, CHUNK), lambda s, g: (g, rb(s), 0, 0)),
        pl.BlockSpec((None, 1, DT_LANES), lambda s, g: (g, 0, 0)),
        pl.BlockSpec((None, 2 * HPG, 1), lambda s, g: (g, 0, 0)),
        pl.BlockSpec((None, 1, DT_LANES), lambda s, g: (g, 0, 0)),
    ]
    args = [xs3, bm3, cm3, dt4, dtT4, alog4, alogT4, dskx]
    if has_h0:
        in_specs.append(pl.BlockSpec((None, 2, GW, STATE), lambda s, g: (s, 0, g, 0)))
        args.append(h0)
    aliases = {}
    if keep_state:
        in_specs.append(pl.BlockSpec(memory_space=pl.ANY))
        aliases[len(args)] = 1
        args.append(st_prev)
    y_shape = jax.ShapeDtypeStruct((n_seq * nck, CHUNK, D_INNER), F32)
    y_spec = pl.BlockSpec((nck, CHUNK, GW), lambda s, g: (s, 0, g))
    n_layers = DEPTH // 2
    state_slots = None
    if emit_state:
        out_shape = (y_shape, jax.ShapeDtypeStruct((n_seq, n_layers, 2, HEADS * HEADDIM, STATE), F32))
        if keep_state:
            state_slots = (True,)
            st_spec = pl.BlockSpec((None, 1, 2, GW, STATE), lambda s, g: (s, state_layer, 0, g, 0))
        else:
            state_slots = tuple(l == state_layer for l in range(n_layers))
            st_spec = pl.BlockSpec((None, n_layers, 2, GW, STATE), lambda s, g: (s, 0, 0, g, 0))
        out_specs = (y_spec, st_spec)
    else:
        out_shape, out_specs = y_shape, y_spec
    return pl.pallas_call(
        functools.partial(_ssd_scan_kernel, n_chunks=nck, has_h0=has_h0, n_kept=len(aliases),
                          state_slots=state_slots),
        out_shape=out_shape,
        grid=(n_seq, GROUPS),
        in_specs=in_specs,
        out_specs=out_specs,
        scratch_shapes=[pltpu.VMEM((2, STATE, GW), F32)],
        input_output_aliases=aliases,
        compiler_params=_cparams(("arbitrary", "arbitrary")),
        name="ssd_scan",
    )(*args)


def _ssd_out_kernel(x_ref, yc_ref, yl_ref, z_ref, mod_ref, nw_ref, wout_ref, o_ref):
    y = jnp.where(pl.program_id(0) < R_CTX // TM_OUT, yc_ref[...], yl_ref[...])
    v = y * _silu(z_ref[...])
    v = v * lax.rsqrt(jnp.mean(v * v, axis=-1, keepdims=True) + EPS) * nw_ref[...]
    mix = jnp.dot(v.astype(BF16), wout_ref[...], preferred_element_type=F32)
    o_ref[...] = x_ref[...] + mod_ref[2:3, :] * mix


def _ssd_out(x, y_ctx, y_lat, z, mod_l, nw, w_out):
    n_ctx = R_CTX // TM_OUT
    return pl.pallas_call(
        _ssd_out_kernel,
        out_shape=jax.ShapeDtypeStruct((R, D), F32),
        grid=(R // TM_OUT,),
        in_specs=[
            pl.BlockSpec((TM_OUT, D), lambda i: (i, 0)),
            pl.BlockSpec((TM_OUT, D_INNER), lambda i: (jnp.minimum(i, n_ctx - 1), 0)),
            pl.BlockSpec((TM_OUT, D_INNER), lambda i: (jnp.maximum(i - n_ctx, 0), 0)),
            pl.BlockSpec((TM_OUT, D_INNER), lambda i: (i, 0)),
            pl.BlockSpec((None, MOD_CHUNKS, D), lambda i: (_cond_of_tile(i * TM_OUT // TM), 0, 0)),
            pl.BlockSpec((1, D_INNER), lambda i: (0, 0)),
            pl.BlockSpec((D_INNER, D), lambda i: (0, 0)),
        ],
        out_specs=pl.BlockSpec((TM_OUT, D), lambda i: (i, 0)),
        compiler_params=_cparams(("arbitrary",)),
        name="ssd_out",
    )(x, y_ctx, y_lat, z, mod_l, nw, w_out)


def _router_kernel(x_ref, mod_ref, nw_ref, wrT_ref, hn_ref, affT_ref):
    hn = _norm_mod(x_ref[...], nw_ref[...], mod_ref[4:5, :], mod_ref[3:4, :])
    hn_hi = hn.astype(BF16)
    hn_ref[...] = hn_hi
    hn_lo = (hn - hn_hi.astype(F32)).astype(BF16)
    w = wrT_ref[...]
    w_hi = w.astype(BF16)
    w_lo = (w - w_hi.astype(F32)).astype(BF16)
    logits = lax.dot_general(jnp.concatenate([w_hi, w_lo, w_hi], axis=1), jnp.concatenate([hn_hi, hn_hi, hn_lo], axis=1),
                             (((1,), (1,)), ((), ())), preferred_element_type=F32)
    e = jnp.exp(logits - jnp.max(logits, axis=0, keepdims=True))
    affT_ref[...] = e / jnp.sum(e, axis=0, keepdims=True)


def _router(x, mod_l, nw, wrT):
    return pl.pallas_call(
        _router_kernel,
        out_shape=(jax.ShapeDtypeStruct((R, D), BF16), jax.ShapeDtypeStruct((N_EXPERTS, R), F32)),
        grid=(R // TM,),
        in_specs=[
            pl.BlockSpec((TM, D), lambda i: (i, 0)),
            pl.BlockSpec((None, MOD_CHUNKS, D), lambda i: (_cond_of_tile(i), 0, 0)),
            pl.BlockSpec((1, D), lambda i: (0, 0)),
            pl.BlockSpec((N_EXPERTS, D), lambda i: (0, 0)),
        ],
        out_specs=(pl.BlockSpec((TM, D), lambda i: (i, 0)), pl.BlockSpec((N_EXPERTS, TM), lambda i: (0, i))),
        compiler_params=_cparams(("arbitrary",)),
        name="router",
    )(x, mod_l, nw, wrT)


def _excl_cumsum_lanes(m):
    blk = 256
    t = m.shape[1]
    a = lax.broadcasted_iota(jnp.int32, (blk, blk), 0)
    b = lax.broadcasted_iota(jnp.int32, (blk, blk), 1)
    strict = jnp.where(a < b, 1.0, 0.0).astype(BF16)
    carry = jnp.zeros((m.shape[0], 1), F32)
    outs = []
    for k in range(t // blk):
        mk = m[:, k * blk:(k + 1) * blk]
        outs.append(jnp.dot(mk.astype(BF16), strict, preferred_element_type=F32) + carry)
        carry = carry + jnp.sum(mk, axis=1, keepdims=True)
    return outs[0] if len(outs) == 1 else jnp.concatenate(outs, axis=1)


def _select_request(aff, cap, base):
    bits = pltpu.bitcast(aff, jnp.int32)
    thr = jnp.zeros((N_EXPERTS, 1), jnp.int32)
    for k in range(30, -1, -1):
        trial = thr | (1 << k)
        cnt = jnp.sum(jnp.where(bits >= trial, 1.0, 0.0), axis=1, keepdims=True)
        thr = jnp.where(cnt >= cap, trial, thr)
    gt = bits > thr
    eq = jnp.where(bits == thr, 1.0, 0.0)
    need = cap - jnp.sum(jnp.where(gt, 1.0, 0.0), axis=1, keepdims=True)
    sel = gt | ((eq > 0.0) & (_excl_cumsum_lanes(eq) < need))
    pos = _excl_cumsum_lanes(jnp.where(sel, 1.0, 0.0))
    return jnp.where(sel, pos + base, -1.0), jnp.where(sel, aff, 0.0)


def _select_kernel(affT_ref, slotT_ref, slot_tok_ref, gate_tok_ref, lo_ref):
    s = pl.program_id(0)

    def emit(slot, gate):
        slotT_ref[...] = slot.astype(jnp.int32)
        pad = jnp.zeros((128 - N_EXPERTS, SR_TOKENS), F32)
        slot_tok_ref[...] = jnp.concatenate([slot, pad - 1.0], axis=0).T.astype(jnp.int32)
        gate_tok_ref[...] = jnp.concatenate([gate, pad], axis=0).T
        t = lax.broadcasted_iota(jnp.int32, (SR_TOKENS, 128), 0)
        jcol = lax.broadcasted_iota(jnp.int32, (SR_TOKENS, 128), 1)
        before = jnp.where(t < jcol * BLK, 1.0, 0.0).astype(BF16)
        chosen = jnp.where(slot >= 0.0, 1.0, 0.0).astype(BF16)
        lo_ref[...] = jnp.dot(chosen, before, preferred_element_type=F32).astype(jnp.int32)

    @pl.when(s < N_SR_CTX)
    def _():
        outs = [_select_request(affT_ref[:, r * L_CTX:(r + 1) * L_CTX], CAP_CTX, float(r * CAP_CTX))
                for r in range(CTX_PER_SR)]
        emit(jnp.concatenate([o[0] for o in outs], axis=1), jnp.concatenate([o[1] for o in outs], axis=1))

    @pl.when(s >= N_SR_CTX)
    def _():
        emit(*_select_request(affT_ref[...], CAP_LAT, 0.0))


def _select(affT):
    return pl.pallas_call(
        _select_kernel,
        out_shape=(
            jax.ShapeDtypeStruct((N_EXPERTS, R), jnp.int32),
            jax.ShapeDtypeStruct((R, 128), jnp.int32),
            jax.ShapeDtypeStruct((R, 128), F32),
            jax.ShapeDtypeStruct((N_SR, N_EXPERTS, 128), jnp.int32),
        ),
        grid=(N_SR,),
        in_specs=[pl.BlockSpec((N_EXPERTS, SR_TOKENS), lambda s: (0, s))],
        out_specs=(
            pl.BlockSpec((N_EXPERTS, SR_TOKENS), lambda s: (0, s)),
            pl.BlockSpec((SR_TOKENS, 128), lambda s: (s, 0)),
            pl.BlockSpec((SR_TOKENS, 128), lambda s: (s, 0)),
            pl.BlockSpec((None, N_EXPERTS, 128), lambda s: (s, 0, 0)),
        ),
        compiler_params=_cparams(("arbitrary",)),
        name="select",
    )(affT)


def _lo_at(lo_ref, sr, e, j):
    return lo_ref[(sr * N_EXPERTS + e) * LO_COLS + j]


def _gather_kernel(lo_ref, hn_ref, slotT_ref, xe_ref, acc_ref):
    sr = pl.program_id(0)
    e0 = pl.program_id(1) * E_HALF
    starts = {}
    fits = None
    for e in range(E_HALF):
        for j in range(N_BLK):
            start = (_lo_at(lo_ref, sr, e0 + e, j) // BF16_ROWS) * BF16_ROWS
            ok = _lo_at(lo_ref, sr, e0 + e, j + 1) - start <= GW_ROWS
            fits = ok if fits is None else jnp.logical_and(fits, ok)
            starts[e, j] = start

    @pl.when(fits)
    def _():
        acc_ref[...] = jnp.zeros(acc_ref.shape, BF16)
        w_iota = lax.broadcasted_iota(jnp.int32, (GW_ROWS, BLK), 0)
        for j in range(N_BLK):
            rows = [jnp.where(w_iota == slotT_ref[e:e + 1, j * BLK:(j + 1) * BLK] - starts[e, j], 1.0, 0.0).astype(BF16)
                    for e in range(E_HALF)]
            part = jnp.dot(jnp.concatenate(rows, axis=0), hn_ref[j * BLK:(j + 1) * BLK, :],
                           preferred_element_type=F32).astype(BF16)
            for e in range(E_HALF):
                win = pl.ds(pl.multiple_of(starts[e, j], BF16_ROWS), GW_ROWS)
                acc_ref[e, win, :] = acc_ref[e, win, :] + part[e * GW_ROWS:(e + 1) * GW_ROWS, :]
        xe_ref[...] = acc_ref[:, :SR_SLOTS, :]

    @pl.when(jnp.logical_not(fits))
    def _():
        s_iota = lax.broadcasted_iota(jnp.int32, (SR_SLOTS, SR_TOKENS), 0)
        for e in range(E_HALF):
            onehot = jnp.where(s_iota == slotT_ref[e:e + 1, :], 1.0, 0.0).astype(BF16)
            xe_ref[e] = jnp.dot(onehot, hn_ref[...], preferred_element_type=F32).astype(BF16)


def _gather(lo_flat, hn, slotT):
    return pl.pallas_call(
        _gather_kernel,
        out_shape=jax.ShapeDtypeStruct((N_EXPERTS, N_SR * SR_SLOTS, D), BF16),
        grid_spec=pltpu.PrefetchScalarGridSpec(
            num_scalar_prefetch=1,
            grid=(N_SR, N_EXPERTS // E_HALF),
            in_specs=[
                pl.BlockSpec((SR_TOKENS, D), lambda s, h, lo: (s, 0)),
                pl.BlockSpec((E_HALF, SR_TOKENS), lambda s, h, lo: (h, s)),
            ],
            out_specs=pl.BlockSpec((E_HALF, SR_SLOTS, D), lambda s, h, lo: (h, s, 0)),
            scratch_shapes=[pltpu.VMEM((E_HALF, SR_SLOTS + GW_ROWS, D), BF16)],
        ),
        compiler_params=_cparams(("arbitrary", "arbitrary")),
        name="gather",
    )(lo_flat, hn, slotT)


def _ffn_kernel(xe_ref, wg_ref, wu_ref, wd_ref, ye_ref, acc_ref):
    f = pl.program_id(1)
    xe = xe_ref[...]
    hg = jnp.dot(xe, wg_ref[...].astype(BF16), preferred_element_type=F32)
    hu = jnp.dot(xe, wu_ref[...].astype(BF16), preferred_element_type=F32)
    hid = (_silu(hg) * hu).astype(BF16)
    part = jnp.dot(hid, wd_ref[...].astype(BF16), preferred_element_type=F32)

    @pl.when(f == 0)
    def _():
        acc_ref[...] = part

    @pl.when(f > 0)
    def _():
        acc_ref[...] = acc_ref[...] + part

    @pl.when(f == pl.num_programs(1) - 1)
    def _():
        ye_ref[...] = acc_ref[...].astype(BF16)


def _ffn(xe, wg, wu, wd, layer):
    tf = 512
    m = xe.shape[1]
    return pl.pallas_call(
        _ffn_kernel,
        out_shape=jax.ShapeDtypeStruct((N_EXPERTS, m, D), BF16),
        grid=(N_EXPERTS, D // tf),
        in_specs=[
            pl.BlockSpec((None, m, D), lambda e, f: (e, 0, 0)),
            pl.BlockSpec((None, None, D, tf), lambda e, f: (layer, e, 0, f)),
            pl.BlockSpec((None, None, D, tf), lambda e, f: (layer, e, 0, f)),
            pl.BlockSpec((None, None, tf, D), lambda e, f: (layer, e, f, 0)),
        ],
        out_specs=pl.BlockSpec((None, m, D), lambda e, f: (e, 0, 0)),
        scratch_shapes=[pltpu.VMEM((m, D), F32)],
        compiler_params=_cparams(("arbitrary", "arbitrary")),
        name="expert_ffn",
    )(xe, wg, wu, wd)


def _combine_kernel(lo_ref, x_ref, ye_ref, slot_ref, gate_ref, mod_ref, fnw_ref, o_ref, yw_ref, acc_ref, *,
                    final_norm):
    sr = pl.program_id(0)
    j = pl.program_id(1)
    starts = []
    fits = None
    for e in range(N_EXPERTS):
        start = jnp.minimum((_lo_at(lo_ref, sr, e, j) // BF16_ROWS) * BF16_ROWS, SR_SLOTS - CW_ROWS)
        ok = _lo_at(lo_ref, sr, e, j + 1) - start <= CW_ROWS
        fits = ok if fits is None else jnp.logical_and(fits, ok)
        starts.append(start)
    slot = slot_ref[...]
    gate = gate_ref[...]

    @pl.when(fits)
    def _():
        for e in range(N_EXPERTS):
            yw_ref[e * CW_ROWS:(e + 1) * CW_ROWS, :] = ye_ref[e, pl.ds(pl.multiple_of(starts[e], BF16_ROWS), CW_ROWS), :]
        n_k = N_EXPERTS * CW_ROWS
        k_exp = lax.broadcasted_iota(jnp.int32, (1, n_k), 1) // CW_ROWS
        k_row = (lax.broadcasted_iota(jnp.int32, (1, n_k), 1) % CW_ROWS).astype(F32)
        start_k = jnp.zeros((1, n_k), F32)
        for e in range(N_EXPERTS):
            start_k = jnp.where(k_exp == e, starts[e].astype(F32), start_k)
        spread = jnp.where(lax.broadcasted_iota(jnp.int32, (128, n_k), 0) == k_exp, 1.0, 0.0).astype(BF16)
        slot_k = jnp.dot(slot.astype(F32).astype(BF16), spread, preferred_element_type=F32)
        g_hi = gate.astype(BF16)
        g_lo = (gate - g_hi.astype(F32)).astype(BF16)
        hit = slot_k - start_k == k_row
        sw_hi = jnp.where(hit, jnp.dot(g_hi, spread, preferred_element_type=F32), 0.0).astype(BF16)
        sw_lo = jnp.where(hit, jnp.dot(g_lo, spread, preferred_element_type=F32), 0.0).astype(BF16)
        yw = yw_ref[...]
        acc_ref[...] = (jnp.dot(sw_hi, yw, preferred_element_type=F32)
                        + jnp.dot(sw_lo, yw, preferred_element_type=F32))

    @pl.when(jnp.logical_not(fits))
    def _():
        s_iota = lax.broadcasted_iota(jnp.int32, (BLK, SR_SLOTS), 1)
        acc = jnp.zeros((BLK, D), F32)
        for e in range(N_EXPERTS):
            onehot = jnp.where(slot[:, e:e + 1] == s_iota, 1.0, 0.0).astype(BF16)
            acc = acc + gate[:, e:e + 1] * jnp.dot(onehot, ye_ref[e], preferred_element_type=F32)
        acc_ref[...] = acc

    out = x_ref[...] + mod_ref[5:6, :] * acc_ref[...]
    if final_norm:
        out = out * lax.rsqrt(jnp.mean(out * out, axis=-1, keepdims=True) + EPS) * fnw_ref[...]
    o_ref[...] = out


def _combine(lo_flat, x, ye, slot_tok, gate_tok, mod_l, fnw, *, final_norm):
    blk_of = lambda s, j: s * N_BLK + j
    return pl.pallas_call(
        functools.partial(_combine_kernel, final_norm=final_norm),
        out_shape=jax.ShapeDtypeStruct((R, D), F32),
        grid_spec=pltpu.PrefetchScalarGridSpec(
            num_scalar_prefetch=1,
            grid=(N_SR, N_BLK),
            in_specs=[
                pl.BlockSpec((BLK, D), lambda s, j, lo: (blk_of(s, j), 0)),
                pl.BlockSpec((N_EXPERTS, SR_SLOTS, D), lambda s, j, lo: (0, s, 0)),
                pl.BlockSpec((BLK, 128), lambda s, j, lo: (blk_of(s, j), 0)),
                pl.BlockSpec((BLK, 128), lambda s, j, lo: (blk_of(s, j), 0)),
                pl.BlockSpec((None, MOD_CHUNKS, D),
                             lambda s, j, lo: (_cond_of_tile(blk_of(s, j) * BLK // TM), 0, 0)),
                pl.BlockSpec((1, D), lambda s, j, lo: (0, 0)),
            ],
            out_specs=pl.BlockSpec((BLK, D), lambda s, j, lo: (blk_of(s, j), 0)),
            scratch_shapes=[pltpu.VMEM((N_EXPERTS * CW_ROWS, D), BF16), pltpu.VMEM((BLK, D), F32)],
        ),
        compiler_params=_cparams(("arbitrary", "arbitrary")),
        name="combine",
    )(lo_flat, x, ye, slot_tok, gate_tok, mod_l, fnw)


def _moe(x, mod_l, nw, wrT, wg, wu, wd, fnw, layer, *, final_norm):
    hn, affT = _router(x, mod_l, nw, wrT)
    slotT, slot_tok, gate_tok, lo = _select(affT)
    lo_flat = lo[:, :, :LO_COLS].reshape(-1)
    xe = _gather(lo_flat, hn, slotT)
    ye = _ffn(xe, wg, wu, wd, layer)
    return _combine(lo_flat, x, ye, slot_tok, gate_tok, mod_l, fnw, final_norm=final_norm)


def _group_major(p):
    return p.reshape(2, GROUPS, HPG).transpose(1, 0, 2).reshape(GROUPS, 2 * HPG)


def kernel(x_prompt, x_sample, state_ssm, c, c_ctx, norm1_w, norm2_w, w_mod, b_mod, conv_in_w, conv_w, conv_out_w, ssd_in_w, ssd_conv_w, ssd_conv_b, ssd_dt_bias, ssd_a_log, ssd_d, ssd_norm_w, ssd_out_w, router_w, exp_w_gate, exp_w_up, exp_w_down, final_norm_w):
    x = jnp.concatenate([x_prompt.reshape(R_CTX, D), x_sample.reshape(R_LAT, D)], axis=0)
    cond = jnp.concatenate([c_ctx[None, :], c, jnp.zeros((8 - N_COND, D), F32)], axis=0)
    mods = _modulation(cond.T, w_mod, b_mod)
    mods = mods[:, :N_COND].reshape(DEPTH, N_COND, MOD_CHUNKS, D)
    fnw = final_norm_w.reshape(1, D)

    states = None
    for layer in range(DEPTH):
        mod_l = mods[layer]
        j = layer // 2
        nw1 = norm1_w[layer].reshape(1, D)
        if layer % 2 == 0:
            x = _conv_mixer(x, mod_l, nw1, conv_in_w[j].astype(BF16), conv_w[j], conv_out_w[j].astype(BF16))
        else:
            w_in = ssd_in_w[j]
            w_dt = w_in[:, D_INNER + CONV_DIM:].reshape(D, 2, GROUPS, HPG).transpose(0, 2, 1, 3)
            w_dt_rep = jnp.broadcast_to(w_dt.reshape(D, GROUPS, 1, 2 * HPG), (D, GROUPS, N_SPLIT, 2 * HPG))
            w_in_b = jnp.concatenate([w_in[:, :D_INNER + CONV_DIM], w_dt_rep.reshape(D, GROUPS * DT_LANES)],
                                     axis=1).astype(BF16)
            dtb = _group_major(ssd_dt_bias[j])
            dtb_rep = jnp.tile(dtb, (1, N_SPLIT)).reshape(1, GROUPS * DT_LANES)
            z, xs, bm, cm, dt4, dtT4 = _ssd_in(
                x, mod_l, nw1, w_in_b, w_dt.reshape(D, 2 * HEADS).T.astype(BF16), ssd_conv_w[j],
                ssd_conv_b[j].reshape(1, CONV_DIM), dtb_rep, dtb.reshape(2 * HEADS, 1))
            alog4 = _group_major(ssd_a_log[j])
            dsum = (ssd_d[j][0] + ssd_d[j][1]).reshape(GROUPS, HPG)
            dsk4 = jnp.tile(jnp.concatenate([jnp.zeros_like(dsum), dsum], axis=1), (1, N_SPLIT))
            scan_args = (xs.reshape(R // CHUNK, CHUNK, D_INNER), bm.reshape(R // CHUNK, CHUNK, GROUPS * STATE),
                         cm.reshape(R // CHUNK, CHUNK, GROUPS * STATE), dt4, dtT4,
                         jnp.tile(alog4, (1, N_SPLIT)).reshape(GROUPS, 1, DT_LANES),
                         alog4.reshape(GROUPS, 2 * HPG, 1), dsk4.reshape(GROUPS, 1, DT_LANES))
            y_ctx, states = _ssd_scan(*scan_args, None, states, n_seq=N_CTX, seq_len=L_CTX, row_block0=0,
                                      state_layer=j)
            h0 = state_ssm[:, j].reshape(N_LAT, 2, HEADS * HEADDIM, STATE)
            y_lat = _ssd_scan(*scan_args, h0, None, n_seq=N_LAT, seq_len=L_LAT, row_block0=R_CTX // L_LAT,
                              state_layer=None)
            x = _ssd_out(x, y_ctx.reshape(R_CTX, D_INNER), y_lat.reshape(R_LAT, D_INNER), z, mod_l,
                         ssd_norm_w[j].reshape(1, D_INNER), ssd_out_w[j].astype(BF16))
        x = _moe(x, mod_l, norm2_w[layer].reshape(1, D), router_w[layer].T, exp_w_gate, exp_w_up, exp_w_down, fnw,
                 layer, final_norm=(layer == DEPTH - 1))

    y_prompt = x[:R_CTX].reshape(N_CTX, L_CTX, D)
    y_sample = x[R_CTX:].reshape(N_LAT, L_LAT, D)
    return y_prompt, y_sample, states.reshape(N_CTX, DEPTH // 2, 2, HEADS, HEADDIM, STATE)
```

```python
import functools

import jax
import jax.numpy as jnp
import numpy as np
from jax import lax
from jax.experimental import pallas as pl
from jax.experimental.pallas import tpu as pltpu

F32 = jnp.float32
BF16 = jnp.bfloat16
HIGHEST = lax.Precision.HIGHEST
LOG2E = 1.4426950408889634

D = 1024
DEPTH = 4
N_CTX, L_CTX = 16, 256
N_LAT, L_LAT = 2, 2048
GRID_W = 64
R_CTX = N_CTX * L_CTX
R_LAT = N_LAT * L_LAT
R = R_CTX + R_LAT
N_COND = 1 + N_LAT
MOD_CHUNKS = 6
D_INNER = 2048
HEADDIM = 64
HEADS = 32
GROUPS = 4
HPG = HEADS // GROUPS
GW = HPG * HEADDIM
STATE = 128
CHUNK = 128
CONV_DIM = D_INNER + 2 * GROUPS * STATE
N_SPLIT = 3
DT_LANES = N_SPLIT * 2 * HPG
N_EXPERTS = 16
EPS = 1e-6

TM = 512
TM_OUT = 512
CTX_TILES = R_CTX // TM
LAT_TILES_PER_REQ = L_LAT // TM
SR_TOKENS = 2048
N_SR = R // SR_TOKENS
N_SR_CTX = R_CTX // SR_TOKENS
SR_SLOTS = 2 * SR_TOKENS // N_EXPERTS
CTX_PER_SR = SR_TOKENS // L_CTX
CAP_CTX = 2 * L_CTX // N_EXPERTS
CAP_LAT = 2 * L_LAT // N_EXPERTS
BLK = 256
N_BLK = SR_TOKENS // BLK
LO_COLS = 16
BF16_ROWS = 16
GW_ROWS = 80
CW_ROWS = 64
E_HALF = N_EXPERTS // 2
VMEM_LIMIT = 56 * 1024 * 1024


def _cparams(sem):
    return pltpu.CompilerParams(dimension_semantics=sem, vmem_limit_bytes=VMEM_LIMIT)


def _cond_of_tile(i):
    return jnp.where(i < CTX_TILES, 0, 1 + (i - CTX_TILES) // LAT_TILES_PER_REQ)


def _silu(v):
    return v * (1.0 / (1.0 + jnp.exp(-v)))


def _softplus(v):
    return jnp.maximum(v, 0.0) + jnp.log1p(jnp.exp(-jnp.abs(v)))


def _norm_mod(x, nw, scale, shift):
    y = x * lax.rsqrt(jnp.mean(x * x, axis=-1, keepdims=True) + EPS)
    return y * nw * (1.0 + scale) + shift


def _conv3_rows(p, w_ref, tile_idx):
    n = p.shape[0]
    period = jnp.where(tile_idx < CTX_TILES, L_CTX, GRID_W)
    r = lax.broadcasted_iota(jnp.int32, (n, 1), 0) & (period - 1)
    prev = jnp.where(r == 0, 0.0, pltpu.roll(p, 1, axis=0))
    nxt = jnp.where(r == period - 1, 0.0, pltpu.roll(p, n - 1, axis=0))
    return prev * w_ref[0:1, :] + p * w_ref[1:2, :] + nxt * w_ref[2:3, :]


def _split_pieces(v):
    pieces = []
    r = v
    for _ in range(N_SPLIT):
        p = r.astype(BF16)
        pieces.append(p)
        r = r - p.astype(F32)
    return pieces


def _modulation_kernel(condT_ref, w_ref, b_ref, o_ref):
    s = _silu(condT_ref[...])
    w = w_ref[...]
    rows = [jnp.sum(w * s[:, r:r + 1], axis=0, keepdims=True) + b_ref[...] for r in range(N_COND)]
    rows.append(jnp.zeros((8 - N_COND, w.shape[1]), F32))
    o_ref[...] = jnp.concatenate(rows, axis=0)


def _modulation(condT, w_mod, b_mod):
    tn = 1536
    n = MOD_CHUNKS * D
    return pl.pallas_call(
        _modulation_kernel,
        out_shape=jax.ShapeDtypeStruct((DEPTH, 8, n), F32),
        grid=(DEPTH, n // tn),
        in_specs=[
            pl.BlockSpec((D, 8), lambda l, j: (0, 0)),
            pl.BlockSpec((None, D, tn), lambda l, j: (l, 0, j)),
            pl.BlockSpec((None, 1, tn), lambda l, j: (l, 0, j)),
        ],
        out_specs=pl.BlockSpec((None, 8, tn), lambda l, j: (l, 0, j)),
        compiler_params=_cparams(("arbitrary", "arbitrary")),
        name="modulation",
    )(condT, w_mod, b_mod.reshape(DEPTH, 1, n))


def _conv_mixer_kernel(x_ref, mod_ref, nw_ref, win_ref, cw_ref, wout_ref, o_ref):
    i = pl.program_id(0)
    x = x_ref[...]
    hn = _norm_mod(x, nw_ref[...], mod_ref[1:2, :], mod_ref[0:1, :]).astype(BF16)
    gb = jnp.dot(hn, win_ref[:, 0:D], preferred_element_type=F32)
    gc = jnp.dot(hn, win_ref[:, D:2 * D], preferred_element_type=F32)
    v = jnp.dot(hn, win_ref[:, 2 * D:3 * D], preferred_element_type=F32)
    q = (gb * _conv3_rows(gc * v, cw_ref, i)).astype(BF16)
    mix = jnp.dot(q, wout_ref[...], preferred_element_type=F32)
    o_ref[...] = x + mod_ref[2:3, :] * mix


def _conv_mixer(x, mod_l, nw, w_in, cw, w_out):
    return pl.pallas_call(
        _conv_mixer_kernel,
        out_shape=jax.ShapeDtypeStruct((R, D), F32),
        grid=(R // TM,),
        in_specs=[
            pl.BlockSpec((TM, D), lambda i: (i, 0)),
            pl.BlockSpec((None, MOD_CHUNKS, D), lambda i: (_cond_of_tile(i), 0, 0)),
            pl.BlockSpec((1, D), lambda i: (0, 0)),
            pl.BlockSpec((D, 3 * D), lambda i: (0, 0)),
            pl.BlockSpec((3, D), lambda i: (0, 0)),
            pl.BlockSpec((D, D), lambda i: (0, 0)),
        ],
        out_specs=pl.BlockSpec((TM, D), lambda i: (i, 0)),
        compiler_params=_cparams(("arbitrary",)),
        name="conv_mixer",
    )(x, mod_l, nw, w_in, cw, w_out)


def _ssd_in_kernel(x_ref, mod_ref, nw_ref, win_ref, wdtT_ref, cw_ref, cb_ref, dtb_ref, dtbT_ref,
                   z_ref, xs_ref, bm_ref, cm_ref, dt_ref, dtT_ref):
    i = pl.program_id(0)
    hn = _norm_mod(x_ref[...], nw_ref[...], mod_ref[1:2, :], mod_ref[0:1, :]).astype(BF16)
    for k in range(D_INNER // D):
        z_ref[:, k * D:(k + 1) * D] = jnp.dot(hn, win_ref[:, k * D:(k + 1) * D], preferred_element_type=F32)
    for k in range(CONV_DIM // D):
        lo = D_INNER + k * D
        u = jnp.dot(hn, win_ref[:, lo:lo + D], preferred_element_type=F32)
        u = _silu(_conv3_rows(u, cw_ref.at[:, k * D:(k + 1) * D], i) + cb_ref[:, k * D:(k + 1) * D])
        if k < D_INNER // D:
            xs_ref[:, k * D:(k + 1) * D] = u
        else:
            bm_ref[...] = u[:, :GROUPS * STATE]
            cm_ref[...] = u[:, GROUPS * STATE:]
    lo = D_INNER + CONV_DIM
    dt = _softplus(jnp.dot(hn, win_ref[:, lo:lo + GROUPS * DT_LANES], preferred_element_type=F32) + dtb_ref[...])
    dtT = _softplus(
        lax.dot_general(wdtT_ref[...], hn, (((1,), (1,)), ((), ())), preferred_element_type=F32) + dtbT_ref[...])
    for g in range(GROUPS):
        dt_ref[g] = dt[:, g * DT_LANES:(g + 1) * DT_LANES].reshape(TM // CHUNK, CHUNK, DT_LANES)
        for k in range(TM // CHUNK):
            dtT_ref[g, k] = dtT[g * 2 * HPG:(g + 1) * 2 * HPG, k * CHUNK:(k + 1) * CHUNK]


def _ssd_in(x, mod_l, nw, w_in, w_dtT, cw, cb, dtb, dtbT):
    n_in = w_in.shape[1]
    nck = TM // CHUNK
    return pl.pallas_call(
        _ssd_in_kernel,
        out_shape=(
            jax.ShapeDtypeStruct((R, D_INNER), F32),
            jax.ShapeDtypeStruct((R, D_INNER), F32),
            jax.ShapeDtypeStruct((R, GROUPS * STATE), F32),
            jax.ShapeDtypeStruct((R, GROUPS * STATE), F32),
            jax.ShapeDtypeStruct((GROUPS, R // CHUNK, CHUNK, DT_LANES), F32),
            jax.ShapeDtypeStruct((GROUPS, R // CHUNK, 2 * HPG, CHUNK), F32),
        ),
        grid=(R // TM,),
        in_specs=[
            pl.BlockSpec((TM, D), lambda i: (i, 0)),
            pl.BlockSpec((None, MOD_CHUNKS, D), lambda i: (_cond_of_tile(i), 0, 0)),
            pl.BlockSpec((1, D), lambda i: (0, 0)),
            pl.BlockSpec((D, n_in), lambda i: (0, 0)),
            pl.BlockSpec((2 * HEADS, D), lambda i: (0, 0)),
            pl.BlockSpec((3, CONV_DIM), lambda i: (0, 0)),
            pl.BlockSpec((1, CONV_DIM), lambda i: (0, 0)),
            pl.BlockSpec((1, GROUPS * DT_LANES), lambda i: (0, 0)),
            pl.BlockSpec((2 * HEADS, 1), lambda i: (0, 0)),
        ],
        out_specs=(
            pl.BlockSpec((TM, D_INNER), lambda i: (i, 0)),
            pl.BlockSpec((TM, D_INNER), lambda i: (i, 0)),
            pl.BlockSpec((TM, GROUPS * STATE), lambda i: (i, 0)),
            pl.BlockSpec((TM, GROUPS * STATE), lambda i: (i, 0)),
            pl.BlockSpec((GROUPS, nck, CHUNK, DT_LANES), lambda i: (0, i, 0, 0)),
            pl.BlockSpec((GROUPS, nck, 2 * HPG, CHUNK), lambda i: (0, i, 0, 0)),
        ),
        compiler_params=_cparams(("arbitrary",)),
        name="ssd_in",
    )(x, mod_l, nw, w_in, w_dtT, cw, cb, dtb, dtbT)


def _ssd_scan_kernel(*refs, n_chunks, has_h0, n_kept, state_slots):
    emit_state = state_slots is not None
    xs_ref, bm_ref, cm_ref, dt_ref, dtT_ref, alog_ref, alogT_ref, dsk_ref = refs[:8]
    tri_k_ref, tri_r_ref, expf_ref, expb_ref, colsel_ref = refs[8:13]
    k = 13
    h0_ref = None
    if has_h0:
        h0_ref = refs[k]
        k += 1
    k += n_kept
    y_ref = refs[k]
    k += 1
    st_out_ref = None
    if emit_state:
        st_out_ref = refs[k]
        k += 1
    st_ref = refs[k]

    a_row = -jnp.exp(alog_ref[...]) * LOG2E
    a_col = -jnp.exp(alogT_ref[...]) * LOG2E
    row_is_fwd = lax.broadcasted_iota(jnp.int32, (2 * HPG, 1), 0) < HPG
    qi = lax.broadcasted_iota(jnp.int32, (CHUNK, CHUNK), 0)
    si = lax.broadcasted_iota(jnp.int32, (CHUNK, CHUNK), 1)
    lower = si <= qi
    lane = lax.broadcasted_iota(jnp.int32, (1, DT_LANES), 1)
    piece_of_lane = lane // (2 * HPG)
    lane_is_fwd = lane % (2 * HPG) < HPG
    lane_c = lax.broadcasted_iota(jnp.int32, (1, CHUNK), 1)
    first_half = lane_c < HEADDIM

    def lane_pieces(v):
        pieces = _split_pieces(v)
        out = pieces[N_SPLIT - 1]
        for r in range(N_SPLIT - 2, -1, -1):
            out = jnp.where(piece_of_lane == r, pieces[r], out)
        return out

    def expand_many(vs, e):
        out = jnp.dot(jnp.concatenate([lane_pieces(v) for v in vs], axis=0), e, preferred_element_type=F32)
        res, r0 = [], 0
        for v in vs:
            res.append(out[r0:r0 + v.shape[0], :])
            r0 += v.shape[0]
        return res

    for d in range(2):
        if has_h0:
            st_ref[d] = h0_ref[d].T
        else:
            st_ref[d] = jnp.zeros((STATE, GW), F32)

    y_ref[...] = jnp.zeros(y_ref.shape, F32)

    def body(k_, carry):
        cf = k_
        x = xs_ref[cf]
        b = bm_ref[cf]
        cm = cm_ref[cf]
        dt = dt_ref[cf]
        dtT = dtT_ref[cf]
        cr = n_chunks - 1 - k_
        xr = xs_ref[cr]
        br = bm_ref[cr]
        dtr = dt_ref[cr]

        dta_rows = jnp.concatenate(_split_pieces(dt * a_row), axis=0)
        dtaT_lanes = jnp.concatenate(_split_pieces(dtT * a_col), axis=1)
        acs2 = jnp.dot(tri_k_ref[...], dta_rows, preferred_element_type=F32)
        acs_lo, acs_up = acs2[:CHUNK, :], acs2[CHUNK:, :]
        acs_t2 = jnp.dot(dtaT_lanes, tri_r_ref[...], preferred_element_type=F32)
        acs_up_r = jnp.dot(tri_k_ref[CHUNK:, :], jnp.concatenate(_split_pieces(dtr * a_row), axis=0),
                           preferred_element_type=F32)
        r_t = jnp.log(dtT) * LOG2E - jnp.where(row_is_fwd, acs_t2[:, :CHUNK], acs_t2[:, CHUNK:])
        col_b = jnp.dot(lane_pieces(jnp.where(lane_is_fwd, acs_lo, acs_up)), colsel_ref[...],
                        preferred_element_type=F32)

        a_last = acs_lo[CHUNK - 1:CHUNK, :]
        a_tot = acs_up_r[0:1, :]
        cb_diag = jnp.sum(cm * b, axis=1, keepdims=True)
        e_out_f, e_in_f = expand_many([jnp.exp2(acs_lo), jnp.exp2(a_last - acs_lo) * dt], expf_ref[...])
        e_self, e_out_b, e_in_b = expand_many(
            [dsk_ref[...] + cb_diag * dt, jnp.exp2(acs_up_r), jnp.exp2(a_tot - acs_up_r) * dtr], expb_ref[...])
        e_keep_f = e_out_f[CHUNK - 1:CHUNK, :]
        e_keep_b = e_out_b[0:1, :]

        cmb = cm.astype(BF16)
        cb = lax.dot_general(cmb, b.astype(BF16), (((1,), (1,)), ((), ())), preferred_element_type=F32)
        parts = []
        for hp in range(HPG // 2):
            ws = []
            for h in (2 * hp, 2 * hp + 1):
                arg = jnp.where(lower, col_b[:, h * CHUNK:(h + 1) * CHUNK] + r_t[h:h + 1, :],
                                col_b[:, (HPG + h) * CHUNK:(HPG + h + 1) * CHUNK] + r_t[HPG + h:HPG + h + 1, :])
                ws.append((cb * jnp.exp2(arg)).astype(BF16))
            xp = x[:, hp * CHUNK:(hp + 1) * CHUNK]
            x2 = jnp.concatenate([jnp.where(first_half, xp, 0.0), jnp.where(first_half, 0.0, xp)], axis=0)
            parts.append(jnp.dot(jnp.concatenate(ws, axis=1), x2.astype(BF16), preferred_element_type=F32))
        st_f = st_ref[0]
        y = (jnp.concatenate(parts, axis=1) + e_self * x
             + jnp.dot(cmb, st_f.astype(BF16), preferred_element_type=F32) * e_out_f)
        y_ref[cf] = y_ref[cf] + y
        st_ref[0] = st_f * e_keep_f + jnp.dot(
            b.T.astype(BF16), (x * e_in_f).astype(BF16), preferred_element_type=F32)

        st_b = st_ref[1]
        y_ref[cr] = y_ref[cr] + jnp.dot(cm_ref[cr].astype(BF16), st_b.astype(BF16),
                                        preferred_element_type=F32) * e_out_b
        st_ref[1] = st_b * e_keep_b + jnp.dot(
            br.T.astype(BF16), (xr * e_in_b).astype(BF16), preferred_element_type=F32)
        return carry

    lax.fori_loop(0, n_chunks, body, 0)
    if emit_state:
        for slot, own in enumerate(state_slots):
            for d in range(2):
                st_out_ref[slot, d] = st_ref[d].T if own else jnp.zeros((GW, STATE), F32)


def _scan_constants():
    q = np.arange(CHUNK)
    tri_lo = (q[None, :] <= q[:, None]).astype(np.float32)
    tri_up = tri_lo.T
    tri_k = np.concatenate([np.tile(tri_lo, (1, N_SPLIT)), np.tile(tri_up, (1, N_SPLIT))], axis=0)
    tri_r = np.concatenate([np.tile(tri_up, (N_SPLIT, 1)), np.tile(tri_lo, (N_SPLIT, 1))], axis=1)
    head = np.arange(DT_LANES) % (2 * HPG)
    chan_head = np.arange(GW) // HEADDIM
    exp_f = (head[:, None] == chan_head[None, :]).astype(np.float32)
    exp_b = (head[:, None] == chan_head[None, :] + HPG).astype(np.float32)
    col_sel = (head[:, None] == (np.arange(2 * HPG * CHUNK) // CHUNK)[None, :]).astype(np.float32)
    return [jnp.asarray(m, BF16) for m in (tri_k, tri_r, exp_f, exp_b, col_sel)]


def _ssd_scan(xs3, bm3, cm3, dt4, dtT4, alog4, alogT4, dskx, h0, st_prev, *, n_seq, seq_len, row_block0, state_layer):
    nck = seq_len // CHUNK
    has_h0 = h0 is not None
    emit_state = state_layer is not None
    keep_state = st_prev is not None
    rb = lambda s: s + row_block0
    in_specs = [
        pl.BlockSpec((nck, CHUNK, GW), lambda s, g: (rb(s), 0, g)),
        pl.BlockSpec((nck, CHUNK, STATE), lambda s, g: (rb(s), 0, g)),
        pl.BlockSpec((nck, CHUNK, STATE), lambda s, g: (rb(s), 0, g)),
        pl.BlockSpec((None, nck, CHUNK, DT_LANES), lambda s, g: (g, rb(s), 0, 0)),
        pl.BlockSpec((None, nck, 2 * HPG, CHUNK), lambda s, g: (g, rb(s), 0, 0)),
        pl.BlockSpec((None, 1, DT_LANES), lambda s, g: (g, 0, 0)),
        pl.BlockSpec((None, 2 * HPG, 1), lambda s, g: (g, 0, 0)),
        pl.BlockSpec((None, 1, DT_LANES), lambda s, g: (g, 0, 0)),
    ]
    args = [xs3, bm3, cm3, dt4, dtT4, alog4, alogT4, dskx]
    for const in _scan_constants():
        in_specs.append(pl.BlockSpec(const.shape, lambda s, g: (0, 0)))
        args.append(const)
    if has_h0:
        in_specs.append(pl.BlockSpec((None, 2, GW, STATE), lambda s, g: (s, 0, g, 0)))
        args.append(h0)
    aliases = {}
    if keep_state:
        in_specs.append(pl.BlockSpec(memory_space=pl.ANY))
        aliases[len(args)] = 1
        args.append(st_prev)
    y_shape = jax.ShapeDtypeStruct((n_seq * nck, CHUNK, D_INNER), F32)
    y_spec = pl.BlockSpec((nck, CHUNK, GW), lambda s, g: (s, 0, g))
    n_layers = DEPTH // 2
    state_slots = None
    if emit_state:
        out_shape = (y_shape, jax.ShapeDtypeStruct((n_seq, n_layers, 2, HEADS * HEADDIM, STATE), F32))
        if keep_state:
            state_slots = (True,)
            st_spec = pl.BlockSpec((None, 1, 2, GW, STATE), lambda s, g: (s, state_layer, 0, g, 0))
        else:
            state_slots = tuple(l == state_layer for l in range(n_layers))
            st_spec = pl.BlockSpec((None, n_layers, 2, GW, STATE), lambda s, g: (s, 0, 0, g, 0))
        out_specs = (y_spec, st_spec)
    else:
        out_shape, out_specs = y_shape, y_spec
    return pl.pallas_call(
        functools.partial(_ssd_scan_kernel, n_chunks=nck, has_h0=has_h0, n_kept=len(aliases),
                          state_slots=state_slots),
        out_shape=out_shape,
        grid=(n_seq, GROUPS),
        in_specs=in_specs,
        out_specs=out_specs,
        scratch_shapes=[pltpu.VMEM((2, STATE, GW), F32)],
        input_output_aliases=aliases,
        compiler_params=_cparams(("arbitrary", "arbitrary")),
        name="ssd_scan",
    )(*args)


def _ssd_out_kernel(x_ref, yc_ref, yl_ref, z_ref, mod_ref, nw_ref, wout_ref, o_ref):
    y = jnp.where(pl.program_id(0) < R_CTX // TM_OUT, yc_ref[...], yl_ref[...])
    v = y * _silu(z_ref[...])
    v = v * lax.rsqrt(jnp.mean(v * v, axis=-1, keepdims=True) + EPS) * nw_ref[...]
    mix = jnp.dot(v.astype(BF16), wout_ref[...], preferred_element_type=F32)
    o_ref[...] = x_ref[...] + mod_ref[2:3, :] * mix


def _ssd_out(x, y_ctx, y_lat, z, mod_l, nw, w_out):
    n_ctx = R_CTX // TM_OUT
    return pl.pallas_call(
        _ssd_out_kernel,
        out_shape=jax.ShapeDtypeStruct((R, D), F32),
        grid=(R // TM_OUT,),
        in_specs=[
            pl.BlockSpec((TM_OUT, D), lambda i: (i, 0)),
            pl.BlockSpec((TM_OUT, D_INNER), lambda i: (jnp.minimum(i, n_ctx - 1), 0)),
            pl.BlockSpec((TM_OUT, D_INNER), lambda i: (jnp.maximum(i - n_ctx, 0), 0)),
            pl.BlockSpec((TM_OUT, D_INNER), lambda i: (i, 0)),
            pl.BlockSpec((None, MOD_CHUNKS, D), lambda i: (_cond_of_tile(i * TM_OUT // TM), 0, 0)),
            pl.BlockSpec((1, D_INNER), lambda i: (0, 0)),
            pl.BlockSpec((D_INNER, D), lambda i: (0, 0)),
        ],
        out_specs=pl.BlockSpec((TM_OUT, D), lambda i: (i, 0)),
        compiler_params=_cparams(("arbitrary",)),
        name="ssd_out",
    )(x, y_ctx, y_lat, z, mod_l, nw, w_out)


def _router_kernel(x_ref, mod_ref, nw_ref, wrT_ref, hn_ref, affT_ref):
    hn = _norm_mod(x_ref[...], nw_ref[...], mod_ref[4:5, :], mod_ref[3:4, :])
    hn_hi = hn.astype(BF16)
    hn_ref[...] = hn_hi
    hn_lo = (hn - hn_hi.astype(F32)).astype(BF16)
    w = wrT_ref[...]
    w_hi = w.astype(BF16)
    w_lo = (w - w_hi.astype(F32)).astype(BF16)
    logits = lax.dot_general(jnp.concatenate([w_hi, w_lo, w_hi], axis=1), jnp.concatenate([hn_hi, hn_hi, hn_lo], axis=1),
                             (((1,), (1,)), ((), ())), preferred_element_type=F32)
    e = jnp.exp(logits - jnp.max(logits, axis=0, keepdims=True))
    affT_ref[...] = e / jnp.sum(e, axis=0, keepdims=True)


def _router(x, mod_l, nw, wrT):
    return pl.pallas_call(
        _router_kernel,
        out_shape=(jax.ShapeDtypeStruct((R, D), BF16), jax.ShapeDtypeStruct((N_EXPERTS, R), F32)),
        grid=(R // TM,),
        in_specs=[
            pl.BlockSpec((TM, D), lambda i: (i, 0)),
            pl.BlockSpec((None, MOD_CHUNKS, D), lambda i: (_cond_of_tile(i), 0, 0)),
            pl.BlockSpec((1, D), lambda i: (0, 0)),
            pl.BlockSpec((N_EXPERTS, D), lambda i: (0, 0)),
        ],
        out_specs=(pl.BlockSpec((TM, D), lambda i: (i, 0)), pl.BlockSpec((N_EXPERTS, TM), lambda i: (0, i))),
        compiler_params=_cparams(("arbitrary",)),
        name="router",
    )(x, mod_l, nw, wrT)


def _excl_cumsum_lanes(m):
    blk = 256
    t = m.shape[1]
    a = lax.broadcasted_iota(jnp.int32, (blk, blk), 0)
    b = lax.broadcasted_iota(jnp.int32, (blk, blk), 1)
    strict = jnp.where(a < b, 1.0, 0.0).astype(BF16)
    carry = jnp.zeros((m.shape[0], 1), F32)
    outs = []
    for k in range(t // blk):
        mk = m[:, k * blk:(k + 1) * blk]
        outs.append(jnp.dot(mk.astype(BF16), strict, preferred_element_type=F32) + carry)
        carry = carry + jnp.sum(mk, axis=1, keepdims=True)
    return outs[0] if len(outs) == 1 else jnp.concatenate(outs, axis=1)


def _select_request(aff, cap, base):
    thr_bits = jnp.zeros((N_EXPERTS, 1), jnp.int32)
    for k in range(30, -1, -1):
        trial = thr_bits | (1 << k)
        cnt = jnp.sum(jnp.where(aff >= pltpu.bitcast(trial, F32), 1.0, 0.0), axis=1, keepdims=True)
        thr_bits = jnp.where(cnt >= cap, trial, thr_bits)
    thr = pltpu.bitcast(thr_bits, F32)
    gt = aff > thr
    eq = jnp.where(aff == thr, 1.0, 0.0)
    need = cap - jnp.sum(jnp.where(gt, 1.0, 0.0), axis=1, keepdims=True)
    sel = gt | ((eq > 0.0) & (_excl_cumsum_lanes(eq) < need))
    pos = _excl_cumsum_lanes(jnp.where(sel, 1.0, 0.0))
    return jnp.where(sel, pos + base, -1.0), jnp.where(sel, aff, 0.0)


def _select_kernel(affT_ref, slotT_ref, slot_tok_ref, gate_tok_ref, lo_ref):
    s = pl.program_id(0)

    def emit(slot, gate):
        slotT_ref[...] = slot.astype(jnp.int32)
        pad = jnp.zeros((128 - N_EXPERTS, SR_TOKENS), F32)
        slot_tok_ref[...] = jnp.concatenate([slot, pad - 1.0], axis=0).T.astype(jnp.int32)
        gate_tok_ref[...] = jnp.concatenate([gate, pad], axis=0).T
        t = lax.broadcasted_iota(jnp.int32, (SR_TOKENS, 128), 0)
        jcol = lax.broadcasted_iota(jnp.int32, (SR_TOKENS, 128), 1)
        before = jnp.where(t < jcol * BLK, 1.0, 0.0).astype(BF16)
        chosen = jnp.where(slot >= 0.0, 1.0, 0.0).astype(BF16)
        lo_ref[...] = jnp.dot(chosen, before, preferred_element_type=F32).astype(jnp.int32)

    @pl.when(s < N_SR_CTX)
    def _():
        outs = [_select_request(affT_ref[:, r * L_CTX:(r + 1) * L_CTX], CAP_CTX, float(r * CAP_CTX))
                for r in range(CTX_PER_SR)]
        emit(jnp.concatenate([o[0] for o in outs], axis=1), jnp.concatenate([o[1] for o in outs], axis=1))

    @pl.when(s >= N_SR_CTX)
    def _():
        emit(*_select_request(affT_ref[...], CAP_LAT, 0.0))


def _select(affT):
    return pl.pallas_call(
        _select_kernel,
        out_shape=(
            jax.ShapeDtypeStruct((N_EXPERTS, R), jnp.int32),
            jax.ShapeDtypeStruct((R, 128), jnp.int32),
            jax.ShapeDtypeStruct((R, 128), F32),
            jax.ShapeDtypeStruct((N_SR, N_EXPERTS, 128), jnp.int32),
        ),
        grid=(N_SR,),
        in_specs=[pl.BlockSpec((N_EXPERTS, SR_TOKENS), lambda s: (0, s))],
        out_specs=(
            pl.BlockSpec((N_EXPERTS, SR_TOKENS), lambda s: (0, s)),
            pl.BlockSpec((SR_TOKENS, 128), lambda s: (s, 0)),
            pl.BlockSpec((SR_TOKENS, 128), lambda s: (s, 0)),
            pl.BlockSpec((None, N_EXPERTS, 128), lambda s: (s, 0, 0)),
        ),
        compiler_params=_cparams(("arbitrary",)),
        name="select",
    )(affT)


def _lo_at(lo_ref, sr, e, j):
    return lo_ref[(sr * N_EXPERTS + e) * LO_COLS + j]


def _gather_kernel(lo_ref, hn_ref, slotT_ref, xe_ref, acc_ref):
    sr = pl.program_id(0)
    e0 = pl.program_id(1) * E_HALF
    starts = {}
    fits = None
    for e in range(E_HALF):
        for j in range(N_BLK):
            start = (_lo_at(lo_ref, sr, e0 + e, j) // BF16_ROWS) * BF16_ROWS
            ok = _lo_at(lo_ref, sr, e0 + e, j + 1) - start <= GW_ROWS
            fits = ok if fits is None else jnp.logical_and(fits, ok)
            starts[e, j] = start

    @pl.when(fits)
    def _():
        acc_ref[...] = jnp.zeros(acc_ref.shape, BF16)
        w_iota = lax.broadcasted_iota(jnp.int32, (GW_ROWS, BLK), 0)
        for j in range(N_BLK):
            rows = [jnp.where(w_iota == slotT_ref[e:e + 1, j * BLK:(j + 1) * BLK] - starts[e, j], 1.0, 0.0).astype(BF16)
                    for e in range(E_HALF)]
            part = jnp.dot(jnp.concatenate(rows, axis=0), hn_ref[j * BLK:(j + 1) * BLK, :],
                           preferred_element_type=F32).astype(BF16)
            for e in range(E_HALF):
                win = pl.ds(pl.multiple_of(starts[e, j], BF16_ROWS), GW_ROWS)
                acc_ref[e, win, :] = acc_ref[e, win, :] + part[e * GW_ROWS:(e + 1) * GW_ROWS, :]
        xe_ref[...] = acc_ref[:, :SR_SLOTS, :]

    @pl.when(jnp.logical_not(fits))
    def _():
        s_iota = lax.broadcasted_iota(jnp.int32, (SR_SLOTS, SR_TOKENS), 0)
        for e in range(E_HALF):
            onehot = jnp.where(s_iota == slotT_ref[e:e + 1, :], 1.0, 0.0).astype(BF16)
            xe_ref[e] = jnp.dot(onehot, hn_ref[...], preferred_element_type=F32).astype(BF16)


def _gather(lo_flat, hn, slotT):
    return pl.pallas_call(
        _gather_kernel,
        out_shape=jax.ShapeDtypeStruct((N_EXPERTS, N_SR * SR_SLOTS, D), BF16),
        grid_spec=pltpu.PrefetchScalarGridSpec(
            num_scalar_prefetch=1,
            grid=(N_SR, N_EXPERTS // E_HALF),
            in_specs=[
                pl.BlockSpec((SR_TOKENS, D), lambda s, h, lo: (s, 0)),
                pl.BlockSpec((E_HALF, SR_TOKENS), lambda s, h, lo: (h, s)),
            ],
            out_specs=pl.BlockSpec((E_HALF, SR_SLOTS, D), lambda s, h, lo: (h, s, 0)),
            scratch_shapes=[pltpu.VMEM((E_HALF, SR_SLOTS + GW_ROWS, D), BF16)],
        ),
        compiler_params=_cparams(("arbitrary", "arbitrary")),
        name="gather",
    )(lo_flat, hn, slotT)


def _ffn_kernel(xe_ref, wg_ref, wu_ref, wd_ref, ye_ref, acc_ref):
    f = pl.program_id(1)
    xe = xe_ref[...]
    hg = jnp.dot(xe, wg_ref[...].astype(BF16), preferred_element_type=F32)
    hu = jnp.dot(xe, wu_ref[...].astype(BF16), preferred_element_type=F32)
    hid = (_silu(hg) * hu).astype(BF16)
    part = jnp.dot(hid, wd_ref[...].astype(BF16), preferred_element_type=F32)

    @pl.when(f == 0)
    def _():
        acc_ref[...] = part

    @pl.when(f > 0)
    def _():
        acc_ref[...] = acc_ref[...] + part

    @pl.when(f == pl.num_programs(1) - 1)
    def _():
        ye_ref[...] = acc_ref[...].astype(BF16)


def _ffn(xe, wg, wu, wd, layer):
    tf = 512
    m = xe.shape[1]
    return pl.pallas_call(
        _ffn_kernel,
        out_shape=jax.ShapeDtypeStruct((N_EXPERTS, m, D), BF16),
        grid=(N_EXPERTS, D // tf),
        in_specs=[
            pl.BlockSpec((None, m, D), lambda e, f: (e, 0, 0)),
            pl.BlockSpec((None, None, D, tf), lambda e, f: (layer, e, 0, f)),
            pl.BlockSpec((None, None, D, tf), lambda e, f: (layer, e, 0, f)),
            pl.BlockSpec((None, None, tf, D), lambda e, f: (layer, e, f, 0)),
        ],
        out_specs=pl.BlockSpec((None, m, D), lambda e, f: (e, 0, 0)),
        scratch_shapes=[pltpu.VMEM((m, D), F32)],
        compiler_params=_cparams(("arbitrary", "arbitrary")),
        name="expert_ffn",
    )(xe, wg, wu, wd)


def _combine_kernel(lo_ref, x_ref, ye_ref, slot_ref, gate_ref, mod_ref, fnw_ref, o_ref, yw_ref, acc_ref, *,
                    final_norm):
    sr = pl.program_id(0)
    j = pl.program_id(1)
    starts = []
    fits = None
    for e in range(N_EXPERTS):
        start = jnp.minimum((_lo_at(lo_ref, sr, e, j) // BF16_ROWS) * BF16_ROWS, SR_SLOTS - CW_ROWS)
        ok = _lo_at(lo_ref, sr, e, j + 1) - start <= CW_ROWS
        fits = ok if fits is None else jnp.logical_and(fits, ok)
        starts.append(start)
    slot = slot_ref[...]
    gate = gate_ref[...]

    @pl.when(fits)
    def _():
        for e in range(N_EXPERTS):
            yw_ref[e * CW_ROWS:(e + 1) * CW_ROWS, :] = ye_ref[e, pl.ds(pl.multiple_of(starts[e], BF16_ROWS), CW_ROWS), :]
        n_k = N_EXPERTS * CW_ROWS
        k_exp = lax.broadcasted_iota(jnp.int32, (1, n_k), 1) // CW_ROWS
        k_row = (lax.broadcasted_iota(jnp.int32, (1, n_k), 1) % CW_ROWS).astype(F32)
        start_k = jnp.zeros((1, n_k), F32)
        for e in range(N_EXPERTS):
            start_k = jnp.where(k_exp == e, starts[e].astype(F32), start_k)
        spread = jnp.where(lax.broadcasted_iota(jnp.int32, (128, n_k), 0) == k_exp, 1.0, 0.0).astype(BF16)
        slot_k = jnp.dot(slot.astype(F32).astype(BF16), spread, preferred_element_type=F32)
        g_hi = gate.astype(BF16)
        g_lo = (gate - g_hi.astype(F32)).astype(BF16)
        hit = slot_k - start_k == k_row
        sw_hi = jnp.where(hit, jnp.dot(g_hi, spread, preferred_element_type=F32), 0.0).astype(BF16)
        sw_lo = jnp.where(hit, jnp.dot(g_lo, spread, preferred_element_type=F32), 0.0).astype(BF16)
        yw = yw_ref[...]
        acc_ref[...] = (jnp.dot(sw_hi, yw, preferred_element_type=F32)
                        + jnp.dot(sw_lo, yw, preferred_element_type=F32))

    @pl.when(jnp.logical_not(fits))
    def _():
        s_iota = lax.broadcasted_iota(jnp.int32, (BLK, SR_SLOTS), 1)
        acc = jnp.zeros((BLK, D), F32)
        for e in range(N_EXPERTS):
            onehot = jnp.where(slot[:, e:e + 1] == s_iota, 1.0, 0.0).astype(BF16)
            acc = acc + gate[:, e:e + 1] * jnp.dot(onehot, ye_ref[e], preferred_element_type=F32)
        acc_ref[...] = acc

    out = x_ref[...] + mod_ref[5:6, :] * acc_ref[...]
    if final_norm:
        out = out * lax.rsqrt(jnp.mean(out * out, axis=-1, keepdims=True) + EPS) * fnw_ref[...]
    o_ref[...] = out


def _combine(lo_flat, x, ye, slot_tok, gate_tok, mod_l, fnw, *, final_norm):
    blk_of = lambda s, j: s * N_BLK + j
    return pl.pallas_call(
        functools.partial(_combine_kernel, final_norm=final_norm),
        out_shape=jax.ShapeDtypeStruct((R, D), F32),
        grid_spec=pltpu.PrefetchScalarGridSpec(
            num_scalar_prefetch=1,
            grid=(N_SR, N_BLK),
            in_specs=[
                pl.BlockSpec((BLK, D), lambda s, j, lo: (blk_of(s, j), 0)),
                pl.BlockSpec((N_EXPERTS, SR_SLOTS, D), lambda s, j, lo: (0, s, 0)),
                pl.BlockSpec((BLK, 128), lambda s, j, lo: (blk_of(s, j), 0)),
                pl.BlockSpec((BLK, 128), lambda s, j, lo: (blk_of(s, j), 0)),
                pl.BlockSpec((None, MOD_CHUNKS, D),
                             lambda s, j, lo: (_cond_of_tile(blk_of(s, j) * BLK // TM), 0, 0)),
                pl.BlockSpec((1, D), lambda s, j, lo: (0, 0)),
            ],
            out_specs=pl.BlockSpec((BLK, D), lambda s, j, lo: (blk_of(s, j), 0)),
            scratch_shapes=[pltpu.VMEM((N_EXPERTS * CW_ROWS, D), BF16), pltpu.VMEM((BLK, D), F32)],
        ),
        compiler_params=_cparams(("arbitrary", "arbitrary")),
        name="combine",
    )(lo_flat, x, ye, slot_tok, gate_tok, mod_l, fnw)


def _moe(x, mod_l, nw, wrT, wg, wu, wd, fnw, layer, *, final_norm):
    hn, affT = _router(x, mod_l, nw, wrT)
    slotT, slot_tok, gate_tok, lo = _select(affT)
    lo_flat = lo[:, :, :LO_COLS].reshape(-1)
    xe = _gather(lo_flat, hn, slotT)
    ye = _ffn(xe, wg, wu, wd, layer)
    return _combine(lo_flat, x, ye, slot_tok, gate_tok, mod_l, fnw, final_norm=final_norm)


def _group_major(p):
    return p.reshape(2, GROUPS, HPG).transpose(1, 0, 2).reshape(GROUPS, 2 * HPG)


def kernel(x_prompt, x_sample, state_ssm, c, c_ctx, norm1_w, norm2_w, w_mod, b_mod, conv_in_w, conv_w, conv_out_w, ssd_in_w, ssd_conv_w, ssd_conv_b, ssd_dt_bias, ssd_a_log, ssd_d, ssd_norm_w, ssd_out_w, router_w, exp_w_gate, exp_w_up, exp_w_down, final_norm_w):
    x = jnp.concatenate([x_prompt.reshape(R_CTX, D), x_sample.reshape(R_LAT, D)], axis=0)
    cond = jnp.concatenate([c_ctx[None, :], c, jnp.zeros((8 - N_COND, D), F32)], axis=0)
    mods = _modulation(cond.T, w_mod, b_mod)
    mods = mods[:, :N_COND].reshape(DEPTH, N_COND, MOD_CHUNKS, D)
    fnw = final_norm_w.reshape(1, D)

    states = None
    for layer in range(DEPTH):
        mod_l = mods[layer]
        j = layer // 2
        nw1 = norm1_w[layer].reshape(1, D)
        if layer % 2 == 0:
            x = _conv_mixer(x, mod_l, nw1, conv_in_w[j].astype(BF16), conv_w[j], conv_out_w[j].astype(BF16))
        else:
            w_in = ssd_in_w[j]
            w_dt = w_in[:, D_INNER + CONV_DIM:].reshape(D, 2, GROUPS, HPG).transpose(0, 2, 1, 3)
            w_dt_rep = jnp.broadcast_to(w_dt.reshape(D, GROUPS, 1, 2 * HPG), (D, GROUPS, N_SPLIT, 2 * HPG))
            w_in_b = jnp.concatenate([w_in[:, :D_INNER + CONV_DIM], w_dt_rep.reshape(D, GROUPS * DT_LANES)],
                                     axis=1).astype(BF16)
            dtb = _group_major(ssd_dt_bias[j])
            dtb_rep = jnp.tile(dtb, (1, N_SPLIT)).reshape(1, GROUPS * DT_LANES)
            z, xs, bm, cm, dt4, dtT4 = _ssd_in(
                x, mod_l, nw1, w_in_b, w_dt.reshape(D, 2 * HEADS).T.astype(BF16), ssd_conv_w[j],
                ssd_conv_b[j].reshape(1, CONV_DIM), dtb_rep, dtb.reshape(2 * HEADS, 1))
            alog4 = _group_major(ssd_a_log[j])
            dsum = (ssd_d[j][0] + ssd_d[j][1]).reshape(GROUPS, HPG)
            dsk4 = jnp.tile(jnp.concatenate([jnp.zeros_like(dsum), dsum], axis=1), (1, N_SPLIT))
            scan_args = (xs.reshape(R // CHUNK, CHUNK, D_INNER), bm.reshape(R // CHUNK, CHUNK, GROUPS * STATE),
                         cm.reshape(R // CHUNK, CHUNK, GROUPS * STATE), dt4, dtT4,
                         jnp.tile(alog4, (1, N_SPLIT)).reshape(GROUPS, 1, DT_LANES),
                         alog4.reshape(GROUPS, 2 * HPG, 1), dsk4.reshape(GROUPS, 1, DT_LANES))
            y_ctx, states = _ssd_scan(*scan_args, None, states, n_seq=N_CTX, seq_len=L_CTX, row_block0=0,
                                      state_layer=j)
            h0 = state_ssm[:, j].reshape(N_LAT, 2, HEADS * HEADDIM, STATE)
            y_lat = _ssd_scan(*scan_args, h0, None, n_seq=N_LAT, seq_len=L_LAT, row_block0=R_CTX // L_LAT,
                              state_layer=None)
            x = _ssd_out(x, y_ctx.reshape(R_CTX, D_INNER), y_lat.reshape(R_LAT, D_INNER), z, mod_l,
                         ssd_norm_w[j].reshape(1, D_INNER), ssd_out_w[j].astype(BF16))
        x = _moe(x, mod_l, norm2_w[layer].reshape(1, D), router_w[layer].T, exp_w_gate, exp_w_up, exp_w_down, fnw,
                 layer, final_norm=(layer == DEPTH - 1))

    y_prompt = x[:R_CTX].reshape(N_CTX, L_CTX, D)
    y_sample = x[R_CTX:].reshape(N_LAT, L_LAT, D)
    return y_prompt, y_sample, states.reshape(N_CTX, DEPTH // 2, 2, HEADS, HEADDIM, STATE)
```

```python
import functools

import jax
import jax.numpy as jnp
import numpy as np
from jax import lax
from jax.experimental import pallas as pl
from jax.experimental.pallas import tpu as pltpu

F32 = jnp.float32
BF16 = jnp.bfloat16
HIGHEST = lax.Precision.HIGHEST
LOG2E = 1.4426950408889634

D = 1024
DEPTH = 4
N_CTX, L_CTX = 16, 256
N_LAT, L_LAT = 2, 2048
GRID_W = 64
R_CTX = N_CTX * L_CTX
R_LAT = N_LAT * L_LAT
R = R_CTX + R_LAT
N_COND = 1 + N_LAT
MOD_CHUNKS = 6
D_INNER = 2048
HEADDIM = 64
HEADS = 32
GROUPS = 4
HPG = HEADS // GROUPS
GW = HPG * HEADDIM
STATE = 128
CHUNK = 128
CONV_DIM = D_INNER + 2 * GROUPS * STATE
N_SPLIT = 3
DT_LANES = N_SPLIT * 2 * HPG
N_EXPERTS = 16
EPS = 1e-6

TM = 512
TM_OUT = 512
CTX_TILES = R_CTX // TM
LAT_TILES_PER_REQ = L_LAT // TM
SR_TOKENS = 2048
N_SR = R // SR_TOKENS
N_SR_CTX = R_CTX // SR_TOKENS
SR_SLOTS = 2 * SR_TOKENS // N_EXPERTS
CTX_PER_SR = SR_TOKENS // L_CTX
CAP_CTX = 2 * L_CTX // N_EXPERTS
CAP_LAT = 2 * L_LAT // N_EXPERTS
BLK = 256
N_BLK = SR_TOKENS // BLK
LO_COLS = 16
BF16_ROWS = 16
GW_ROWS = 80
CW_ROWS = 64
E_HALF = N_EXPERTS // 2
VMEM_LIMIT = 56 * 1024 * 1024


def _cparams(sem):
    return pltpu.CompilerParams(dimension_semantics=sem, vmem_limit_bytes=VMEM_LIMIT)


def _cond_of_tile(i):
    return jnp.where(i < CTX_TILES, 0, 1 + (i - CTX_TILES) // LAT_TILES_PER_REQ)


def _silu(v):
    h = 0.5 * v
    return h + h * jnp.tanh(h)


def _softplus(v):
    return jnp.maximum(v, 0.0) + jnp.log1p(jnp.exp(-jnp.abs(v)))


def _norm_mod(x, nw, scale, shift):
    y = x * lax.rsqrt(jnp.mean(x * x, axis=-1, keepdims=True) + EPS)
    return y * nw * (1.0 + scale) + shift


def _conv3_rows(p, w_ref, tile_idx):
    n = p.shape[0]
    period = jnp.where(tile_idx < CTX_TILES, L_CTX, GRID_W)
    r = lax.broadcasted_iota(jnp.int32, (n, 1), 0) & (period - 1)
    prev = jnp.where(r == 0, 0.0, pltpu.roll(p, 1, axis=0))
    nxt = jnp.where(r == period - 1, 0.0, pltpu.roll(p, n - 1, axis=0))
    return prev * w_ref[0:1, :] + p * w_ref[1:2, :] + nxt * w_ref[2:3, :]


def _split_pieces(v):
    pieces = []
    r = v
    for _ in range(N_SPLIT):
        p = r.astype(BF16)
        pieces.append(p)
        r = r - p.astype(F32)
    return pieces


def _modulation_kernel(condT_ref, w_ref, b_ref, o_ref):
    s = _silu(condT_ref[...])
    w = w_ref[...]
    rows = [jnp.sum(w * s[:, r:r + 1], axis=0, keepdims=True) + b_ref[...] for r in range(N_COND)]
    rows.append(jnp.zeros((8 - N_COND, w.shape[1]), F32))
    o_ref[...] = jnp.concatenate(rows, axis=0)


def _modulation(condT, w_mod, b_mod):
    tn = 1536
    n = MOD_CHUNKS * D
    return pl.pallas_call(
        _modulation_kernel,
        out_shape=jax.ShapeDtypeStruct((DEPTH, 8, n), F32),
        grid=(DEPTH, n // tn),
        in_specs=[
            pl.BlockSpec((D, 8), lambda l, j: (0, 0)),
            pl.BlockSpec((None, D, tn), lambda l, j: (l, 0, j)),
            pl.BlockSpec((None, 1, tn), lambda l, j: (l, 0, j)),
        ],
        out_specs=pl.BlockSpec((None, 8, tn), lambda l, j: (l, 0, j)),
        compiler_params=_cparams(("arbitrary", "arbitrary")),
        name="modulation",
    )(condT, w_mod, b_mod.reshape(DEPTH, 1, n))


def _conv_mixer_kernel(*refs, split_input):
    i = pl.program_id(0)
    if split_input:
        xc_ref, xl_ref, mod_ref, nw_ref, win_ref, cw_ref, wout_ref, o_ref = refs
        x = jnp.where(i < CTX_TILES, xc_ref[...], xl_ref[...])
    else:
        x_ref, mod_ref, nw_ref, win_ref, cw_ref, wout_ref, o_ref = refs
        x = x_ref[...]
    hn = _norm_mod(x, nw_ref[...], mod_ref[1:2, :], mod_ref[0:1, :]).astype(BF16)
    gb = jnp.dot(hn, win_ref[:, 0:D], preferred_element_type=F32)
    gc = jnp.dot(hn, win_ref[:, D:2 * D], preferred_element_type=F32)
    v = jnp.dot(hn, win_ref[:, 2 * D:3 * D], preferred_element_type=F32)
    q = (gb * _conv3_rows(gc * v, cw_ref, i)).astype(BF16)
    mix = jnp.dot(q, wout_ref[...], preferred_element_type=F32)
    o_ref[...] = x + mod_ref[2:3, :] * mix


def _conv_mixer(xs, mod_l, nw, w_in, cw, w_out):
    split_input = isinstance(xs, tuple)
    if split_input:
        x_specs = [pl.BlockSpec((TM, D), lambda i: (jnp.minimum(i, CTX_TILES - 1), 0)),
                   pl.BlockSpec((TM, D), lambda i: (jnp.maximum(i - CTX_TILES, 0), 0))]
    else:
        xs = (xs,)
        x_specs = [pl.BlockSpec((TM, D), lambda i: (i, 0))]
    return pl.pallas_call(
        functools.partial(_conv_mixer_kernel, split_input=split_input),
        out_shape=jax.ShapeDtypeStruct((R, D), F32),
        grid=(R // TM,),
        in_specs=x_specs + [
            pl.BlockSpec((None, MOD_CHUNKS, D), lambda i: (_cond_of_tile(i), 0, 0)),
            pl.BlockSpec((1, D), lambda i: (0, 0)),
            pl.BlockSpec((D, 3 * D), lambda i: (0, 0)),
            pl.BlockSpec((3, D), lambda i: (0, 0)),
            pl.BlockSpec((D, D), lambda i: (0, 0)),
        ],
        out_specs=pl.BlockSpec((TM, D), lambda i: (i, 0)),
        compiler_params=_cparams(("arbitrary",)),
        name="conv_mixer",
    )(*xs, mod_l, nw, w_in, cw, w_out)


def _ssd_in_kernel(x_ref, mod_ref, nw_ref, win_ref, wdtT_ref, cw_ref, cb_ref, dtb_ref, dtbT_ref,
                   z_ref, xs_ref, bm_ref, cm_ref, dt_ref, dtT_ref):
    i = pl.program_id(0)
    hn = _norm_mod(x_ref[...], nw_ref[...], mod_ref[1:2, :], mod_ref[0:1, :]).astype(BF16)
    for k in range(D_INNER // D):
        z_ref[:, k * D:(k + 1) * D] = jnp.dot(hn, win_ref[:, k * D:(k + 1) * D], preferred_element_type=F32)
    for k in range(CONV_DIM // D):
        lo = D_INNER + k * D
        u = jnp.dot(hn, win_ref[:, lo:lo + D], preferred_element_type=F32)
        u = _silu(_conv3_rows(u, cw_ref.at[:, k * D:(k + 1) * D], i) + cb_ref[:, k * D:(k + 1) * D])
        if k < D_INNER // D:
            xs_ref[:, k * D:(k + 1) * D] = u
        else:
            bm_ref[...] = u[:, :GROUPS * STATE]
            cm_ref[...] = u[:, GROUPS * STATE:]
    lo = D_INNER + CONV_DIM
    dt = _softplus(jnp.dot(hn, win_ref[:, lo:lo + GROUPS * DT_LANES], preferred_element_type=F32) + dtb_ref[...])
    dtT = _softplus(
        lax.dot_general(wdtT_ref[...], hn, (((1,), (1,)), ((), ())), preferred_element_type=F32) + dtbT_ref[...])
    for g in range(GROUPS):
        dt_ref[g] = dt[:, g * DT_LANES:(g + 1) * DT_LANES].reshape(TM // CHUNK, CHUNK, DT_LANES)
        for k in range(TM // CHUNK):
            dtT_ref[g, k] = dtT[g * 2 * HPG:(g + 1) * 2 * HPG, k * CHUNK:(k + 1) * CHUNK]


def _ssd_in(x, mod_l, nw, w_in, w_dtT, cw, cb, dtb, dtbT):
    n_in = w_in.shape[1]
    nck = TM // CHUNK
    return pl.pallas_call(
        _ssd_in_kernel,
        out_shape=(
            jax.ShapeDtypeStruct((R, D_INNER), F32),
            jax.ShapeDtypeStruct((R, D_INNER), F32),
            jax.ShapeDtypeStruct((R, GROUPS * STATE), F32),
            jax.ShapeDtypeStruct((R, GROUPS * STATE), F32),
            jax.ShapeDtypeStruct((GROUPS, R // CHUNK, CHUNK, DT_LANES), F32),
            jax.ShapeDtypeStruct((GROUPS, R // CHUNK, 2 * HPG, CHUNK), F32),
        ),
        grid=(R // TM,),
        in_specs=[
            pl.BlockSpec((TM, D), lambda i: (i, 0)),
            pl.BlockSpec((None, MOD_CHUNKS, D), lambda i: (_cond_of_tile(i), 0, 0)),
            pl.BlockSpec((1, D), lambda i: (0, 0)),
            pl.BlockSpec((D, n_in), lambda i: (0, 0)),
            pl.BlockSpec((2 * HEADS, D), lambda i: (0, 0)),
            pl.BlockSpec((3, CONV_DIM), lambda i: (0, 0)),
            pl.BlockSpec((1, CONV_DIM), lambda i: (0, 0)),
            pl.BlockSpec((1, GROUPS * DT_LANES), lambda i: (0, 0)),
            pl.BlockSpec((2 * HEADS, 1), lambda i: (0, 0)),
        ],
        out_specs=(
            pl.BlockSpec((TM, D_INNER), lambda i: (i, 0)),
            pl.BlockSpec((TM, D_INNER), lambda i: (i, 0)),
            pl.BlockSpec((TM, GROUPS * STATE), lambda i: (i, 0)),
            pl.BlockSpec((TM, GROUPS * STATE), lambda i: (i, 0)),
            pl.BlockSpec((GROUPS, nck, CHUNK, DT_LANES), lambda i: (0, i, 0, 0)),
            pl.BlockSpec((GROUPS, nck, 2 * HPG, CHUNK), lambda i: (0, i, 0, 0)),
        ),
        compiler_params=_cparams(("arbitrary",)),
        name="ssd_in",
    )(x, mod_l, nw, w_in, w_dtT, cw, cb, dtb, dtbT)


def _ssd_scan_kernel(*refs, n_chunks, has_h0, n_kept, state_slots):
    emit_state = state_slots is not None
    xs_ref, bm_ref, cm_ref, dt_ref, dtT_ref, alog_ref, alogT_ref, dsk_ref = refs[:8]
    tri_k_ref, tri_r_ref, expf_ref, expb_ref, colsel_ref = refs[8:13]
    k = 13
    h0_ref = None
    if has_h0:
        h0_ref = refs[k]
        k += 1
    k += n_kept
    y_ref = refs[k]
    k += 1
    st_out_ref = None
    if emit_state:
        st_out_ref = refs[k]
        k += 1
    st_ref = refs[k]

    a_row = -jnp.exp(alog_ref[...]) * LOG2E
    a_col = -jnp.exp(alogT_ref[...]) * LOG2E
    row_is_fwd = lax.broadcasted_iota(jnp.int32, (2 * HPG, 1), 0) < HPG
    qi = lax.broadcasted_iota(jnp.int32, (CHUNK, CHUNK), 0)
    si = lax.broadcasted_iota(jnp.int32, (CHUNK, CHUNK), 1)
    lower = si <= qi
    lane = lax.broadcasted_iota(jnp.int32, (1, DT_LANES), 1)
    piece_of_lane = lane // (2 * HPG)
    lane_is_fwd = lane % (2 * HPG) < HPG
    lane_c = lax.broadcasted_iota(jnp.int32, (1, CHUNK), 1)
    first_half = lane_c < HEADDIM

    def lane_pieces(v):
        pieces = _split_pieces(v)
        out = pieces[N_SPLIT - 1]
        for r in range(N_SPLIT - 2, -1, -1):
            out = jnp.where(piece_of_lane == r, pieces[r], out)
        return out

    def expand_many(vs, e):
        out = jnp.dot(jnp.concatenate([lane_pieces(v) for v in vs], axis=0), e, preferred_element_type=F32)
        res, r0 = [], 0
        for v in vs:
            res.append(out[r0:r0 + v.shape[0], :])
            r0 += v.shape[0]
        return res

    for d in range(2):
        if has_h0:
            st_ref[d] = h0_ref[d].T
        else:
            st_ref[d] = jnp.zeros((STATE, GW), F32)

    y_ref[...] = jnp.zeros(y_ref.shape, F32)

    def body(k_, carry):
        cf = k_
        x = xs_ref[cf]
        b = bm_ref[cf]
        cm = cm_ref[cf]
        dt = dt_ref[cf]
        dtT = dtT_ref[cf]
        cr = n_chunks - 1 - k_
        xr = xs_ref[cr]
        br = bm_ref[cr]
        dtr = dt_ref[cr]

        dta_rows = jnp.concatenate(_split_pieces(dt * a_row), axis=0)
        dtaT_lanes = jnp.concatenate(_split_pieces(dtT * a_col), axis=1)
        acs2 = jnp.dot(tri_k_ref[...], dta_rows, preferred_element_type=F32)
        acs_lo, acs_up = acs2[:CHUNK, :], acs2[CHUNK:, :]
        acs_t2 = jnp.dot(dtaT_lanes, tri_r_ref[...], preferred_element_type=F32)
        acs_up_r = jnp.dot(tri_k_ref[CHUNK:, :], jnp.concatenate(_split_pieces(dtr * a_row), axis=0),
                           preferred_element_type=F32)
        r_t = jnp.log(dtT) * LOG2E - jnp.where(row_is_fwd, acs_t2[:, :CHUNK], acs_t2[:, CHUNK:])
        col_b = jnp.dot(lane_pieces(jnp.where(lane_is_fwd, acs_lo, acs_up)), colsel_ref[...],
                        preferred_element_type=F32)

        a_last = acs_lo[CHUNK - 1:CHUNK, :]
        a_tot = acs_up_r[0:1, :]
        cb_diag = jnp.sum(cm * b, axis=1, keepdims=True)
        e_out_f, e_in_f = expand_many([jnp.exp2(acs_lo), jnp.exp2(a_last - acs_lo) * dt], expf_ref[...])
        e_self, e_out_b, e_in_b = expand_many(
            [dsk_ref[...] + cb_diag * dt, jnp.exp2(acs_up_r), jnp.exp2(a_tot - acs_up_r) * dtr], expb_ref[...])
        e_keep_f = e_out_f[CHUNK - 1:CHUNK, :]
        e_keep_b = e_out_b[0:1, :]

        cmb = cm.astype(BF16)
        cb = lax.dot_general(cmb, b.astype(BF16), (((1,), (1,)), ((), ())), preferred_element_type=F32)
        parts = []
        for hp in range(HPG // 2):
            ws = []
            for h in (2 * hp, 2 * hp + 1):
                arg = jnp.where(lower, col_b[:, h * CHUNK:(h + 1) * CHUNK] + r_t[h:h + 1, :],
                                col_b[:, (HPG + h) * CHUNK:(HPG + h + 1) * CHUNK] + r_t[HPG + h:HPG + h + 1, :])
                ws.append((cb * jnp.exp2(arg)).astype(BF16))
            xp = x[:, hp * CHUNK:(hp + 1) * CHUNK]
            x2 = jnp.concatenate([jnp.where(first_half, xp, 0.0), jnp.where(first_half, 0.0, xp)], axis=0)
            parts.append(jnp.dot(jnp.concatenate(ws, axis=1), x2.astype(BF16), preferred_element_type=F32))
        st_f = st_ref[0]
        y = (jnp.concatenate(parts, axis=1) + e_self * x
             + jnp.dot(cmb, st_f.astype(BF16), preferred_element_type=F32) * e_out_f)
        y_ref[cf] = y_ref[cf] + y
        st_ref[0] = st_f * e_keep_f + jnp.dot(
            b.T.astype(BF16), (x * e_in_f).astype(BF16), preferred_element_type=F32)

        st_b = st_ref[1]
        y_ref[cr] = y_ref[cr] + jnp.dot(cm_ref[cr].astype(BF16), st_b.astype(BF16),
                                        preferred_element_type=F32) * e_out_b
        st_ref[1] = st_b * e_keep_b + jnp.dot(
            br.T.astype(BF16), (xr * e_in_b).astype(BF16), preferred_element_type=F32)
        return carry

    lax.fori_loop(0, n_chunks, body, 0)
    if emit_state:
        for slot, own in enumerate(state_slots):
            for d in range(2):
                st_out_ref[slot, d] = st_ref[d].T if own else jnp.zeros((GW, STATE), F32)


def _scan_constants():
    q = np.arange(CHUNK)
    tri_lo = (q[None, :] <= q[:, None]).astype(np.float32)
    tri_up = tri_lo.T
    tri_k = np.concatenate([np.tile(tri_lo, (1, N_SPLIT)), np.tile(tri_up, (1, N_SPLIT))], axis=0)
    tri_r = np.concatenate([np.tile(tri_up, (N_SPLIT, 1)), np.tile(tri_lo, (N_SPLIT, 1))], axis=1)
    head = np.arange(DT_LANES) % (2 * HPG)
    chan_head = np.arange(GW) // HEADDIM
    exp_f = (head[:, None] == chan_head[None, :]).astype(np.float32)
    exp_b = (head[:, None] == chan_head[None, :] + HPG).astype(np.float32)
    col_sel = (head[:, None] == (np.arange(2 * HPG * CHUNK) // CHUNK)[None, :]).astype(np.float32)
    return [jnp.asarray(m, BF16) for m in (tri_k, tri_r, exp_f, exp_b, col_sel)]


def _ssd_scan(xs3, bm3, cm3, dt4, dtT4, alog4, alogT4, dskx, h0, st_prev, *, n_seq, seq_len, row_block0, state_layer):
    nck = seq_len // CHUNK
    has_h0 = h0 is not None
    emit_state = state_layer is not None
    keep_state = st_prev is not None
    rb = lambda s: s + row_block0
    in_specs = [
        pl.BlockSpec((nck, CHUNK, GW), lambda s, g: (rb(s), 0, g)),
        pl.BlockSpec((nck, CHUNK, STATE), lambda s, g: (rb(s), 0, g)),
        pl.BlockSpec((nck, CHUNK, STATE), lambda s, g: (rb(s), 0, g)),
        pl.BlockSpec((None, nck, CHUNK, DT_LANES), lambda s, g: (g, rb(s), 0, 0)),
        pl.BlockSpec((None, nck, 2 * HPG, CHUNK), lambda s, g: (g, rb(s), 0, 0)),
        pl.BlockSpec((None, 1, DT_LANES), lambda s, g: (g, 0, 0)),
        pl.BlockSpec((None, 2 * HPG, 1), lambda s, g: (g, 0, 0)),
        pl.BlockSpec((None, 1, DT_LANES), lambda s, g: (g, 0, 0)),
    ]
    args = [xs3, bm3, cm3, dt4, dtT4, alog4, alogT4, dskx]
    for const in _scan_constants():
        in_specs.append(pl.BlockSpec(const.shape, lambda s, g: (0, 0)))
        args.append(const)
    if has_h0:
        in_specs.append(pl.BlockSpec((None, 2, GW, STATE), lambda s, g: (s, 0, g, 0)))
        args.append(h0)
    aliases = {}
    if keep_state:
        in_specs.append(pl.BlockSpec(memory_space=pl.ANY))
        aliases[len(args)] = 1
        args.append(st_prev)
    y_shape = jax.ShapeDtypeStruct((n_seq * nck, CHUNK, D_INNER), F32)
    y_spec = pl.BlockSpec((nck, CHUNK, GW), lambda s, g: (s, 0, g))
    n_layers = DEPTH // 2
    state_slots = None
    if emit_state:
        out_shape = (y_shape, jax.ShapeDtypeStruct((n_seq, n_layers, 2, HEADS * HEADDIM, STATE), F32))
        if keep_state:
            state_slots = (True,)
            st_spec = pl.BlockSpec((None, 1, 2, GW, STATE), lambda s, g: (s, state_layer, 0, g, 0))
        else:
            state_slots = tuple(l == state_layer for l in range(n_layers))
            st_spec = pl.BlockSpec((None, n_layers, 2, GW, STATE), lambda s, g: (s, 0, 0, g, 0))
        out_specs = (y_spec, st_spec)
    else:
        out_shape, out_specs = y_shape, y_spec
    return pl.pallas_call(
        functools.partial(_ssd_scan_kernel, n_chunks=nck, has_h0=has_h0, n_kept=len(aliases),
                          state_slots=state_slots),
        out_shape=out_shape,
        grid=(n_seq, GROUPS),
        in_specs=in_specs,
        out_specs=out_specs,
        scratch_shapes=[pltpu.VMEM((2, STATE, GW), F32)],
        input_output_aliases=aliases,
        compiler_params=_cparams(("arbitrary", "arbitrary")),
        name="ssd_scan",
    )(*args)


def _ssd_out_kernel(x_ref, yc_ref, yl_ref, z_ref, mod_ref, nw_ref, wout_ref, o_ref):
    y = jnp.where(pl.program_id(0) < R_CTX // TM_OUT, yc_ref[...], yl_ref[...])
    v = y * _silu(z_ref[...])
    v = v * lax.rsqrt(jnp.mean(v * v, axis=-1, keepdims=True) + EPS) * nw_ref[...]
    mix = jnp.dot(v.astype(BF16), wout_ref[...], preferred_element_type=F32)
    o_ref[...] = x_ref[...] + mod_ref[2:3, :] * mix


def _ssd_out(x, y_ctx, y_lat, z, mod_l, nw, w_out):
    n_ctx = R_CTX // TM_OUT
    return pl.pallas_call(
        _ssd_out_kernel,
        out_shape=jax.ShapeDtypeStruct((R, D), F32),
        grid=(R // TM_OUT,),
        in_specs=[
            pl.BlockSpec((TM_OUT, D), lambda i: (i, 0)),
            pl.BlockSpec((TM_OUT, D_INNER), lambda i: (jnp.minimum(i, n_ctx - 1), 0)),
            pl.BlockSpec((TM_OUT, D_INNER), lambda i: (jnp.maximum(i - n_ctx, 0), 0)),
            pl.BlockSpec((TM_OUT, D_INNER), lambda i: (i, 0)),
            pl.BlockSpec((None, MOD_CHUNKS, D), lambda i: (_cond_of_tile(i * TM_OUT // TM), 0, 0)),
            pl.BlockSpec((1, D_INNER), lambda i: (0, 0)),
            pl.BlockSpec((D_INNER, D), lambda i: (0, 0)),
        ],
        out_specs=pl.BlockSpec((TM_OUT, D), lambda i: (i, 0)),
        compiler_params=_cparams(("arbitrary",)),
        name="ssd_out",
    )(x, y_ctx, y_lat, z, mod_l, nw, w_out)


def _router_kernel(x_ref, mod_ref, nw_ref, wrT_ref, hn_ref, affT_ref):
    hn = _norm_mod(x_ref[...], nw_ref[...], mod_ref[4:5, :], mod_ref[3:4, :])
    hn_hi = hn.astype(BF16)
    hn_ref[...] = hn_hi
    hn_lo = (hn - hn_hi.astype(F32)).astype(BF16)
    w = wrT_ref[...]
    w_hi = w.astype(BF16)
    w_lo = (w - w_hi.astype(F32)).astype(BF16)
    logits = lax.dot_general(jnp.concatenate([w_hi, w_lo, w_hi], axis=1), jnp.concatenate([hn_hi, hn_hi, hn_lo], axis=1),
                             (((1,), (1,)), ((), ())), preferred_element_type=F32)
    e = jnp.exp(logits - jnp.max(logits, axis=0, keepdims=True))
    affT_ref[...] = e / jnp.sum(e, axis=0, keepdims=True)


def _router(x, mod_l, nw, wrT):
    return pl.pallas_call(
        _router_kernel,
        out_shape=(jax.ShapeDtypeStruct((R, D), BF16), jax.ShapeDtypeStruct((N_EXPERTS, R), F32)),
        grid=(R // TM,),
        in_specs=[
            pl.BlockSpec((TM, D), lambda i: (i, 0)),
            pl.BlockSpec((None, MOD_CHUNKS, D), lambda i: (_cond_of_tile(i), 0, 0)),
            pl.BlockSpec((1, D), lambda i: (0, 0)),
            pl.BlockSpec((N_EXPERTS, D), lambda i: (0, 0)),
        ],
        out_specs=(pl.BlockSpec((TM, D), lambda i: (i, 0)), pl.BlockSpec((N_EXPERTS, TM), lambda i: (0, i))),
        compiler_params=_cparams(("arbitrary",)),
        name="router",
    )(x, mod_l, nw, wrT)


def _excl_cumsum_lanes(m):
    blk = 256
    t = m.shape[1]
    a = lax.broadcasted_iota(jnp.int32, (blk, blk), 0)
    b = lax.broadcasted_iota(jnp.int32, (blk, blk), 1)
    strict = jnp.where(a < b, 1.0, 0.0).astype(BF16)
    carry = jnp.zeros((m.shape[0], 1), F32)
    outs = []
    for k in range(t // blk):
        mk = m[:, k * blk:(k + 1) * blk]
        outs.append(jnp.dot(mk.astype(BF16), strict, preferred_element_type=F32) + carry)
        carry = carry + jnp.sum(mk, axis=1, keepdims=True)
    return outs[0] if len(outs) == 1 else jnp.concatenate(outs, axis=1)


def _select_request(aff, cap, base):
    thr_bits = jnp.zeros((N_EXPERTS, 1), jnp.int32)
    for k in range(30, -1, -1):
        trial = thr_bits | (1 << k)
        cnt = jnp.sum(jnp.where(aff >= pltpu.bitcast(trial, F32), 1.0, 0.0), axis=1, keepdims=True)
        thr_bits = jnp.where(cnt >= cap, trial, thr_bits)
    thr = pltpu.bitcast(thr_bits, F32)
    gt = aff > thr
    eq = jnp.where(aff == thr, 1.0, 0.0)
    need = cap - jnp.sum(jnp.where(gt, 1.0, 0.0), axis=1, keepdims=True)
    sel = gt | ((eq > 0.0) & (_excl_cumsum_lanes(eq) < need))
    pos = _excl_cumsum_lanes(jnp.where(sel, 1.0, 0.0))
    return jnp.where(sel, pos + base, -1.0), jnp.where(sel, aff, 0.0)


def _select_kernel(affT_ref, slotT_ref, slot_tok_ref, gate_tok_ref, lo_ref):
    s = pl.program_id(0)

    def emit(slot, gate):
        slotT_ref[...] = slot.astype(jnp.int32)
        pad = jnp.zeros((128 - N_EXPERTS, SR_TOKENS), F32)
        slot_tok_ref[...] = jnp.concatenate([slot, pad - 1.0], axis=0).T.astype(jnp.int32)
        gate_tok_ref[...] = jnp.concatenate([gate, pad], axis=0).T
        t = lax.broadcasted_iota(jnp.int32, (SR_TOKENS, 128), 0)
        jcol = lax.broadcasted_iota(jnp.int32, (SR_TOKENS, 128), 1)
        before = jnp.where(t < jcol * BLK, 1.0, 0.0).astype(BF16)
        chosen = jnp.where(slot >= 0.0, 1.0, 0.0).astype(BF16)
        lo_ref[...] = jnp.dot(chosen, before, preferred_element_type=F32).astype(jnp.int32)

    @pl.when(s < N_SR_CTX)
    def _():
        outs = [_select_request(affT_ref[:, r * L_CTX:(r + 1) * L_CTX], CAP_CTX, float(r * CAP_CTX))
                for r in range(CTX_PER_SR)]
        emit(jnp.concatenate([o[0] for o in outs], axis=1), jnp.concatenate([o[1] for o in outs], axis=1))

    @pl.when(s >= N_SR_CTX)
    def _():
        emit(*_select_request(affT_ref[...], CAP_LAT, 0.0))


def _select(affT):
    return pl.pallas_call(
        _select_kernel,
        out_shape=(
            jax.ShapeDtypeStruct((N_EXPERTS, R), jnp.int32),
            jax.ShapeDtypeStruct((R, 128), jnp.int32),
            jax.ShapeDtypeStruct((R, 128), F32),
            jax.ShapeDtypeStruct((N_SR, N_EXPERTS, 128), jnp.int32),
        ),
        grid=(N_SR,),
        in_specs=[pl.BlockSpec((N_EXPERTS, SR_TOKENS), lambda s: (0, s))],
        out_specs=(
            pl.BlockSpec((N_EXPERTS, SR_TOKENS), lambda s: (0, s)),
            pl.BlockSpec((SR_TOKENS, 128), lambda s: (s, 0)),
            pl.BlockSpec((SR_TOKENS, 128), lambda s: (s, 0)),
            pl.BlockSpec((None, N_EXPERTS, 128), lambda s: (s, 0, 0)),
        ),
        compiler_params=_cparams(("arbitrary",)),
        name="select",
    )(affT)


def _lo_at(lo_ref, sr, e, j):
    return lo_ref[(sr * N_EXPERTS + e) * LO_COLS + j]


def _gather_kernel(lo_ref, hn_ref, slotT_ref, xe_ref, acc_ref):
    sr = pl.program_id(0)
    e0 = pl.program_id(1) * E_HALF
    starts = {}
    fits = None
    for e in range(E_HALF):
        for j in range(N_BLK):
            start = (_lo_at(lo_ref, sr, e0 + e, j) // BF16_ROWS) * BF16_ROWS
            ok = _lo_at(lo_ref, sr, e0 + e, j + 1) - start <= GW_ROWS
            fits = ok if fits is None else jnp.logical_and(fits, ok)
            starts[e, j] = start

    @pl.when(fits)
    def _():
        acc_ref[...] = jnp.zeros(acc_ref.shape, BF16)
        w_iota = lax.broadcasted_iota(jnp.int32, (GW_ROWS, BLK), 0)
        for j in range(N_BLK):
            rows = [jnp.where(w_iota == slotT_ref[e:e + 1, j * BLK:(j + 1) * BLK] - starts[e, j], 1.0, 0.0).astype(BF16)
                    for e in range(E_HALF)]
            part = jnp.dot(jnp.concatenate(rows, axis=0), hn_ref[j * BLK:(j + 1) * BLK, :],
                           preferred_element_type=F32).astype(BF16)
            for e in range(E_HALF):
                win = pl.ds(pl.multiple_of(starts[e, j], BF16_ROWS), GW_ROWS)
                acc_ref[e, win, :] = acc_ref[e, win, :] + part[e * GW_ROWS:(e + 1) * GW_ROWS, :]
        xe_ref[...] = acc_ref[:, :SR_SLOTS, :]

    @pl.when(jnp.logical_not(fits))
    def _():
        s_iota = lax.broadcasted_iota(jnp.int32, (SR_SLOTS, SR_TOKENS), 0)
        for e in range(E_HALF):
            onehot = jnp.where(s_iota == slotT_ref[e:e + 1, :], 1.0, 0.0).astype(BF16)
            xe_ref[e] = jnp.dot(onehot, hn_ref[...], preferred_element_type=F32).astype(BF16)


def _gather(lo_flat, hn, slotT):
    return pl.pallas_call(
        _gather_kernel,
        out_shape=jax.ShapeDtypeStruct((N_EXPERTS, N_SR * SR_SLOTS, D), BF16),
        grid_spec=pltpu.PrefetchScalarGridSpec(
            num_scalar_prefetch=1,
            grid=(N_SR, N_EXPERTS // E_HALF),
            in_specs=[
                pl.BlockSpec((SR_TOKENS, D), lambda s, h, lo: (s, 0)),
                pl.BlockSpec((E_HALF, SR_TOKENS), lambda s, h, lo: (h, s)),
            ],
            out_specs=pl.BlockSpec((E_HALF, SR_SLOTS, D), lambda s, h, lo: (h, s, 0)),
            scratch_shapes=[pltpu.VMEM((E_HALF, SR_SLOTS + GW_ROWS, D), BF16)],
        ),
        compiler_params=_cparams(("arbitrary", "arbitrary")),
        name="gather",
    )(lo_flat, hn, slotT)


def _ffn_kernel(xe_ref, wg_ref, wu_ref, wd_ref, ye_ref):
    xe = xe_ref[...]
    hg = jnp.dot(xe, wg_ref[...].astype(BF16), preferred_element_type=F32)
    hu = jnp.dot(xe, wu_ref[...].astype(BF16), preferred_element_type=F32)
    hid = (_silu(hg) * hu).astype(BF16)
    ye_ref[...] = jnp.dot(hid, wd_ref[...].astype(BF16), preferred_element_type=F32).astype(BF16)


def _ffn(xe, wg, wu, wd, layer):
    m = xe.shape[1]
    w_spec = pl.BlockSpec((None, None, D, D), lambda e: (layer, e, 0, 0))
    return pl.pallas_call(
        _ffn_kernel,
        out_shape=jax.ShapeDtypeStruct((N_EXPERTS, m, D), BF16),
        grid=(N_EXPERTS,),
        in_specs=[pl.BlockSpec((None, m, D), lambda e: (e, 0, 0)), w_spec, w_spec, w_spec],
        out_specs=pl.BlockSpec((None, m, D), lambda e: (e, 0, 0)),
        compiler_params=_cparams(("arbitrary",)),
        name="expert_ffn",
    )(xe, wg, wu, wd)


def _combine_kernel(lo_ref, x_ref, ye_ref, slot_ref, gate_ref, mod_ref, fnw_ref, *rest, final):
    if final:
        oc_ref, ol_ref, yw_ref, acc_ref = rest
    else:
        o_ref, yw_ref, acc_ref = rest
    sr = pl.program_id(0)
    j = pl.program_id(1)
    starts = []
    fits = None
    for e in range(N_EXPERTS):
        start = jnp.minimum((_lo_at(lo_ref, sr, e, j) // BF16_ROWS) * BF16_ROWS, SR_SLOTS - CW_ROWS)
        ok = _lo_at(lo_ref, sr, e, j + 1) - start <= CW_ROWS
        fits = ok if fits is None else jnp.logical_and(fits, ok)
        starts.append(start)
    slot = slot_ref[...]
    gate = gate_ref[...]

    @pl.when(fits)
    def _():
        for e in range(N_EXPERTS):
            yw_ref[e * CW_ROWS:(e + 1) * CW_ROWS, :] = ye_ref[e, pl.ds(pl.multiple_of(starts[e], BF16_ROWS), CW_ROWS), :]
        n_k = N_EXPERTS * CW_ROWS
        k_exp = lax.broadcasted_iota(jnp.int32, (1, n_k), 1) // CW_ROWS
        k_row = (lax.broadcasted_iota(jnp.int32, (1, n_k), 1) % CW_ROWS).astype(F32)
        start_k = jnp.zeros((1, n_k), F32)
        for e in range(N_EXPERTS):
            start_k = jnp.where(k_exp == e, starts[e].astype(F32), start_k)
        spread = jnp.where(lax.broadcasted_iota(jnp.int32, (128, n_k), 0) == k_exp, 1.0, 0.0).astype(BF16)
        slot_k = jnp.dot(slot.astype(F32).astype(BF16), spread, preferred_element_type=F32)
        g_hi = gate.astype(BF16)
        g_lo = (gate - g_hi.astype(F32)).astype(BF16)
        hit = slot_k - start_k == k_row
        sw_hi = jnp.where(hit, jnp.dot(g_hi, spread, preferred_element_type=F32), 0.0).astype(BF16)
        sw_lo = jnp.where(hit, jnp.dot(g_lo, spread, preferred_element_type=F32), 0.0).astype(BF16)
        yw = yw_ref[...]
        acc_ref[...] = (jnp.dot(sw_hi, yw, preferred_element_type=F32)
                        + jnp.dot(sw_lo, yw, preferred_element_type=F32))

    @pl.when(jnp.logical_not(fits))
    def _():
        s_iota = lax.broadcasted_iota(jnp.int32, (BLK, SR_SLOTS), 1)
        acc = jnp.zeros((BLK, D), F32)
        for e in range(N_EXPERTS):
            onehot = jnp.where(slot[:, e:e + 1] == s_iota, 1.0, 0.0).astype(BF16)
            acc = acc + gate[:, e:e + 1] * jnp.dot(onehot, ye_ref[e], preferred_element_type=F32)
        acc_ref[...] = acc

    out = x_ref[...] + mod_ref[5:6, :] * acc_ref[...]
    if not final:
        o_ref[...] = out
    else:
        out = out * lax.rsqrt(jnp.mean(out * out, axis=-1, keepdims=True) + EPS) * fnw_ref[...]
        ol_ref[...] = out

        @pl.when(sr < N_SR_CTX)
        def _():
            oc_ref[...] = out


def _combine(lo_flat, x, ye, slot_tok, gate_tok, mod_l, fnw, *, final):
    blk_of = lambda s, j: s * N_BLK + j
    n_ctx_blk = R_CTX // BLK
    if final:
        out_shape = (jax.ShapeDtypeStruct((R_CTX, D), F32), jax.ShapeDtypeStruct((R_LAT, D), F32))
        out_specs = (pl.BlockSpec((BLK, D), lambda s, j, lo: (jnp.minimum(blk_of(s, j), n_ctx_blk - 1), 0)),
                     pl.BlockSpec((BLK, D), lambda s, j, lo: (jnp.maximum(blk_of(s, j) - n_ctx_blk, 0), 0)))
    else:
        out_shape = jax.ShapeDtypeStruct((R, D), F32)
        out_specs = pl.BlockSpec((BLK, D), lambda s, j, lo: (blk_of(s, j), 0))
    return pl.pallas_call(
        functools.partial(_combine_kernel, final=final),
        out_shape=out_shape,
        grid_spec=pltpu.PrefetchScalarGridSpec(
            num_scalar_prefetch=1,
            grid=(N_SR, N_BLK),
            in_specs=[
                pl.BlockSpec((BLK, D), lambda s, j, lo: (blk_of(s, j), 0)),
                pl.BlockSpec((N_EXPERTS, SR_SLOTS, D), lambda s, j, lo: (0, s, 0)),
                pl.BlockSpec((BLK, 128), lambda s, j, lo: (blk_of(s, j), 0)),
                pl.BlockSpec((BLK, 128), lambda s, j, lo: (blk_of(s, j), 0)),
                pl.BlockSpec((None, MOD_CHUNKS, D),
                             lambda s, j, lo: (_cond_of_tile(blk_of(s, j) * BLK // TM), 0, 0)),
                pl.BlockSpec((1, D), lambda s, j, lo: (0, 0)),
            ],
            out_specs=out_specs,
            scratch_shapes=[pltpu.VMEM((N_EXPERTS * CW_ROWS, D), BF16), pltpu.VMEM((BLK, D), F32)],
        ),
        compiler_params=_cparams(("arbitrary", "arbitrary")),
        name="combine",
    )(lo_flat, x, ye, slot_tok, gate_tok, mod_l, fnw)


def _moe(x, mod_l, nw, wrT, wg, wu, wd, fnw, layer, *, final):
    hn, affT = _router(x, mod_l, nw, wrT)
    slotT, slot_tok, gate_tok, lo = _select(affT)
    lo_flat = lo[:, :, :LO_COLS].reshape(-1)
    xe = _gather(lo_flat, hn, slotT)
    ye = _ffn(xe, wg, wu, wd, layer)
    return _combine(lo_flat, x, ye, slot_tok, gate_tok, mod_l, fnw, final=final)


def _group_major(p):
    return p.reshape(2, GROUPS, HPG).transpose(1, 0, 2).reshape(GROUPS, 2 * HPG)


def kernel(x_prompt, x_sample, state_ssm, c, c_ctx, norm1_w, norm2_w, w_mod, b_mod, conv_in_w, conv_w, conv_out_w, ssd_in_w, ssd_conv_w, ssd_conv_b, ssd_dt_bias, ssd_a_log, ssd_d, ssd_norm_w, ssd_out_w, router_w, exp_w_gate, exp_w_up, exp_w_down, final_norm_w):
    x = (x_prompt.reshape(R_CTX, D), x_sample.reshape(R_LAT, D))
    cond =jnp.concatenate([c_ctx[None, :], c, jnp.zeros((8 - N_COND, D), F32)], axis=0)
    mods = _modulation(cond.T, w_mod, b_mod)
    mods = mods[:, :N_COND].reshape(DEPTH, N_COND, MOD_CHUNKS, D)
    fnw = final_norm_w.reshape(1, D)

    states = None
    for layer in range(DEPTH):
        mod_l = mods[layer]
        j = layer // 2
        nw1 = norm1_w[layer].reshape(1, D)
        if layer % 2 == 0:
            x = _conv_mixer(x, mod_l, nw1, conv_in_w[j].astype(BF16), conv_w[j], conv_out_w[j].astype(BF16))
        else:
            w_in = ssd_in_w[j]
            w_dt = w_in[:, D_INNER + CONV_DIM:].reshape(D, 2, GROUPS, HPG).transpose(0, 2, 1, 3)
            w_dt_rep = jnp.broadcast_to(w_dt.reshape(D, GROUPS, 1, 2 * HPG), (D, GROUPS, N_SPLIT, 2 * HPG))
            w_in_b = jnp.concatenate([w_in[:, :D_INNER + CONV_DIM], w_dt_rep.reshape(D, GROUPS * DT_LANES)],
                                     axis=1).astype(BF16)
            dtb = _group_major(ssd_dt_bias[j])
            dtb_rep = jnp.tile(dtb, (1, N_SPLIT)).reshape(1, GROUPS * DT_LANES)
            z, xs, bm, cm, dt4, dtT4 = _ssd_in(
                x, mod_l, nw1, w_in_b, w_dt.reshape(D, 2 * HEADS).T.astype(BF16), ssd_conv_w[j],
                ssd_conv_b[j].reshape(1, CONV_DIM), dtb_rep, dtb.reshape(2 * HEADS, 1))
            alog4 = _group_major(ssd_a_log[j])
            dsum = (ssd_d[j][0] + ssd_d[j][1]).reshape(GROUPS, HPG)
            dsk4 = jnp.tile(jnp.concatenate([jnp.zeros_like(dsum), dsum], axis=1), (1, N_SPLIT))
            scan_args = (xs.reshape(R // CHUNK, CHUNK, D_INNER), bm.reshape(R // CHUNK, CHUNK, GROUPS * STATE),
                         cm.reshape(R // CHUNK, CHUNK, GROUPS * STATE), dt4, dtT4,
                         jnp.tile(alog4, (1, N_SPLIT)).reshape(GROUPS, 1, DT_LANES),
                         alog4.reshape(GROUPS, 2 * HPG, 1), dsk4.reshape(GROUPS, 1, DT_LANES))
            y_ctx, states = _ssd_scan(*scan_args, None, states, n_seq=N_CTX, seq_len=L_CTX, row_block0=0,
                                      state_layer=j)
            h0 = state_ssm[:, j].reshape(N_LAT, 2, HEADS * HEADDIM, STATE)
            y_lat = _ssd_scan(*scan_args, h0, None, n_seq=N_LAT, seq_len=L_LAT, row_block0=R_CTX // L_LAT,
                              state_layer=None)
            x = _ssd_out(x, y_ctx.reshape(R_CTX, D_INNER), y_lat.reshape(R_LAT, D_INNER), z, mod_l,
                         ssd_norm_w[j].reshape(1, D_INNER), ssd_out_w[j].astype(BF16))
        x = _moe(x, mod_l, norm2_w[layer].reshape(1, D), router_w[layer].T, exp_w_gate, exp_w_up, exp_w_down, fnw,
                 layer, final=(layer == DEPTH - 1))

    y_prompt, y_sample = x
    return (y_prompt.reshape(N_CTX, L_CTX, D), y_sample.reshape(N_LAT, L_LAT, D),
            states.reshape(N_CTX, DEPTH // 2, 2, HEADS, HEADDIM, STATE))
```

```python
import functools

import jax
import jax.numpy as jnp
import numpy as np
from jax import lax
from jax.experimental import pallas as pl
from jax.experimental.pallas import tpu as pltpu

F32 = jnp.float32
BF16 = jnp.bfloat16
HIGHEST = lax.Precision.HIGHEST
LOG2E = 1.4426950408889634

D = 1024
DEPTH = 4
N_CTX, L_CTX = 16, 256
N_LAT, L_LAT = 2, 2048
GRID_W = 64
R_CTX = N_CTX * L_CTX
R_LAT = N_LAT * L_LAT
R = R_CTX + R_LAT
N_COND = 1 + N_LAT
MOD_CHUNKS = 6
D_INNER = 2048
HEADDIM = 64
HEADS = 32
GROUPS = 4
HPG = HEADS // GROUPS
GW = HPG * HEADDIM
STATE = 128
CHUNK = 128
CONV_DIM = D_INNER + 2 * GROUPS * STATE
N_SPLIT = 2
DT_LANES = N_SPLIT * 2 * HPG
N_EXPERTS = 16
EPS = 1e-6

TM = 512
TM_OUT = 512
CTX_TILES = R_CTX // TM
LAT_TILES_PER_REQ = L_LAT // TM
SR_TOKENS = 2048
N_SR = R // SR_TOKENS
N_SR_CTX = R_CTX // SR_TOKENS
SR_SLOTS = 2 * SR_TOKENS // N_EXPERTS
CTX_PER_SR = SR_TOKENS // L_CTX
CAP_CTX = 2 * L_CTX // N_EXPERTS
CAP_LAT = 2 * L_LAT // N_EXPERTS
BLK = 256
N_BLK = SR_TOKENS // BLK
LO_COLS = 16
BF16_ROWS = 16
GW_ROWS = 80
CW_ROWS = 64
E_HALF = N_EXPERTS // 2
VMEM_LIMIT = 56 * 1024 * 1024


def _cparams(sem):
    return pltpu.CompilerParams(dimension_semantics=sem, vmem_limit_bytes=VMEM_LIMIT)


def _cond_of_tile(i):
    return jnp.where(i < CTX_TILES, 0, 1 + (i - CTX_TILES) // LAT_TILES_PER_REQ)


def _silu(v):
    h = 0.5 * v
    return h + h * jnp.tanh(h)


def _softplus(v):
    return jnp.maximum(v, 0.0) + jnp.log1p(jnp.exp(-jnp.abs(v)))


def _norm_mod(x, nw, scale, shift):
    y = x * lax.rsqrt(jnp.mean(x * x, axis=-1, keepdims=True) + EPS)
    return y * nw * (1.0 + scale) + shift


def _conv3_rows(p, w_ref, tile_idx):
    n = p.shape[0]
    period = jnp.where(tile_idx < CTX_TILES, L_CTX, GRID_W)
    r = lax.broadcasted_iota(jnp.int32, (n, 1), 0) & (period - 1)
    prev = jnp.where(r == 0, 0.0, pltpu.roll(p, 1, axis=0))
    nxt = jnp.where(r == period - 1, 0.0, pltpu.roll(p, n - 1, axis=0))
    return prev * w_ref[0:1, :] + p * w_ref[1:2, :] + nxt * w_ref[2:3, :]


def _split_pieces(v):
    pieces = []
    r = v
    for _ in range(N_SPLIT):
        p = r.astype(BF16)
        pieces.append(p)
        r = r - p.astype(F32)
    return pieces


def _modulation_kernel(condT_ref, w_ref, b_ref, o_ref):
    s = _silu(condT_ref[...])
    w = w_ref[...]
    rows = [jnp.sum(w * s[:, r:r + 1], axis=0, keepdims=True) + b_ref[...] for r in range(N_COND)]
    rows.append(jnp.zeros((8 - N_COND, w.shape[1]), F32))
    o_ref[...] = jnp.concatenate(rows, axis=0)


def _modulation(condT, w_mod, b_mod):
    tn = 1536
    n = MOD_CHUNKS * D
    return pl.pallas_call(
        _modulation_kernel,
        out_shape=jax.ShapeDtypeStruct((DEPTH, 8, n), F32),
        grid=(DEPTH, n // tn),
        in_specs=[
            pl.BlockSpec((D, 8), lambda l, j: (0, 0)),
            pl.BlockSpec((None, D, tn), lambda l, j: (l, 0, j)),
            pl.BlockSpec((None, 1, tn), lambda l, j: (l, 0, j)),
        ],
        out_specs=pl.BlockSpec((None, 8, tn), lambda l, j: (l, 0, j)),
        compiler_params=_cparams(("arbitrary", "arbitrary")),
        name="modulation",
    )(condT, w_mod, b_mod.reshape(DEPTH, 1, n))


def _conv_mixer_kernel(*refs, split_input):
    i = pl.program_id(0)
    if split_input:
        xc_ref, xl_ref, mod_ref, nw_ref, win_ref, cw_ref, wout_ref, o_ref = refs
        x = jnp.where(i < CTX_TILES, xc_ref[...], xl_ref[...])
    else:
        x_ref, mod_ref, nw_ref, win_ref, cw_ref, wout_ref, o_ref = refs
        x = x_ref[...]
    hn = _norm_mod(x, nw_ref[...], mod_ref[1:2, :], mod_ref[0:1, :]).astype(BF16)
    gb = jnp.dot(hn, win_ref[:, 0:D], preferred_element_type=F32)
    gc = jnp.dot(hn, win_ref[:, D:2 * D], preferred_element_type=F32)
    v = jnp.dot(hn, win_ref[:, 2 * D:3 * D], preferred_element_type=F32)
    q = (gb * _conv3_rows(gc * v, cw_ref, i)).astype(BF16)
    mix = jnp.dot(q, wout_ref[...], preferred_element_type=F32)
    o_ref[...] = x + mod_ref[2:3, :] * mix


def _conv_mixer(xs, mod_l, nw, w_in, cw, w_out, j):
    split_input = isinstance(xs, tuple)
    if split_input:
        x_specs = [pl.BlockSpec((TM, D), lambda i: (jnp.minimum(i, CTX_TILES - 1), 0)),
                   pl.BlockSpec((TM, D), lambda i: (jnp.maximum(i - CTX_TILES, 0), 0))]
    else:
        xs = (xs,)
        x_specs = [pl.BlockSpec((TM, D), lambda i: (i, 0))]
    return pl.pallas_call(
        functools.partial(_conv_mixer_kernel, split_input=split_input),
        out_shape=jax.ShapeDtypeStruct((R, D), F32),
        grid=(R // TM,),
        in_specs=x_specs + [
            pl.BlockSpec((None, MOD_CHUNKS, D), lambda i: (_cond_of_tile(i), 0, 0)),
            pl.BlockSpec((1, D), lambda i: (0, 0)),
            pl.BlockSpec((None, D, 3 * D), lambda i: (j, 0, 0)),
            pl.BlockSpec((None, 3, D), lambda i: (j, 0, 0)),
            pl.BlockSpec((None, D, D), lambda i: (j, 0, 0)),
        ],
        out_specs=pl.BlockSpec((TM, D), lambda i: (i, 0)),
        compiler_params=_cparams(("arbitrary",)),
        name="conv_mixer",
    )(*xs, mod_l, nw, w_in, cw, w_out)


def _ssd_in_kernel(x_ref, mod_ref, nw_ref, win_ref, wdt_ref, wdtT_ref, cw_ref, cb_ref, dtb_ref, dtbT_ref,
                   z_ref, xs_ref, bm_ref, cm_ref, dt_ref, dtT_ref):
    i = pl.program_id(0)
    hn = _norm_mod(x_ref[...], nw_ref[...], mod_ref[1:2, :], mod_ref[0:1, :]).astype(BF16)
    for k in range(D_INNER // D):
        z_ref[:, k * D:(k + 1) * D] = jnp.dot(hn, win_ref[:, k * D:(k + 1) * D], preferred_element_type=F32)
    for k in range(CONV_DIM // D):
        lo = D_INNER + k * D
        u = jnp.dot(hn, win_ref[:, lo:lo + D], preferred_element_type=F32)
        u = _silu(_conv3_rows(u, cw_ref.at[:, k * D:(k + 1) * D], i) + cb_ref[:, k * D:(k + 1) * D])
        if k < D_INNER // D:
            xs_ref[:, k * D:(k + 1) * D] = u
        else:
            bm_ref[...] = u[:, :GROUPS * STATE]
            cm_ref[...] = u[:, GROUPS * STATE:]
    dt = _softplus(jnp.dot(hn, wdt_ref[...], preferred_element_type=F32) + dtb_ref[...])
    dtT = _softplus(
        lax.dot_general(wdtT_ref[...], hn, (((1,), (1,)), ((), ())), preferred_element_type=F32) + dtbT_ref[...])
    for g in range(GROUPS):
        dt_ref[g] = dt[:, g * DT_LANES:(g + 1) * DT_LANES].reshape(TM // CHUNK, CHUNK, DT_LANES)
        for k in range(TM // CHUNK):
            dtT_ref[g, k] = dtT[g * 2 * HPG:(g + 1) * 2 * HPG, k * CHUNK:(k + 1) * CHUNK]


def _ssd_in(x, mod_l, nw, w_in, w_dt, w_dtT, cw, cb, dtb, dtbT, j):
    n_in = w_in.shape[2]
    nck = TM // CHUNK
    return pl.pallas_call(
        _ssd_in_kernel,
        out_shape=(
            jax.ShapeDtypeStruct((R, D_INNER), F32),
            jax.ShapeDtypeStruct((R, D_INNER), F32),
            jax.ShapeDtypeStruct((R, GROUPS * STATE), F32),
            jax.ShapeDtypeStruct((R, GROUPS * STATE), F32),
            jax.ShapeDtypeStruct((GROUPS, R // CHUNK, CHUNK, DT_LANES), F32),
            jax.ShapeDtypeStruct((GROUPS, R // CHUNK, 2 * HPG, CHUNK), F32),
        ),
        grid=(R // TM,),
        in_specs=[
            pl.BlockSpec((TM, D), lambda i: (i, 0)),
            pl.BlockSpec((None, MOD_CHUNKS, D), lambda i: (_cond_of_tile(i), 0, 0)),
            pl.BlockSpec((1, D), lambda i: (0, 0)),
            pl.BlockSpec((None, D, n_in), lambda i: (j, 0, 0)),
            pl.BlockSpec((D, GROUPS * DT_LANES), lambda i: (0, 0)),
            pl.BlockSpec((2 * HEADS, D), lambda i: (0, 0)),
            pl.BlockSpec((None, 3, CONV_DIM), lambda i: (j, 0, 0)),
            pl.BlockSpec((None, 1, CONV_DIM), lambda i: (j, 0, 0)),
            pl.BlockSpec((1, GROUPS * DT_LANES), lambda i: (0, 0)),
            pl.BlockSpec((2 * HEADS, 1), lambda i: (0, 0)),
        ],
        out_specs=(
            pl.BlockSpec((TM, D_INNER), lambda i: (i, 0)),
            pl.BlockSpec((TM, D_INNER), lambda i: (i, 0)),
            pl.BlockSpec((TM, GROUPS * STATE), lambda i: (i, 0)),
            pl.BlockSpec((TM, GROUPS * STATE), lambda i: (i, 0)),
            pl.BlockSpec((GROUPS, nck, CHUNK, DT_LANES), lambda i: (0, i, 0, 0)),
            pl.BlockSpec((GROUPS, nck, 2 * HPG, CHUNK), lambda i: (0, i, 0, 0)),
        ),
        compiler_params=_cparams(("arbitrary",)),
        name="ssd_in",
    )(x, mod_l, nw, w_in, w_dt, w_dtT, cw, cb, dtb, dtbT)


def _ssd_scan_kernel(*refs, n_chunks, has_h0, n_kept, state_slots):
    emit_state = state_slots is not None
    xs_ref, bm_ref, cm_ref, dt_ref, dtT_ref, alog_ref, alogT_ref, dsk_ref = refs[:8]
    tri_k_ref, tri_r_ref, expf_ref, expb_ref, colsel_ref = refs[8:13]
    k = 13
    h0_ref = None
    if has_h0:
        h0_ref = refs[k]
        k += 1
    k += n_kept
    y_ref = refs[k]
    k += 1
    st_out_ref = None
    if emit_state:
        st_out_ref = refs[k]
        k += 1
    st_ref = refs[k]

    a_row = -jnp.exp(alog_ref[...]) * LOG2E
    a_col = -jnp.exp(alogT_ref[...]) * LOG2E
    row_is_fwd = lax.broadcasted_iota(jnp.int32, (2 * HPG, 1), 0) < HPG
    qi = lax.broadcasted_iota(jnp.int32, (CHUNK, CHUNK), 0)
    si = lax.broadcasted_iota(jnp.int32, (CHUNK, CHUNK), 1)
    lower = si <= qi
    lane = lax.broadcasted_iota(jnp.int32, (1, DT_LANES), 1)
    piece_of_lane = lane // (2 * HPG)
    lane_is_fwd = lane % (2 * HPG) < HPG
    lane_c = lax.broadcasted_iota(jnp.int32, (1, CHUNK), 1)
    first_half = lane_c < HEADDIM

    def lane_pieces(v):
        pieces = _split_pieces(v)
        out = pieces[N_SPLIT - 1]
        for r in range(N_SPLIT - 2, -1, -1):
            out = jnp.where(piece_of_lane == r, pieces[r], out)
        return out

    def expand_many(vs, e):
        out = jnp.dot(jnp.concatenate([lane_pieces(v) for v in vs], axis=0), e, preferred_element_type=F32)
        res, r0 = [], 0
        for v in vs:
            res.append(out[r0:r0 + v.shape[0], :])
            r0 += v.shape[0]
        return res

    for d in range(2):
        if has_h0:
            st_ref[d] = h0_ref[d].T
        else:
            st_ref[d] = jnp.zeros((STATE, GW), F32)

    y_ref[...] = jnp.zeros(y_ref.shape, F32)

    def body(k_, carry):
        cf = k_
        x = xs_ref[cf]
        b = bm_ref[cf]
        cm = cm_ref[cf]
        dt = dt_ref[cf]
        dtT = dtT_ref[cf]
        cr = n_chunks - 1 - k_
        xr = xs_ref[cr]
        br = bm_ref[cr]
        dtr = dt_ref[cr]

        dta_rows = jnp.concatenate(_split_pieces(dt * a_row), axis=0)
        dtaT_lanes = jnp.concatenate(_split_pieces(dtT * a_col), axis=1)
        acs2 = jnp.dot(tri_k_ref[...], dta_rows, preferred_element_type=F32)
        acs_lo, acs_up = acs2[:CHUNK, :], acs2[CHUNK:, :]
        acs_t2 = jnp.dot(dtaT_lanes, tri_r_ref[...], preferred_element_type=F32)
        acs_up_r = jnp.dot(tri_k_ref[CHUNK:, :], jnp.concatenate(_split_pieces(dtr * a_row), axis=0),
                           preferred_element_type=F32)
        r_t = jnp.log(dtT) * LOG2E - jnp.where(row_is_fwd, acs_t2[:, :CHUNK], acs_t2[:, CHUNK:])
        col_b = jnp.dot(lane_pieces(jnp.where(lane_is_fwd, acs_lo, acs_up)), colsel_ref[...],
                        preferred_element_type=F32)

        a_last = acs_lo[CHUNK - 1:CHUNK, :]
        a_tot = acs_up_r[0:1, :]
        cb_diag = jnp.sum(cm * b, axis=1, keepdims=True)
        e_out_f, e_in_f = expand_many([jnp.exp2(acs_lo), jnp.exp2(a_last - acs_lo) * dt], expf_ref[...])
        e_self, e_out_b, e_in_b = expand_many(
            [dsk_ref[...] + cb_diag * dt, jnp.exp2(acs_up_r), jnp.exp2(a_tot - acs_up_r) * dtr], expb_ref[...])
        e_keep_f = e_out_f[CHUNK - 1:CHUNK, :]
        e_keep_b = e_out_b[0:1, :]

        cmb = cm.astype(BF16)
        cb = lax.dot_general(cmb, b.astype(BF16), (((1,), (1,)), ((), ())), preferred_element_type=F32)
        parts = []
        for hp in range(HPG // 2):
            ws = []
            for h in (2 * hp, 2 * hp + 1):
                arg = jnp.where(lower, col_b[:, h * CHUNK:(h + 1) * CHUNK] + r_t[h:h + 1, :],
                                col_b[:, (HPG + h) * CHUNK:(HPG + h + 1) * CHUNK] + r_t[HPG + h:HPG + h + 1, :])
                ws.append((cb * jnp.exp2(arg)).astype(BF16))
            xp = x[:, hp * CHUNK:(hp + 1) * CHUNK]
            x2 = jnp.concatenate([jnp.where(first_half, xp, 0.0), jnp.where(first_half, 0.0, xp)], axis=0)
            parts.append(jnp.dot(jnp.concatenate(ws, axis=1), x2.astype(BF16), preferred_element_type=F32))
        st_f = st_ref[0]
        y = (jnp.concatenate(parts, axis=1) + e_self * x
             + jnp.dot(cmb, st_f.astype(BF16), preferred_element_type=F32) * e_out_f)
        y_ref[cf] = y_ref[cf] + y
        st_ref[0] = st_f * e_keep_f + jnp.dot(
            b.T.astype(BF16), (x * e_in_f).astype(BF16), preferred_element_type=F32)

        st_b = st_ref[1]
        y_ref[cr] = y_ref[cr] + jnp.dot(cm_ref[cr].astype(BF16), st_b.astype(BF16),
                                        preferred_element_type=F32) * e_out_b
        st_ref[1] = st_b * e_keep_b + jnp.dot(
            br.T.astype(BF16), (xr * e_in_b).astype(BF16), preferred_element_type=F32)
        return carry

    lax.fori_loop(0, n_chunks, body, 0)
    if emit_state:
        for slot, own in enumerate(state_slots):
            for d in range(2):
                st_out_ref[slot, d] = st_ref[d].T if own else jnp.zeros((GW, STATE), F32)


def _scan_constants():
    q = np.arange(CHUNK)
    tri_lo = (q[None, :] <= q[:, None]).astype(np.float32)
    tri_up = tri_lo.T
    tri_k = np.concatenate([np.tile(tri_lo, (1, N_SPLIT)), np.tile(tri_up, (1, N_SPLIT))], axis=0)
    tri_r = np.concatenate([np.tile(tri_up, (N_SPLIT, 1)), np.tile(tri_lo, (N_SPLIT, 1))], axis=1)
    head = np.arange(DT_LANES) % (2 * HPG)
    chan_head = np.arange(GW) // HEADDIM
    exp_f = (head[:, None] == chan_head[None, :]).astype(np.float32)
    exp_b = (head[:, None] == chan_head[None, :] + HPG).astype(np.float32)
    col_sel = (head[:, None] == (np.arange(2 * HPG * CHUNK) // CHUNK)[None, :]).astype(np.float32)
    return [jnp.asarray(m, BF16) for m in (tri_k, tri_r, exp_f, exp_b, col_sel)]


def _ssd_scan(xs3, bm3, cm3, dt4, dtT4, alog4, alogT4, dskx, h0, st_prev, *, n_seq, seq_len, row_block0, state_layer):
    nck = seq_len // CHUNK
    has_h0 = h0 is not None
    emit_state = state_layer is not None
    keep_state = st_prev is not None
    rb = lambda s: s + row_block0
    in_specs = [
        pl.BlockSpec((nck, CHUNK, GW), lambda s, g: (rb(s), 0, g)),
        pl.BlockSpec((nck, CHUNK, STATE), lambda s, g: (rb(s), 0, g)),
        pl.BlockSpec((nck, CHUNK, STATE), lambda s, g: (rb(s), 0, g)),
        pl.BlockSpec((None, nck, CHUNK, DT_LANES), lambda s, g: (g, rb(s), 0, 0)),
        pl.BlockSpec((None, nck, 2 * HPG, CHUNK), lambda s, g: (g, rb(s), 0, 0)),
        pl.BlockSpec((None, 1, DT_LANES), lambda s, g: (g, 0, 0)),
        pl.BlockSpec((None, 2 * HPG, 1), lambda s, g: (g, 0, 0)),
        pl.BlockSpec((None, 1, DT_LANES), lambda s, g: (g, 0, 0)),
    ]
    args = [xs3, bm3, cm3, dt4, dtT4, alog4, alogT4, dskx]
    for const in _scan_constants():
        in_specs.append(pl.BlockSpec(const.shape, lambda s, g: (0, 0)))
        args.append(const)
    if has_h0:
        in_specs.append(pl.BlockSpec((None, 2, GW, STATE), lambda s, g: (s, 0, g, 0)))
        args.append(h0)
    aliases = {}
    if keep_state:
        in_specs.append(pl.BlockSpec(memory_space=pl.ANY))
        aliases[len(args)] = 1
        args.append(st_prev)
    y_shape = jax.ShapeDtypeStruct((n_seq * nck, CHUNK, D_INNER), F32)
    y_spec = pl.BlockSpec((nck, CHUNK, GW), lambda s, g: (s, 0, g))
    n_layers = DEPTH // 2
    state_slots = None
    if emit_state:
        out_shape = (y_shape, jax.ShapeDtypeStruct((n_seq, n_layers, 2, HEADS * HEADDIM, STATE), F32))
        if keep_state:
            state_slots = (True,)
            st_spec = pl.BlockSpec((None, 1, 2, GW, STATE), lambda s, g: (s, state_layer, 0, g, 0))
        else:
            state_slots = tuple(l == state_layer for l in range(n_layers))
            st_spec = pl.BlockSpec((None, n_layers, 2, GW, STATE), lambda s, g: (s, 0, 0, g, 0))
        out_specs = (y_spec, st_spec)
    else:
        out_shape, out_specs = y_shape, y_spec
    return pl.pallas_call(
        functools.partial(_ssd_scan_kernel, n_chunks=nck, has_h0=has_h0, n_kept=len(aliases),
                          state_slots=state_slots),
        out_shape=out_shape,
        grid=(n_seq, GROUPS),
        in_specs=in_specs,
        out_specs=out_specs,
        scratch_shapes=[pltpu.VMEM((2, STATE, GW), F32)],
        input_output_aliases=aliases,
        compiler_params=_cparams(("arbitrary", "arbitrary")),
        name="ssd_scan",
    )(*args)


def _ssd_out_kernel(x_ref, yc_ref, yl_ref, z_ref, mod_ref, nw_ref, wout_ref, o_ref):
    y = jnp.where(pl.program_id(0) < R_CTX // TM_OUT, yc_ref[...], yl_ref[...])
    v = y * _silu(z_ref[...])
    v = v * lax.rsqrt(jnp.mean(v * v, axis=-1, keepdims=True) + EPS) * nw_ref[...]
    mix = jnp.dot(v.astype(BF16), wout_ref[...], preferred_element_type=F32)
    o_ref[...] = x_ref[...] + mod_ref[2:3, :] * mix


def _ssd_out(x, y_ctx, y_lat, z, mod_l, nw, w_out, j):
    n_ctx = R_CTX // TM_OUT
    return pl.pallas_call(
        _ssd_out_kernel,
        out_shape=jax.ShapeDtypeStruct((R, D), F32),
        grid=(R // TM_OUT,),
        in_specs=[
            pl.BlockSpec((TM_OUT, D), lambda i: (i, 0)),
            pl.BlockSpec((TM_OUT, D_INNER), lambda i: (jnp.minimum(i, n_ctx - 1), 0)),
            pl.BlockSpec((TM_OUT, D_INNER), lambda i: (jnp.maximum(i - n_ctx, 0), 0)),
            pl.BlockSpec((TM_OUT, D_INNER), lambda i: (i, 0)),
            pl.BlockSpec((None, MOD_CHUNKS, D), lambda i: (_cond_of_tile(i * TM_OUT // TM), 0, 0)),
            pl.BlockSpec((None, 1, D_INNER), lambda i: (j, 0, 0)),
            pl.BlockSpec((None, D_INNER, D), lambda i: (j, 0, 0)),
        ],
        out_specs=pl.BlockSpec((TM_OUT, D), lambda i: (i, 0)),
        compiler_params=_cparams(("arbitrary",)),
        name="ssd_out",
    )(x, y_ctx, y_lat, z, mod_l, nw, w_out)


def _router_kernel(x_ref, mod_ref, nw_ref, wrT_ref, hn_ref, affT_ref):
    hn = _norm_mod(x_ref[...], nw_ref[...], mod_ref[4:5, :], mod_ref[3:4, :])
    hn_hi = hn.astype(BF16)
    hn_ref[...] = hn_hi
    hn_lo = (hn - hn_hi.astype(F32)).astype(BF16)
    w = wrT_ref[...]
    w_hi = w.astype(BF16)
    w_lo = (w - w_hi.astype(F32)).astype(BF16)
    logits = lax.dot_general(jnp.concatenate([w_hi, w_lo, w_hi], axis=1), jnp.concatenate([hn_hi, hn_hi, hn_lo], axis=1),
                             (((1,), (1,)), ((), ())), preferred_element_type=F32)
    e = jnp.exp(logits - jnp.max(logits, axis=0, keepdims=True))
    affT_ref[...] = e / jnp.sum(e, axis=0, keepdims=True)


def _router(x, mod_l, nw, wrT):
    return pl.pallas_call(
        _router_kernel,
        out_shape=(jax.ShapeDtypeStruct((R, D), BF16), jax.ShapeDtypeStruct((N_EXPERTS, R), F32)),
        grid=(R // TM,),
        in_specs=[
            pl.BlockSpec((TM, D), lambda i: (i, 0)),
            pl.BlockSpec((None, MOD_CHUNKS, D), lambda i: (_cond_of_tile(i), 0, 0)),
            pl.BlockSpec((1, D), lambda i: (0, 0)),
            pl.BlockSpec((N_EXPERTS, D), lambda i: (0, 0)),
        ],
        out_specs=(pl.BlockSpec((TM, D), lambda i: (i, 0)), pl.BlockSpec((N_EXPERTS, TM), lambda i: (0, i))),
        compiler_params=_cparams(("arbitrary",)),
        name="router",
    )(x, mod_l, nw, wrT)


def _excl_cumsum_lanes(m):
    blk = 256
    t = m.shape[1]
    a = lax.broadcasted_iota(jnp.int32, (blk, blk), 0)
    b = lax.broadcasted_iota(jnp.int32, (blk, blk), 1)
    strict = jnp.where(a < b, 1.0, 0.0).astype(BF16)
    carry = jnp.zeros((m.shape[0], 1), F32)
    outs = []
    for k in range(t // blk):
        mk = m[:, k * blk:(k + 1) * blk]
        outs.append(jnp.dot(mk.astype(BF16), strict, preferred_element_type=F32) + carry)
        carry = carry + jnp.sum(mk, axis=1, keepdims=True)
    return outs[0] if len(outs) == 1 else jnp.concatenate(outs, axis=1)


def _select_request(aff, cap, base):
    thr_bits = jnp.zeros((N_EXPERTS, 1), jnp.int32)
    for k in range(30, -1, -1):
        trial = thr_bits | (1 << k)
        cnt = jnp.sum(jnp.where(aff >= pltpu.bitcast(trial, F32), 1.0, 0.0), axis=1, keepdims=True)
        thr_bits = jnp.where(cnt >= cap, trial, thr_bits)
    thr = pltpu.bitcast(thr_bits, F32)
    gt = aff > thr
    eq = jnp.where(aff == thr, 1.0, 0.0)
    need = cap - jnp.sum(jnp.where(gt, 1.0, 0.0), axis=1, keepdims=True)
    sel = gt | ((eq > 0.0) & (_excl_cumsum_lanes(eq) < need))
    pos = _excl_cumsum_lanes(jnp.where(sel, 1.0, 0.0))
    return jnp.where(sel, pos + base, -1.0), jnp.where(sel, aff, 0.0)


def _select_kernel(affT_ref, slotT_ref, slot_tok_ref, gate_tok_ref, lo_ref):
    s = pl.program_id(0)

    def emit(slot, gate):
        slotT_ref[...] = slot.astype(jnp.int32)
        pad = jnp.zeros((128 - N_EXPERTS, SR_TOKENS), F32)
        slot_tok_ref[...] = jnp.concatenate([slot, pad - 1.0], axis=0).T.astype(jnp.int32)
        gate_tok_ref[...] = jnp.concatenate([gate, pad], axis=0).T
        t = lax.broadcasted_iota(jnp.int32, (SR_TOKENS, 128), 0)
        jcol = lax.broadcasted_iota(jnp.int32, (SR_TOKENS, 128), 1)
        before = jnp.where(t < jcol * BLK, 1.0, 0.0).astype(BF16)
        chosen = jnp.where(slot >= 0.0, 1.0, 0.0).astype(BF16)
        lo_ref[...] = jnp.dot(chosen, before, preferred_element_type=F32).astype(jnp.int32)

    @pl.when(s < N_SR_CTX)
    def _():
        outs = [_select_request(affT_ref[:, r * L_CTX:(r + 1) * L_CTX], CAP_CTX, float(r * CAP_CTX))
                for r in range(CTX_PER_SR)]
        emit(jnp.concatenate([o[0] for o in outs], axis=1), jnp.concatenate([o[1] for o in outs], axis=1))

    @pl.when(s >= N_SR_CTX)
    def _():
        emit(*_select_request(affT_ref[...], CAP_LAT, 0.0))


def _select(affT):
    return pl.pallas_call(
        _select_kernel,
        out_shape=(
            jax.ShapeDtypeStruct((N_EXPERTS, R), jnp.int32),
            jax.ShapeDtypeStruct((R, 128), jnp.int32),
            jax.ShapeDtypeStruct((R, 128), F32),
            jax.ShapeDtypeStruct((N_SR, N_EXPERTS, 128), jnp.int32),
        ),
        grid=(N_SR,),
        in_specs=[pl.BlockSpec((N_EXPERTS, SR_TOKENS), lambda s: (0, s))],
        out_specs=(
            pl.BlockSpec((N_EXPERTS, SR_TOKENS), lambda s: (0, s)),
            pl.BlockSpec((SR_TOKENS, 128), lambda s: (s, 0)),
            pl.BlockSpec((SR_TOKENS, 128), lambda s: (s, 0)),
            pl.BlockSpec((None, N_EXPERTS, 128), lambda s: (s, 0, 0)),
        ),
        compiler_params=_cparams(("arbitrary",)),
        name="select",
    )(affT)


def _lo_at(lo_ref, sr, e, j):
    return lo_ref[(sr * N_EXPERTS + e) * LO_COLS + j]


def _gather_kernel(lo_ref, hn_ref, slotT_ref, xe_ref, acc_ref):
    sr = pl.program_id(0)
    e0 = pl.program_id(1) * E_HALF
    starts = {}
    fits = None
    for e in range(E_HALF):
        for j in range(N_BLK):
            start = (_lo_at(lo_ref, sr, e0 + e, j) // BF16_ROWS) * BF16_ROWS
            ok = _lo_at(lo_ref, sr, e0 + e, j + 1) - start <= GW_ROWS
            fits = ok if fits is None else jnp.logical_and(fits, ok)
            starts[e, j] = start

    @pl.when(fits)
    def _():
        acc_ref[...] = jnp.zeros(acc_ref.shape, BF16)
        w_iota = lax.broadcasted_iota(jnp.int32, (GW_ROWS, BLK), 0)
        for j in range(N_BLK):
            rows = [jnp.where(w_iota == slotT_ref[e:e + 1, j * BLK:(j + 1) * BLK] - starts[e, j], 1.0, 0.0).astype(BF16)
                    for e in range(E_HALF)]
            part = jnp.dot(jnp.concatenate(rows, axis=0), hn_ref[j * BLK:(j + 1) * BLK, :],
                           preferred_element_type=F32).astype(BF16)
            for e in range(E_HALF):
                win = pl.ds(pl.multiple_of(starts[e, j], BF16_ROWS), GW_ROWS)
                acc_ref[e, win, :] = acc_ref[e, win, :] + part[e * GW_ROWS:(e + 1) * GW_ROWS, :]
        xe_ref[...] = acc_ref[:, :SR_SLOTS, :]

    @pl.when(jnp.logical_not(fits))
    def _():
        s_iota = lax.broadcasted_iota(jnp.int32, (SR_SLOTS, SR_TOKENS), 0)
        for e in range(E_HALF):
            onehot = jnp.where(s_iota == slotT_ref[e:e + 1, :], 1.0, 0.0).astype(BF16)
            xe_ref[e] = jnp.dot(onehot, hn_ref[...], preferred_element_type=F32).astype(BF16)


def _gather(lo_flat, hn, slotT):
    return pl.pallas_call(
        _gather_kernel,
        out_shape=jax.ShapeDtypeStruct((N_EXPERTS, N_SR * SR_SLOTS, D), BF16),
        grid_spec=pltpu.PrefetchScalarGridSpec(
            num_scalar_prefetch=1,
            grid=(N_SR, N_EXPERTS // E_HALF),
            in_specs=[
                pl.BlockSpec((SR_TOKENS, D), lambda s, h, lo: (s, 0)),
                pl.BlockSpec((E_HALF, SR_TOKENS), lambda s, h, lo: (h, s)),
            ],
            out_specs=pl.BlockSpec((E_HALF, SR_SLOTS, D), lambda s, h, lo: (h, s, 0)),
            scratch_shapes=[pltpu.VMEM((E_HALF, SR_SLOTS + GW_ROWS, D), BF16)],
        ),
        compiler_params=_cparams(("arbitrary", "arbitrary")),
        name="gather",
    )(lo_flat, hn, slotT)


def _ffn_kernel(xe_ref, wg_ref, wu_ref, wd_ref, ye_ref):
    xe = xe_ref[...]
    hg = jnp.dot(xe, wg_ref[...].astype(BF16), preferred_element_type=F32)
    hu = jnp.dot(xe, wu_ref[...].astype(BF16), preferred_element_type=F32)
    hid = (_silu(hg) * hu).astype(BF16)
    ye_ref[...] = jnp.dot(hid, wd_ref[...].astype(BF16), preferred_element_type=F32).astype(BF16)


def _ffn(xe, wg, wu, wd, layer):
    m = xe.shape[1]
    w_spec = pl.BlockSpec((None, None, D, D), lambda e: (layer, e, 0, 0))
    return pl.pallas_call(
        _ffn_kernel,
        out_shape=jax.ShapeDtypeStruct((N_EXPERTS, m, D), BF16),
        grid=(N_EXPERTS,),
        in_specs=[pl.BlockSpec((None, m, D), lambda e: (e, 0, 0)), w_spec, w_spec, w_spec],
        out_specs=pl.BlockSpec((None, m, D), lambda e: (e, 0, 0)),
        compiler_params=_cparams(("arbitrary",)),
        name="expert_ffn",
    )(xe, wg, wu, wd)


def _combine_kernel(lo_ref, x_ref, ye_ref, slot_ref, gate_ref, mod_ref, fnw_ref, *rest, final):
    if final:
        oc_ref, ol_ref, yw_ref, acc_ref = rest
    else:
        o_ref, yw_ref, acc_ref = rest
    sr = pl.program_id(0)
    j = pl.program_id(1)
    starts = []
    fits = None
    for e in range(N_EXPERTS):
        start = jnp.minimum((_lo_at(lo_ref, sr, e, j) // BF16_ROWS) * BF16_ROWS, SR_SLOTS - CW_ROWS)
        ok = _lo_at(lo_ref, sr, e, j + 1) - start <= CW_ROWS
        fits = ok if fits is None else jnp.logical_and(fits, ok)
        starts.append(start)
    slot = slot_ref[...]
    gate = gate_ref[...]

    @pl.when(fits)
    def _():
        for e in range(N_EXPERTS):
            yw_ref[e * CW_ROWS:(e + 1) * CW_ROWS, :] = ye_ref[e, pl.ds(pl.multiple_of(starts[e], BF16_ROWS), CW_ROWS), :]
        n_k = N_EXPERTS * CW_ROWS
        k_exp = lax.broadcasted_iota(jnp.int32, (1, n_k), 1) // CW_ROWS
        k_row = (lax.broadcasted_iota(jnp.int32, (1, n_k), 1) % CW_ROWS).astype(F32)
        start_k = jnp.zeros((1, n_k), F32)
        for e in range(N_EXPERTS):
            start_k = jnp.where(k_exp == e, starts[e].astype(F32), start_k)
        spread = jnp.where(lax.broadcasted_iota(jnp.int32, (128, n_k), 0) == k_exp, 1.0, 0.0).astype(BF16)
        slot_k = jnp.dot(slot.astype(F32).astype(BF16), spread, preferred_element_type=F32)
        hit = slot_k - start_k == k_row
        gate_k = jnp.dot(gate.astype(BF16), spread, preferred_element_type=F32)
        acc_ref[...] = jnp.dot(jnp.where(hit, gate_k, 0.0).astype(BF16), yw_ref[...], preferred_element_type=F32)

    @pl.when(jnp.logical_not(fits))
    def _():
        s_iota = lax.broadcasted_iota(jnp.int32, (BLK, SR_SLOTS), 1)
        acc = jnp.zeros((BLK, D), F32)
        for e in range(N_EXPERTS):
            onehot = jnp.where(slot[:, e:e + 1] == s_iota, 1.0, 0.0).astype(BF16)
            acc = acc + gate[:, e:e + 1] * jnp.dot(onehot, ye_ref[e], preferred_element_type=F32)
        acc_ref[...] = acc

    out = x_ref[...] + mod_ref[5:6, :] * acc_ref[...]
    if not final:
        o_ref[...] = out
    else:
        out = out * lax.rsqrt(jnp.mean(out * out, axis=-1, keepdims=True) + EPS) * fnw_ref[...]
        ol_ref[...] = out

        @pl.when(sr < N_SR_CTX)
        def _():
            oc_ref[...] = out


def _combine(lo_flat, x, ye, slot_tok, gate_tok, mod_l, fnw, *, final):
    blk_of = lambda s, j: s * N_BLK + j
    n_ctx_blk = R_CTX // BLK
    if final:
        out_shape = (jax.ShapeDtypeStruct((R_CTX, D), F32), jax.ShapeDtypeStruct((R_LAT, D), F32))
        out_specs = (pl.BlockSpec((BLK, D), lambda s, j, lo: (jnp.minimum(blk_of(s, j), n_ctx_blk - 1), 0)),
                     pl.BlockSpec((BLK, D), lambda s, j, lo: (jnp.maximum(blk_of(s, j) - n_ctx_blk, 0), 0)))
    else:
        out_shape = jax.ShapeDtypeStruct((R, D), F32)
        out_specs = pl.BlockSpec((BLK, D), lambda s, j, lo: (blk_of(s, j), 0))
    return pl.pallas_call(
        functools.partial(_combine_kernel, final=final),
        out_shape=out_shape,
        grid_spec=pltpu.PrefetchScalarGridSpec(
            num_scalar_prefetch=1,
            grid=(N_SR, N_BLK),
            in_specs=[
                pl.BlockSpec((BLK, D), lambda s, j, lo: (blk_of(s, j), 0)),
                pl.BlockSpec((N_EXPERTS, SR_SLOTS, D), lambda s, j, lo: (0, s, 0)),
                pl.BlockSpec((BLK, 128), lambda s, j, lo: (blk_of(s, j), 0)),
                pl.BlockSpec((BLK, 128), lambda s, j, lo: (blk_of(s, j), 0)),
                pl.BlockSpec((None, MOD_CHUNKS, D),
                             lambda s, j, lo: (_cond_of_tile(blk_of(s, j) * BLK // TM), 0, 0)),
                pl.BlockSpec((1, D), lambda s, j, lo: (0, 0)),
            ],
            out_specs=out_specs,
            scratch_shapes=[pltpu.VMEM((N_EXPERTS * CW_ROWS, D), BF16), pltpu.VMEM((BLK, D), F32)],
        ),
        compiler_params=_cparams(("arbitrary", "arbitrary")),
        name="combine",
    )(lo_flat, x, ye, slot_tok, gate_tok, mod_l, fnw)


def _moe(x, mod_l, nw, wrT, wg, wu, wd, fnw, layer, *, final):
    hn, affT = _router(x, mod_l, nw, wrT)
    slotT, slot_tok, gate_tok, lo = _select(affT)
    lo_flat = lo[:, :, :LO_COLS].reshape(-1)
    xe = _gather(lo_flat, hn, slotT)
    ye = _ffn(xe, wg, wu, wd, layer)
    return _combine(lo_flat, x, ye, slot_tok, gate_tok, mod_l, fnw, final=final)


def _group_major(p):
    return p.reshape(2, GROUPS, HPG).transpose(1, 0, 2).reshape(GROUPS, 2 * HPG)


def kernel(x_prompt, x_sample, state_ssm, c, c_ctx, norm1_w, norm2_w, w_mod, b_mod, conv_in_w, conv_w, conv_out_w, ssd_in_w, ssd_conv_w, ssd_conv_b, ssd_dt_bias, ssd_a_log, ssd_d, ssd_norm_w, ssd_out_w, router_w, exp_w_gate, exp_w_up, exp_w_down, final_norm_w):
    x = (x_prompt.reshape(R_CTX, D), x_sample.reshape(R_LAT, D))
    cond = jnp.concatenate([c_ctx[None, :], c, jnp.zeros((8 - N_COND, D), F32)], axis=0)
    mods = _modulation(cond.T, w_mod, b_mod)
    mods = mods[:, :N_COND].reshape(DEPTH, N_COND, MOD_CHUNKS, D)
    fnw = final_norm_w.reshape(1, D)
    conv_in_b, conv_out_b = conv_in_w.astype(BF16), conv_out_w.astype(BF16)
    ssd_in_b, ssd_out_b = ssd_in_w.astype(BF16), ssd_out_w.astype(BF16)
    ssd_conv_b3 = ssd_conv_b.reshape(DEPTH // 2, 1, CONV_DIM)
    ssd_norm_w3 = ssd_norm_w.reshape(DEPTH // 2, 1, D_INNER)

    states = None
    for layer in range(DEPTH):
        mod_l = mods[layer]
        j = layer // 2
        nw1 = norm1_w[layer].reshape(1, D)
        if layer % 2 == 0:
            x = _conv_mixer(x, mod_l, nw1, conv_in_b, conv_w, conv_out_b, j)
        else:
            w_dt = ssd_in_b[j, :, D_INNER + CONV_DIM:].reshape(D, 2, GROUPS, HPG).transpose(0, 2, 1, 3)
            w_dt_rep = jnp.broadcast_to(w_dt.reshape(D, GROUPS, 1, 2 * HPG), (D, GROUPS, N_SPLIT, 2 * HPG))
            dtb = _group_major(ssd_dt_bias[j])
            dtb_rep = jnp.tile(dtb, (1, N_SPLIT)).reshape(1, GROUPS * DT_LANES)
            z, xs, bm, cm, dt4, dtT4 = _ssd_in(
                x, mod_l, nw1, ssd_in_b, w_dt_rep.reshape(D, GROUPS * DT_LANES), w_dt.reshape(D, 2 * HEADS).T,
                ssd_conv_w, ssd_conv_b3, dtb_rep, dtb.reshape(2 * HEADS, 1), j)
            alog4 = _group_major(ssd_a_log[j])
            dsum = (ssd_d[j][0] + ssd_d[j][1]).reshape(GROUPS, HPG)
            dsk4 = jnp.tile(jnp.concatenate([jnp.zeros_like(dsum), dsum], axis=1), (1, N_SPLIT))
            scan_args = (xs.reshape(R // CHUNK, CHUNK, D_INNER), bm.reshape(R // CHUNK, CHUNK, GROUPS * STATE),
                         cm.reshape(R // CHUNK, CHUNK, GROUPS * STATE), dt4, dtT4,
                         jnp.tile(alog4, (1, N_SPLIT)).reshape(GROUPS, 1, DT_LANES),
                         alog4.reshape(GROUPS, 2 * HPG, 1), dsk4.reshape(GROUPS, 1, DT_LANES))
            y_ctx, states = _ssd_scan(*scan_args, None, states, n_seq=N_CTX, seq_len=L_CTX, row_block0=0,
                                      state_layer=j)
            h0 = state_ssm[:, j].reshape(N_LAT, 2, HEADS * HEADDIM, STATE)
            y_lat = _ssd_scan(*scan_args, h0, None, n_seq=N_LAT, seq_len=L_LAT, row_block0=R_CTX // L_LAT,
                              state_layer=None)
            x = _ssd_out(x, y_ctx.reshape(R_CTX, D_INNER), y_lat.reshape(R_LAT, D_INNER), z, mod_l,
                         ssd_norm_w3, ssd_out_b, j)
        x = _moe(x, mod_l, norm2_w[layer].reshape(1, D), router_w[layer].T, exp_w_gate, exp_w_up, exp_w_down, fnw,
                 layer, final=(layer == DEPTH - 1))

    y_prompt, y_sample = x
    return (y_prompt.reshape(N_CTX, L_CTX, D), y_sample.reshape(N_LAT, L_LAT, D),
            states.reshape(N_CTX, DEPTH // 2, 2, HEADS, HEADDIM, STATE))
```

```python
import functools

import jax
import jax.numpy as jnp
import numpy as np
from jax import lax
from jax.experimental import pallas as pl
from jax.experimental.pallas import tpu as pltpu

F32 = jnp.float32
BF16 = jnp.bfloat16
HIGHEST = lax.Precision.HIGHEST
LOG2E = 1.4426950408889634

D = 1024
DEPTH = 4
N_CTX, L_CTX = 16, 256
N_LAT, L_LAT = 2, 2048
GRID_W = 64
R_CTX = N_CTX * L_CTX
R_LAT = N_LAT * L_LAT
R = R_CTX + R_LAT
N_COND = 1 + N_LAT
MOD_CHUNKS = 6
D_INNER = 2048
HEADDIM = 64
HEADS = 32
GROUPS = 4
HPG = HEADS // GROUPS
GW = HPG * HEADDIM
STATE = 128
CHUNK = 128
CONV_DIM = D_INNER + 2 * GROUPS * STATE
N_SPLIT = 2
DT_LANES = N_SPLIT * 2 * HPG
N_EXPERTS = 16
EPS = 1e-6

TM = 512
TM_OUT = 512
CTX_TILES = R_CTX // TM
LAT_TILES_PER_REQ = L_LAT // TM
SR_TOKENS = 2048
N_SR = R // SR_TOKENS
N_SR_CTX = R_CTX // SR_TOKENS
SR_SLOTS = 2 * SR_TOKENS // N_EXPERTS
CTX_PER_SR = SR_TOKENS // L_CTX
CAP_CTX = 2 * L_CTX // N_EXPERTS
CAP_LAT = 2 * L_LAT // N_EXPERTS
BLK = 256
N_BLK = SR_TOKENS // BLK
LO_COLS = 16
BF16_ROWS = 16
GW_ROWS = 80
CW_ROWS = 64
E_HALF = N_EXPERTS // 2
VMEM_LIMIT = 56 * 1024 * 1024


def _cparams(sem):
    return pltpu.CompilerParams(dimension_semantics=sem, vmem_limit_bytes=VMEM_LIMIT)


def _cond_of_tile(i):
    return jnp.where(i < CTX_TILES, 0, 1 + (i - CTX_TILES) // LAT_TILES_PER_REQ)


def _silu(v):
    h = 0.5 * v
    return h + h * jnp.tanh(h)


def _softplus(v):
    return jnp.maximum(v, 0.0) + jnp.log1p(jnp.exp(-jnp.abs(v)))


def _norm_mod(x, nw, scale, shift):
    y = x * lax.rsqrt(jnp.mean(x * x, axis=-1, keepdims=True) + EPS)
    return y * (nw * (1.0 + scale)) + shift


def _conv3_rows(p, w_ref, tile_idx):
    n = p.shape[0]
    period = jnp.where(tile_idx < CTX_TILES, L_CTX, GRID_W)
    r = lax.broadcasted_iota(jnp.int32, (n, 1), 0) & (period - 1)
    prev = jnp.where(r == 0, 0.0, pltpu.roll(p, 1, axis=0))
    nxt = jnp.where(r == period - 1, 0.0, pltpu.roll(p, n - 1, axis=0))
    return prev * w_ref[0:1, :] + p * w_ref[1:2, :] + nxt * w_ref[2:3, :]


def _split_pieces(v):
    pieces = []
    r = v
    for _ in range(N_SPLIT):
        p = r.astype(BF16)
        pieces.append(p)
        r = r - p.astype(F32)
    return pieces


def _modulation_kernel(condT_ref, w_ref, b_ref, o_ref):
    s = _silu(condT_ref[...])
    w = w_ref[...]
    rows = [jnp.sum(w * s[:, r:r + 1], axis=0, keepdims=True) + b_ref[...] for r in range(N_COND)]
    rows.append(jnp.zeros((8 - N_COND, w.shape[1]), F32))
    o_ref[...] = jnp.concatenate(rows, axis=0)


def _modulation(condT, w_mod, b_mod):
    tn = 1536
    n = MOD_CHUNKS * D
    return pl.pallas_call(
        _modulation_kernel,
        out_shape=jax.ShapeDtypeStruct((DEPTH, 8, n), F32),
        grid=(DEPTH, n // tn),
        in_specs=[
            pl.BlockSpec((D, 8), lambda l, j: (0, 0)),
            pl.BlockSpec((None, D, tn), lambda l, j: (l, 0, j)),
            pl.BlockSpec((None, 1, tn), lambda l, j: (l, 0, j)),
        ],
        out_specs=pl.BlockSpec((None, 8, tn), lambda l, j: (l, 0, j)),
        compiler_params=_cparams(("arbitrary", "arbitrary")),
        name="modulation",
    )(condT, w_mod, b_mod.reshape(DEPTH, 1, n))


def _conv_mixer_kernel(*refs, split_input):
    i = pl.program_id(0)
    if split_input:
        xc_ref, xl_ref, mod_ref, nw_ref, win_ref, cw_ref, wout_ref, o_ref = refs
        x = jnp.where(i < CTX_TILES, xc_ref[...], xl_ref[...])
    else:
        x_ref, mod_ref, nw_ref, win_ref, cw_ref, wout_ref, o_ref = refs
        x = x_ref[...]
    hn = _norm_mod(x, nw_ref[...], mod_ref[1:2, :], mod_ref[0:1, :]).astype(BF16)
    gb = jnp.dot(hn, win_ref[:, 0:D], preferred_element_type=F32)
    gc = jnp.dot(hn, win_ref[:, D:2 * D], preferred_element_type=F32)
    v = jnp.dot(hn, win_ref[:, 2 * D:3 * D], preferred_element_type=F32)
    q = (gb * _conv3_rows(gc * v, cw_ref, i)).astype(BF16)
    mix = jnp.dot(q, wout_ref[...], preferred_element_type=F32)
    o_ref[...] = x + mod_ref[2:3, :] * mix


def _conv_mixer(xs, mod_l, nw, w_in, cw, w_out, j):
    split_input = isinstance(xs, tuple)
    if split_input:
        x_specs = [pl.BlockSpec((TM, D), lambda i: (jnp.minimum(i, CTX_TILES - 1), 0)),
                   pl.BlockSpec((TM, D), lambda i: (jnp.maximum(i - CTX_TILES, 0), 0))]
    else:
        xs = (xs,)
        x_specs = [pl.BlockSpec((TM, D), lambda i: (i, 0))]
    return pl.pallas_call(
        functools.partial(_conv_mixer_kernel, split_input=split_input),
        out_shape=jax.ShapeDtypeStruct((R, D), F32),
        grid=(R // TM,),
        in_specs=x_specs + [
            pl.BlockSpec((None, MOD_CHUNKS, D), lambda i: (_cond_of_tile(i), 0, 0)),
            pl.BlockSpec((1, D), lambda i: (0, 0)),
            pl.BlockSpec((None, D, 3 * D), lambda i: (j, 0, 0)),
            pl.BlockSpec((None, 3, D), lambda i: (j, 0, 0)),
            pl.BlockSpec((None, D, D), lambda i: (j, 0, 0)),
        ],
        out_specs=pl.BlockSpec((TM, D), lambda i: (i, 0)),
        compiler_params=_cparams(("arbitrary",)),
        name="conv_mixer",
    )(*xs, mod_l, nw, w_in, cw, w_out)


def _ssd_in_kernel(x_ref, mod_ref, nw_ref, win_ref, wdt_ref, wdtT_ref, cw_ref, cb_ref, dtb_ref, dtbT_ref,
                   z_ref, xs_ref, bm_ref, cm_ref, dt_ref, dtT_ref):
    i = pl.program_id(0)
    hn = _norm_mod(x_ref[...], nw_ref[...], mod_ref[1:2, :], mod_ref[0:1, :]).astype(BF16)
    for k in range(D_INNER // D):
        z_ref[:, k * D:(k + 1) * D] = jnp.dot(hn, win_ref[:, k * D:(k + 1) * D], preferred_element_type=F32)
    for k in range(CONV_DIM // D):
        lo = D_INNER + k * D
        u = jnp.dot(hn, win_ref[:, lo:lo + D], preferred_element_type=F32)
        u = _silu(_conv3_rows(u, cw_ref.at[:, k * D:(k + 1) * D], i) + cb_ref[:, k * D:(k + 1) * D])
        if k < D_INNER // D:
            xs_ref[:, k * D:(k + 1) * D] = u
        else:
            bm_ref[...] = u[:, :GROUPS * STATE]
            cm_ref[...] = u[:, GROUPS * STATE:]
    dt = _softplus(jnp.dot(hn, wdt_ref[...], preferred_element_type=F32) + dtb_ref[...])
    dtT = _softplus(
        lax.dot_general(wdtT_ref[...], hn, (((1,), (1,)), ((), ())), preferred_element_type=F32) + dtbT_ref[...])
    for g in range(GROUPS):
        dt_ref[g] = dt[:, g * DT_LANES:(g + 1) * DT_LANES].reshape(TM // CHUNK, CHUNK, DT_LANES)
        for k in range(TM // CHUNK):
            dtT_ref[g, k] = dtT[g * 2 * HPG:(g + 1) * 2 * HPG, k * CHUNK:(k + 1) * CHUNK]


def _ssd_in(x, mod_l, nw, w_in, w_dt, w_dtT, cw, cb, dtb, dtbT, j):
    n_in = w_in.shape[2]
    nck = TM // CHUNK
    return pl.pallas_call(
        _ssd_in_kernel,
        out_shape=(
            jax.ShapeDtypeStruct((R, D_INNER), F32),
            jax.ShapeDtypeStruct((R, D_INNER), F32),
            jax.ShapeDtypeStruct((R, GROUPS * STATE), F32),
            jax.ShapeDtypeStruct((R, GROUPS * STATE), F32),
            jax.ShapeDtypeStruct((GROUPS, R // CHUNK, CHUNK, DT_LANES), F32),
            jax.ShapeDtypeStruct((GROUPS, R // CHUNK, 2 * HPG, CHUNK), F32),
        ),
        grid=(R // TM,),
        in_specs=[
            pl.BlockSpec((TM, D), lambda i: (i, 0)),
            pl.BlockSpec((None, MOD_CHUNKS, D), lambda i: (_cond_of_tile(i), 0, 0)),
            pl.BlockSpec((1, D), lambda i: (0, 0)),
            pl.BlockSpec((None, D, n_in), lambda i: (j, 0, 0)),
            pl.BlockSpec((D, GROUPS * DT_LANES), lambda i: (0, 0)),
            pl.BlockSpec((2 * HEADS, D), lambda i: (0, 0)),
            pl.BlockSpec((None, 3, CONV_DIM), lambda i: (j, 0, 0)),
            pl.BlockSpec((None, 1, CONV_DIM), lambda i: (j, 0, 0)),
            pl.BlockSpec((1, GROUPS * DT_LANES), lambda i: (0, 0)),
            pl.BlockSpec((2 * HEADS, 1), lambda i: (0, 0)),
        ],
        out_specs=(
            pl.BlockSpec((TM, D_INNER), lambda i: (i, 0)),
            pl.BlockSpec((TM, D_INNER), lambda i: (i, 0)),
            pl.BlockSpec((TM, GROUPS * STATE), lambda i: (i, 0)),
            pl.BlockSpec((TM, GROUPS * STATE), lambda i: (i, 0)),
            pl.BlockSpec((GROUPS, nck, CHUNK, DT_LANES), lambda i: (0, i, 0, 0)),
            pl.BlockSpec((GROUPS, nck, 2 * HPG, CHUNK), lambda i: (0, i, 0, 0)),
        ),
        compiler_params=_cparams(("arbitrary",)),
        name="ssd_in",
    )(x, mod_l, nw, w_in, w_dt, w_dtT, cw, cb, dtb, dtbT)


def _ssd_scan_kernel(*refs, n_chunks, has_h0, n_kept, state_slots):
    emit_state = state_slots is not None
    xs_ref, bm_ref, cm_ref, dt_ref, dtT_ref, alog_ref, alogT_ref, dsk_ref = refs[:8]
    tri_k_ref, tri_r_ref, expf_ref, expb_ref, colsel_ref = refs[8:13]
    k = 13
    h0_ref = None
    if has_h0:
        h0_ref = refs[k]
        k += 1
    k += n_kept
    y_ref = refs[k]
    k += 1
    st_out_ref = None
    if emit_state:
        st_out_ref = refs[k]
        k += 1
    st_ref = refs[k]

    a_row = -jnp.exp(alog_ref[...]) * LOG2E
    a_col = -jnp.exp(alogT_ref[...]) * LOG2E
    row_is_fwd = lax.broadcasted_iota(jnp.int32, (2 * HPG, 1), 0) < HPG
    qi = lax.broadcasted_iota(jnp.int32, (CHUNK, CHUNK), 0)
    si = lax.broadcasted_iota(jnp.int32, (CHUNK, CHUNK), 1)
    lower = si <= qi
    lane = lax.broadcasted_iota(jnp.int32, (1, DT_LANES), 1)
    piece_of_lane = lane // (2 * HPG)
    lane_is_fwd = lane % (2 * HPG) < HPG
    lane_c = lax.broadcasted_iota(jnp.int32, (1, CHUNK), 1)
    first_half = lane_c < HEADDIM

    def lane_pieces(v):
        pieces = _split_pieces(v)
        out = pieces[N_SPLIT - 1]
        for r in range(N_SPLIT - 2, -1, -1):
            out = jnp.where(piece_of_lane == r, pieces[r], out)
        return out

    def expand_many(vs, e):
        out = jnp.dot(jnp.concatenate([lane_pieces(v) for v in vs], axis=0), e, preferred_element_type=F32)
        res, r0 = [], 0
        for v in vs:
            res.append(out[r0:r0 + v.shape[0], :])
            r0 += v.shape[0]
        return res

    for d in range(2):
        if has_h0:
            st_ref[d] = h0_ref[d].T
        else:
            st_ref[d] = jnp.zeros((STATE, GW), F32)

    y_ref[...] = jnp.zeros(y_ref.shape, F32)

    def body(k_, carry):
        cf = k_
        x = xs_ref[cf]
        b = bm_ref[cf]
        cm = cm_ref[cf]
        dt = dt_ref[cf]
        dtT = dtT_ref[cf]
        cr = n_chunks - 1 - k_
        xr = xs_ref[cr]
        br = bm_ref[cr]
        dtr = dt_ref[cr]

        dta_rows = jnp.concatenate(_split_pieces(dt * a_row), axis=0)
        dtaT_lanes = jnp.concatenate(_split_pieces(dtT * a_col), axis=1)
        acs2 = jnp.dot(tri_k_ref[...], dta_rows, preferred_element_type=F32)
        acs_lo, acs_up = acs2[:CHUNK, :], acs2[CHUNK:, :]
        acs_t2 = jnp.dot(dtaT_lanes, tri_r_ref[...], preferred_element_type=F32)
        acs_up_r = jnp.dot(tri_k_ref[CHUNK:, :], jnp.concatenate(_split_pieces(dtr * a_row), axis=0),
                           preferred_element_type=F32)
        r_t = jnp.log(dtT) * LOG2E - jnp.where(row_is_fwd, acs_t2[:, :CHUNK], acs_t2[:, CHUNK:])
        col_b = jnp.dot(lane_pieces(jnp.where(lane_is_fwd, acs_lo, acs_up)), colsel_ref[...],
                        preferred_element_type=F32)

        a_last = acs_lo[CHUNK - 1:CHUNK, :]
        a_tot = acs_up_r[0:1, :]
        cb_diag = jnp.sum(cm * b, axis=1, keepdims=True)
        e_out_f, e_in_f = expand_many([jnp.exp2(acs_lo), jnp.exp2(a_last - acs_lo) * dt], expf_ref[...])
        e_self, e_out_b, e_in_b = expand_many(
            [dsk_ref[...] + cb_diag * dt, jnp.exp2(acs_up_r), jnp.exp2(a_tot - acs_up_r) * dtr], expb_ref[...])
        e_keep_f = e_out_f[CHUNK - 1:CHUNK, :]
        e_keep_b = e_out_b[0:1, :]

        cmb = cm.astype(BF16)
        cb = lax.dot_general(cmb, b.astype(BF16), (((1,), (1,)), ((), ())), preferred_element_type=F32)
        parts = []
        for hp in range(HPG // 2):
            ws = []
            for h in (2 * hp, 2 * hp + 1):
                arg = jnp.where(lower, col_b[:, h * CHUNK:(h + 1) * CHUNK] + r_t[h:h + 1, :],
                                col_b[:, (HPG + h) * CHUNK:(HPG + h + 1) * CHUNK] + r_t[HPG + h:HPG + h + 1, :])
                ws.append((cb * jnp.exp2(arg)).astype(BF16))
            xp = x[:, hp * CHUNK:(hp + 1) * CHUNK]
            x2 = jnp.concatenate([jnp.where(first_half, xp, 0.0), jnp.where(first_half, 0.0, xp)], axis=0)
            parts.append(jnp.dot(jnp.concatenate(ws, axis=1), x2.astype(BF16), preferred_element_type=F32))
        st_f = st_ref[0]
        y = (jnp.concatenate(parts, axis=1) + e_self * x
             + jnp.dot(cmb, st_f.astype(BF16), preferred_element_type=F32) * e_out_f)
        y_ref[cf] = y_ref[cf] + y
        st_ref[0] = st_f * e_keep_f + jnp.dot(
            b.T.astype(BF16), (x * e_in_f).astype(BF16), preferred_element_type=F32)

        st_b = st_ref[1]
        y_ref[cr] = y_ref[cr] + jnp.dot(cm_ref[cr].astype(BF16), st_b.astype(BF16),
                                        preferred_element_type=F32) * e_out_b
        st_ref[1] = st_b * e_keep_b + jnp.dot(
            br.T.astype(BF16), (xr * e_in_b).astype(BF16), preferred_element_type=F32)
        return carry

    lax.fori_loop(0, n_chunks, body, 0)
    if emit_state:
        for slot, own in enumerate(state_slots):
            for d in range(2):
                st_out_ref[slot, d] = st_ref[d].T if own else jnp.zeros((GW, STATE), F32)


def _scan_constants():
    q = np.arange(CHUNK)
    tri_lo = (q[None, :] <= q[:, None]).astype(np.float32)
    tri_up = tri_lo.T
    tri_k = np.concatenate([np.tile(tri_lo, (1, N_SPLIT)), np.tile(tri_up, (1, N_SPLIT))], axis=0)
    tri_r = np.concatenate([np.tile(tri_up, (N_SPLIT, 1)), np.tile(tri_lo, (N_SPLIT, 1))], axis=1)
    head = np.arange(DT_LANES) % (2 * HPG)
    chan_head = np.arange(GW) // HEADDIM
    exp_f = (head[:, None] == chan_head[None, :]).astype(np.float32)
    exp_b = (head[:, None] == chan_head[None, :] + HPG).astype(np.float32)
    col_sel = (head[:, None] == (np.arange(2 * HPG * CHUNK) // CHUNK)[None, :]).astype(np.float32)
    return [jnp.asarray(m, BF16) for m in (tri_k, tri_r, exp_f, exp_b, col_sel)]


def _ssd_scan(xs3, bm3, cm3, dt4, dtT4, alog4, alogT4, dskx, h0, st_prev, *, n_seq, seq_len, row_block0, state_layer):
    nck = seq_len // CHUNK
    has_h0 = h0 is not None
    emit_state = state_layer is not None
    keep_state = st_prev is not None
    rb = lambda s: s + row_block0
    in_specs = [
        pl.BlockSpec((nck, CHUNK, GW), lambda s, g: (rb(s), 0, g)),
        pl.BlockSpec((nck, CHUNK, STATE), lambda s, g: (rb(s), 0, g)),
        pl.BlockSpec((nck, CHUNK, STATE), lambda s, g: (rb(s), 0, g)),
        pl.BlockSpec((None, nck, CHUNK, DT_LANES), lambda s, g: (g, rb(s), 0, 0)),
        pl.BlockSpec((None, nck, 2 * HPG, CHUNK), lambda s, g: (g, rb(s), 0, 0)),
        pl.BlockSpec((None, 1, DT_LANES), lambda s, g: (g, 0, 0)),
        pl.BlockSpec((None, 2 * HPG, 1), lambda s, g: (g, 0, 0)),
        pl.BlockSpec((None, 1, DT_LANES), lambda s, g: (g, 0, 0)),
    ]
    args = [xs3, bm3, cm3, dt4, dtT4, alog4, alogT4, dskx]
    for const in _scan_constants():
        in_specs.append(pl.BlockSpec(const.shape, lambda s, g: (0, 0)))
        args.append(const)
    if has_h0:
        in_specs.append(pl.BlockSpec((None, 2, GW, STATE), lambda s, g: (s, 0, g, 0)))
        args.append(h0)
    aliases = {}
    if keep_state:
        in_specs.append(pl.BlockSpec(memory_space=pl.ANY))
        aliases[len(args)] = 1
        args.append(st_prev)
    y_shape = jax.ShapeDtypeStruct((n_seq * nck, CHUNK, D_INNER), F32)
    y_spec = pl.BlockSpec((nck, CHUNK, GW), lambda s, g: (s, 0, g))
    n_layers = DEPTH // 2
    state_slots = None
    if emit_state:
        out_shape = (y_shape, jax.ShapeDtypeStruct((n_seq, n_layers, 2, HEADS * HEADDIM, STATE), F32))
        if keep_state:
            state_slots = (True,)
            st_spec = pl.BlockSpec((None, 1, 2, GW, STATE), lambda s, g: (s, state_layer, 0, g, 0))
        else:
            state_slots = tuple(l == state_layer for l in range(n_layers))
            st_spec = pl.BlockSpec((None, n_layers, 2, GW, STATE), lambda s, g: (s, 0, 0, g, 0))
        out_specs = (y_spec, st_spec)
    else:
        out_shape, out_specs = y_shape, y_spec
    return pl.pallas_call(
        functools.partial(_ssd_scan_kernel, n_chunks=nck, has_h0=has_h0, n_kept=len(aliases),
                          state_slots=state_slots),
        out_shape=out_shape,
        grid=(n_seq, GROUPS),
        in_specs=in_specs,
        out_specs=out_specs,
        scratch_shapes=[pltpu.VMEM((2, STATE, GW), F32)],
        input_output_aliases=aliases,
        compiler_params=_cparams(("arbitrary", "arbitrary")),
        name="ssd_scan",
    )(*args)


def _ssd_out_kernel(x_ref, yc_ref, yl_ref, z_ref, mod_ref, nw_ref, wout_ref, o_ref):
    y = jnp.where(pl.program_id(0) < R_CTX // TM_OUT, yc_ref[...], yl_ref[...])
    v = y * _silu(z_ref[...])
    v = v * lax.rsqrt(jnp.mean(v * v, axis=-1, keepdims=True) + EPS) * nw_ref[...]
    mix = jnp.dot(v.astype(BF16), wout_ref[...], preferred_element_type=F32)
    o_ref[...] = x_ref[...] + mod_ref[2:3, :] * mix


def _ssd_out(x, y_ctx, y_lat, z, mod_l, nw, w_out, j):
    n_ctx = R_CTX // TM_OUT
    return pl.pallas_call(
        _ssd_out_kernel,
        out_shape=jax.ShapeDtypeStruct((R, D), F32),
        grid=(R // TM_OUT,),
        in_specs=[
            pl.BlockSpec((TM_OUT, D), lambda i: (i, 0)),
            pl.BlockSpec((TM_OUT, D_INNER), lambda i: (jnp.minimum(i, n_ctx - 1), 0)),
            pl.BlockSpec((TM_OUT, D_INNER), lambda i: (jnp.maximum(i - n_ctx, 0), 0)),
            pl.BlockSpec((TM_OUT, D_INNER), lambda i: (i, 0)),
            pl.BlockSpec((None, MOD_CHUNKS, D), lambda i: (_cond_of_tile(i * TM_OUT // TM), 0, 0)),
            pl.BlockSpec((None, 1, D_INNER), lambda i: (j, 0, 0)),
            pl.BlockSpec((None, D_INNER, D), lambda i: (j, 0, 0)),
        ],
        out_specs=pl.BlockSpec((TM_OUT, D), lambda i: (i, 0)),
        compiler_params=_cparams(("arbitrary",)),
        name="ssd_out",
    )(x, y_ctx, y_lat, z, mod_l, nw, w_out)


def _router_kernel(x_ref, mod_ref, nw_ref, wrT_ref, hn_ref, affT_ref):
    hn = _norm_mod(x_ref[...], nw_ref[...], mod_ref[4:5, :], mod_ref[3:4, :])
    hn_hi = hn.astype(BF16)
    hn_ref[...] = hn_hi
    hn_lo = (hn - hn_hi.astype(F32)).astype(BF16)
    w = wrT_ref[...]
    w_hi = w.astype(BF16)
    w_lo = (w - w_hi.astype(F32)).astype(BF16)
    logits = lax.dot_general(jnp.concatenate([w_hi, w_lo, w_hi], axis=1), jnp.concatenate([hn_hi, hn_hi, hn_lo], axis=1),
                             (((1,), (1,)), ((), ())), preferred_element_type=F32)
    e = jnp.exp(logits - jnp.max(logits, axis=0, keepdims=True))
    affT_ref[...] = e / jnp.sum(e, axis=0, keepdims=True)


def _router(x, mod_l, nw, wrT):
    return pl.pallas_call(
        _router_kernel,
        out_shape=(jax.ShapeDtypeStruct((R, D), BF16), jax.ShapeDtypeStruct((N_EXPERTS, R), F32)),
        grid=(R // TM,),
        in_specs=[
            pl.BlockSpec((TM, D), lambda i: (i, 0)),
            pl.BlockSpec((None, MOD_CHUNKS, D), lambda i: (_cond_of_tile(i), 0, 0)),
            pl.BlockSpec((1, D), lambda i: (0, 0)),
            pl.BlockSpec((N_EXPERTS, D), lambda i: (0, 0)),
        ],
        out_specs=(pl.BlockSpec((TM, D), lambda i: (i, 0)), pl.BlockSpec((N_EXPERTS, TM), lambda i: (0, i))),
        compiler_params=_cparams(("arbitrary",)),
        name="router",
    )(x, mod_l, nw, wrT)


def _excl_cumsum_lanes(m):
    blk = 256
    t = m.shape[1]
    a = lax.broadcasted_iota(jnp.int32, (blk, blk), 0)
    b = lax.broadcasted_iota(jnp.int32, (blk, blk), 1)
    strict = jnp.where(a < b, 1.0, 0.0).astype(BF16)
    carry = jnp.zeros((m.shape[0], 1), F32)
    outs = []
    for k in range(t // blk):
        mk = m[:, k * blk:(k + 1) * blk]
        outs.append(jnp.dot(mk.astype(BF16), strict, preferred_element_type=F32) + carry)
        carry = carry + jnp.sum(mk, axis=1, keepdims=True)
    return outs[0] if len(outs) == 1 else jnp.concatenate(outs, axis=1)


def _select_request(aff, cap, base):
    thr_bits = jnp.zeros((N_EXPERTS, 1), jnp.int32)
    for k in range(30, -1, -1):
        trial = thr_bits | (1 << k)
        cnt = jnp.sum(jnp.where(aff >= pltpu.bitcast(trial, F32), 1.0, 0.0), axis=1, keepdims=True)
        thr_bits = jnp.where(cnt >= cap, trial, thr_bits)
    thr = pltpu.bitcast(thr_bits, F32)
    gt = aff > thr
    eq = jnp.where(aff == thr, 1.0, 0.0)
    need = cap - jnp.sum(jnp.where(gt, 1.0, 0.0), axis=1, keepdims=True)
    sel = gt | ((eq > 0.0) & (_excl_cumsum_lanes(eq) < need))
    pos = _excl_cumsum_lanes(jnp.where(sel, 1.0, 0.0))
    return jnp.where(sel, pos + base, -1.0), jnp.where(sel, aff, 0.0)


def _select_kernel(affT_ref, slotT_ref, slot_tok_ref, gate_tok_ref, lo_ref):
    s = pl.program_id(0)

    def emit(slot, gate):
        slotT_ref[...] = slot.astype(jnp.int32)
        pad = jnp.zeros((128 - N_EXPERTS, SR_TOKENS), F32)
        slot_tok_ref[...] = jnp.concatenate([slot, pad - 1.0], axis=0).T.astype(jnp.int32)
        gate_tok_ref[...] = jnp.concatenate([gate, pad], axis=0).T
        t = lax.broadcasted_iota(jnp.int32, (SR_TOKENS, 128), 0)
        jcol = lax.broadcasted_iota(jnp.int32, (SR_TOKENS, 128), 1)
        before = jnp.where(t < jcol * BLK, 1.0, 0.0).astype(BF16)
        chosen = jnp.where(slot >= 0.0, 1.0, 0.0).astype(BF16)
        lo_ref[...] = jnp.dot(chosen, before, preferred_element_type=F32).astype(jnp.int32)

    @pl.when(s < N_SR_CTX)
    def _():
        outs = [_select_request(affT_ref[:, r * L_CTX:(r + 1) * L_CTX], CAP_CTX, float(r * CAP_CTX))
                for r in range(CTX_PER_SR)]
        emit(jnp.concatenate([o[0] for o in outs], axis=1), jnp.concatenate([o[1] for o in outs], axis=1))

    @pl.when(s >= N_SR_CTX)
    def _():
        emit(*_select_request(affT_ref[...], CAP_LAT, 0.0))


def _select(affT):
    return pl.pallas_call(
        _select_kernel,
        out_shape=(
            jax.ShapeDtypeStruct((N_EXPERTS, R), jnp.int32),
            jax.ShapeDtypeStruct((R, 128), jnp.int32),
            jax.ShapeDtypeStruct((R, 128), F32),
            jax.ShapeDtypeStruct((N_SR, N_EXPERTS, 128), jnp.int32),
        ),
        grid=(N_SR,),
        in_specs=[pl.BlockSpec((N_EXPERTS, SR_TOKENS), lambda s: (0, s))],
        out_specs=(
            pl.BlockSpec((N_EXPERTS, SR_TOKENS), lambda s: (0, s)),
            pl.BlockSpec((SR_TOKENS, 128), lambda s: (s, 0)),
            pl.BlockSpec((SR_TOKENS, 128), lambda s: (s, 0)),
            pl.BlockSpec((None, N_EXPERTS, 128), lambda s: (s, 0, 0)),
        ),
        compiler_params=_cparams(("arbitrary",)),
        name="select",
    )(affT)


def _lo_at(lo_ref, sr, e, j):
    return lo_ref[(sr * N_EXPERTS + e) * LO_COLS + j]


def _gather_kernel(lo_ref, hn_ref, slotT_ref, xe_ref):
    sr = pl.program_id(0)
    e0 = pl.program_id(1) * E_HALF
    starts = {}
    fits = None
    for e in range(E_HALF):
        for j in range(N_BLK):
            start = jnp.minimum((_lo_at(lo_ref, sr, e0 + e, j) // BF16_ROWS) * BF16_ROWS, SR_SLOTS - GW_ROWS)
            ok = _lo_at(lo_ref, sr, e0 + e, j + 1) - start <= GW_ROWS
            fits = ok if fits is None else jnp.logical_and(fits, ok)
            starts[e, j] = start

    @pl.when(fits)
    def _():
        xe_ref[...] = jnp.zeros(xe_ref.shape, BF16)
        w_iota = lax.broadcasted_iota(jnp.int32, (GW_ROWS, BLK), 0)
        for j in range(N_BLK):
            rows = [jnp.where(w_iota == slotT_ref[e:e + 1, j * BLK:(j + 1) * BLK] - starts[e, j], 1.0, 0.0).astype(BF16)
                    for e in range(E_HALF)]
            part = jnp.dot(jnp.concatenate(rows, axis=0), hn_ref[j * BLK:(j + 1) * BLK, :],
                           preferred_element_type=F32).astype(BF16)
            for e in range(E_HALF):
                win = pl.ds(pl.multiple_of(starts[e, j], BF16_ROWS), GW_ROWS)
                xe_ref[e, win, :] = xe_ref[e, win, :] + part[e * GW_ROWS:(e + 1) * GW_ROWS, :]

    @pl.when(jnp.logical_not(fits))
    def _():
        s_iota = lax.broadcasted_iota(jnp.int32, (SR_SLOTS, SR_TOKENS), 0)
        for e in range(E_HALF):
            onehot = jnp.where(s_iota == slotT_ref[e:e + 1, :], 1.0, 0.0).astype(BF16)
            xe_ref[e] = jnp.dot(onehot, hn_ref[...], preferred_element_type=F32).astype(BF16)


def _gather(lo_flat, hn, slotT):
    return pl.pallas_call(
        _gather_kernel,
        out_shape=jax.ShapeDtypeStruct((N_EXPERTS, N_SR * SR_SLOTS, D), BF16),
        grid_spec=pltpu.PrefetchScalarGridSpec(
            num_scalar_prefetch=1,
            grid=(N_SR, N_EXPERTS // E_HALF),
            in_specs=[
                pl.BlockSpec((SR_TOKENS, D), lambda s, h, lo: (s, 0)),
                pl.BlockSpec((E_HALF, SR_TOKENS), lambda s, h, lo: (h, s)),
            ],
            out_specs=pl.BlockSpec((E_HALF, SR_SLOTS, D), lambda s, h, lo: (h, s, 0)),
        ),
        compiler_params=_cparams(("arbitrary", "arbitrary")),
        name="gather",
    )(lo_flat, hn, slotT)


def _ffn_kernel(xe_ref, wg_ref, wu_ref, wd_ref, ye_ref):
    xe = xe_ref[...]
    hg = jnp.dot(xe, wg_ref[...].astype(BF16), preferred_element_type=F32)
    hu = jnp.dot(xe, wu_ref[...].astype(BF16), preferred_element_type=F32)
    hid = (_silu(hg) * hu).astype(BF16)
    ye_ref[...] = jnp.dot(hid, wd_ref[...].astype(BF16), preferred_element_type=F32).astype(BF16)


def _ffn(xe, wg, wu, wd, layer):
    m = xe.shape[1]
    w_spec = pl.BlockSpec((None, None, D, D), lambda e: (layer, e, 0, 0))
    return pl.pallas_call(
        _ffn_kernel,
        out_shape=jax.ShapeDtypeStruct((N_EXPERTS, m, D), BF16),
        grid=(N_EXPERTS,),
        in_specs=[pl.BlockSpec((None, m, D), lambda e: (e, 0, 0)), w_spec, w_spec, w_spec],
        out_specs=pl.BlockSpec((None, m, D), lambda e: (e, 0, 0)),
        compiler_params=_cparams(("arbitrary",)),
        name="expert_ffn",
    )(xe, wg, wu, wd)


def _combine_kernel(lo_ref, x_ref, ye_ref, slot_ref, gate_ref, mod_ref, fnw_ref, *rest, final):
    if final:
        oc_ref, ol_ref, yw_ref, acc_ref = rest
    else:
        o_ref, yw_ref, acc_ref = rest
    sr = pl.program_id(0)
    j = pl.program_id(1)
    starts = []
    fits = None
    for e in range(N_EXPERTS):
        start = jnp.minimum((_lo_at(lo_ref, sr, e, j) // BF16_ROWS) * BF16_ROWS, SR_SLOTS - CW_ROWS)
        ok = _lo_at(lo_ref, sr, e, j + 1) - start <= CW_ROWS
        fits = ok if fits is None else jnp.logical_and(fits, ok)
        starts.append(start)
    slot = slot_ref[...]
    gate = gate_ref[...]

    @pl.when(fits)
    def _():
        for e in range(N_EXPERTS):
            yw_ref[e * CW_ROWS:(e + 1) * CW_ROWS, :] = ye_ref[e, pl.ds(pl.multiple_of(starts[e], BF16_ROWS), CW_ROWS), :]
        n_k = N_EXPERTS * CW_ROWS
        k_exp = lax.broadcasted_iota(jnp.int32, (1, n_k), 1) // CW_ROWS
        k_row = (lax.broadcasted_iota(jnp.int32, (1, n_k), 1) % CW_ROWS).astype(F32)
        start_k = jnp.zeros((1, n_k), F32)
        for e in range(N_EXPERTS):
            start_k = jnp.where(k_exp == e, starts[e].astype(F32), start_k)
        spread = jnp.where(lax.broadcasted_iota(jnp.int32, (128, n_k), 0) == k_exp, 1.0, 0.0).astype(BF16)
        slot_k = jnp.dot(slot.astype(F32).astype(BF16), spread, preferred_element_type=F32)
        hit = slot_k - start_k == k_row
        gate_k = jnp.dot(gate.astype(BF16), spread, preferred_element_type=F32)
        acc_ref[...] = jnp.dot(jnp.where(hit, gate_k, 0.0).astype(BF16), yw_ref[...], preferred_element_type=F32)

    @pl.when(jnp.logical_not(fits))
    def _():
        s_iota = lax.broadcasted_iota(jnp.int32, (BLK, SR_SLOTS), 1)
        acc = jnp.zeros((BLK, D), F32)
        for e in range(N_EXPERTS):
            onehot = jnp.where(slot[:, e:e + 1] == s_iota, 1.0, 0.0).astype(BF16)
            acc = acc + gate[:, e:e + 1] * jnp.dot(onehot, ye_ref[e], preferred_element_type=F32)
        acc_ref[...] = acc

    out = x_ref[...] + mod_ref[5:6, :] * acc_ref[...]
    if not final:
        o_ref[...] = out
    else:
        out = out * lax.rsqrt(jnp.mean(out * out, axis=-1, keepdims=True) + EPS) * fnw_ref[...]
        ol_ref[...] = out

        @pl.when(sr < N_SR_CTX)
        def _():
            oc_ref[...] = out


def _combine(lo_flat, x, ye, slot_tok, gate_tok, mod_l, fnw, *, final):
    blk_of = lambda s, j: s * N_BLK + j
    n_ctx_blk = R_CTX // BLK
    if final:
        out_shape = (jax.ShapeDtypeStruct((R_CTX, D), F32), jax.ShapeDtypeStruct((R_LAT, D), F32))
        out_specs = (pl.BlockSpec((BLK, D), lambda s, j, lo: (jnp.minimum(blk_of(s, j), n_ctx_blk - 1), 0)),
                     pl.BlockSpec((BLK, D), lambda s, j, lo: (jnp.maximum(blk_of(s, j) - n_ctx_blk, 0), 0)))
    else:
        out_shape = jax.ShapeDtypeStruct((R, D), F32)
        out_specs = pl.BlockSpec((BLK, D), lambda s, j, lo: (blk_of(s, j), 0))
    return pl.pallas_call(
        functools.partial(_combine_kernel, final=final),
        out_shape=out_shape,
        grid_spec=pltpu.PrefetchScalarGridSpec(
            num_scalar_prefetch=1,
            grid=(N_SR, N_BLK),
            in_specs=[
                pl.BlockSpec((BLK, D), lambda s, j, lo: (blk_of(s, j), 0)),
                pl.BlockSpec((N_EXPERTS, SR_SLOTS, D), lambda s, j, lo: (0, s, 0)),
                pl.BlockSpec((BLK, 128), lambda s, j, lo: (blk_of(s, j), 0)),
                pl.BlockSpec((BLK, 128), lambda s, j, lo: (blk_of(s, j), 0)),
                pl.BlockSpec((None, MOD_CHUNKS, D),
                             lambda s, j, lo: (_cond_of_tile(blk_of(s, j) * BLK // TM), 0, 0)),
                pl.BlockSpec((1, D), lambda s, j, lo: (0, 0)),
            ],
            out_specs=out_specs,
            scratch_shapes=[pltpu.VMEM((N_EXPERTS * CW_ROWS, D), BF16), pltpu.VMEM((BLK, D), F32)],
        ),
        compiler_params=_cparams(("arbitrary", "arbitrary")),
        name="combine",
    )(lo_flat, x, ye, slot_tok, gate_tok, mod_l, fnw)


def _moe(x, mod_l, nw, wrT, wg, wu, wd, fnw, layer, *, final):
    hn, affT = _router(x, mod_l, nw, wrT)
    slotT, slot_tok, gate_tok, lo = _select(affT)
    lo_flat = lo[:, :, :LO_COLS].reshape(-1)
    xe = _gather(lo_flat, hn, slotT)
    ye = _ffn(xe, wg, wu, wd, layer)
    return _combine(lo_flat, x, ye, slot_tok, gate_tok, mod_l, fnw, final=final)


def _group_major(p):
    return p.reshape(2, GROUPS, HPG).transpose(1, 0, 2).reshape(GROUPS, 2 * HPG)


def kernel(x_prompt, x_sample, state_ssm, c, c_ctx, norm1_w, norm2_w, w_mod, b_mod, conv_in_w, conv_w, conv_out_w, ssd_in_w, ssd_conv_w, ssd_conv_b, ssd_dt_bias, ssd_a_log, ssd_d, ssd_norm_w, ssd_out_w, router_w, exp_w_gate, exp_w_up, exp_w_down, final_norm_w):
    x = (x_prompt.reshape(R_CTX, D), x_sample.reshape(R_LAT, D))
    cond = jnp.concatenate([c_ctx[None, :], c, jnp.zeros((8 - N_COND, D), F32)], axis=0)
    mods = _modulation(cond.T, w_mod, b_mod)
    mods = mods[:, :N_COND].reshape(DEPTH, N_COND, MOD_CHUNKS, D)
    fnw = final_norm_w.reshape(1, D)
    conv_in_b, conv_out_b = conv_in_w.astype(BF16), conv_out_w.astype(BF16)
    ssd_in_b, ssd_out_b = ssd_in_w[:, :, :D_INNER + CONV_DIM].astype(BF16), ssd_out_w.astype(BF16)
    ssd_conv_b3 = ssd_conv_b.reshape(DEPTH // 2, 1, CONV_DIM)
    ssd_norm_w3 = ssd_norm_w.reshape(DEPTH // 2, 1, D_INNER)

    states = None
    for layer in range(DEPTH):
        mod_l = mods[layer]
        j = layer // 2
        nw1 = norm1_w[layer].reshape(1, D)
        if layer % 2 == 0:
            x = _conv_mixer(x, mod_l, nw1, conv_in_b, conv_w, conv_out_b, j)
        else:
            w_dt = ssd_in_w[j, :, D_INNER + CONV_DIM:].astype(BF16).reshape(D, 2, GROUPS, HPG).transpose(0, 2, 1, 3)
            w_dt_rep = jnp.broadcast_to(w_dt.reshape(D, GROUPS, 1, 2 * HPG), (D, GROUPS, N_SPLIT, 2 * HPG))
            dtb = _group_major(ssd_dt_bias[j])
            dtb_rep = jnp.tile(dtb, (1, N_SPLIT)).reshape(1, GROUPS * DT_LANES)
            z, xs, bm, cm, dt4, dtT4 = _ssd_in(
                x, mod_l, nw1, ssd_in_b, w_dt_rep.reshape(D, GROUPS * DT_LANES), w_dt.reshape(D, 2 * HEADS).T,
                ssd_conv_w, ssd_conv_b3, dtb_rep, dtb.reshape(2 * HEADS, 1), j)
            alog4 = _group_major(ssd_a_log[j])
            dsum = (ssd_d[j][0] + ssd_d[j][1]).reshape(GROUPS, HPG)
            dsk4 = jnp.tile(jnp.concatenate([jnp.zeros_like(dsum), dsum], axis=1), (1, N_SPLIT))
            scan_args = (xs.reshape(R // CHUNK, CHUNK, D_INNER), bm.reshape(R // CHUNK, CHUNK, GROUPS * STATE),
                         cm.reshape(R // CHUNK, CHUNK, GROUPS * STATE), dt4, dtT4,
                         jnp.tile(alog4, (1, N_SPLIT)).reshape(GROUPS, 1, DT_LANES),
                         alog4.reshape(GROUPS, 2 * HPG, 1), dsk4.reshape(GROUPS, 1, DT_LANES))
            y_ctx, states = _ssd_scan(*scan_args, None, states, n_seq=N_CTX, seq_len=L_CTX, row_block0=0,
                                      state_layer=j)
            h0 = state_ssm[:, j].reshape(N_LAT, 2, HEADS * HEADDIM, STATE)
            y_lat = _ssd_scan(*scan_args, h0, None, n_seq=N_LAT, seq_len=L_LAT, row_block0=R_CTX // L_LAT,
                              state_layer=None)
            x = _ssd_out(x, y_ctx.reshape(R_CTX, D_INNER), y_lat.reshape(R_LAT, D_INNER), z, mod_l,
                         ssd_norm_w3, ssd_out_b, j)
        x = _moe(x, mod_l, norm2_w[layer].reshape(1, D), router_w[layer].T, exp_w_gate, exp_w_up, exp_w_down, fnw,
                 layer, final=(layer == DEPTH - 1))

    y_prompt, y_sample = x
    return (y_prompt.reshape(N_CTX, L_CTX, D), y_sample.reshape(N_LAT, L_LAT, D),
            states.reshape(N_CTX, DEPTH // 2, 2, HEADS, HEADDIM, STATE))
```

```python
import functools

import jax
import jax.numpy as jnp
import numpy as np
from jax import lax
from jax.experimental import pallas as pl
from jax.experimental.pallas import tpu as pltpu

F32 = jnp.float32
BF16 = jnp.bfloat16
HIGHEST = lax.Precision.HIGHEST
LOG2E = 1.4426950408889634

D = 1024
DEPTH = 4
N_CTX, L_CTX = 16, 256
N_LAT, L_LAT = 2, 2048
GRID_W = 64
R_CTX = N_CTX * L_CTX
R_LAT = N_LAT * L_LAT
R = R_CTX + R_LAT
N_COND = 1 + N_LAT
MOD_CHUNKS = 6
D_INNER = 2048
HEADDIM = 64
HEADS = 32
GROUPS = 4
HPG = HEADS // GROUPS
GW = HPG * HEADDIM
STATE = 128
CHUNK = 128
CONV_DIM = D_INNER + 2 * GROUPS * STATE
N_SPLIT = 2
DT_LANES = N_SPLIT * 2 * HPG
N_EXPERTS = 16
EPS = 1e-6

TM = 512
TM_OUT = 512
CTX_TILES = R_CTX // TM
LAT_TILES_PER_REQ = L_LAT // TM
SR_TOKENS = 2048
N_SR = R // SR_TOKENS
N_SR_CTX = R_CTX // SR_TOKENS
SR_SLOTS = 2 * SR_TOKENS // N_EXPERTS
CTX_PER_SR = SR_TOKENS // L_CTX
CAP_CTX = 2 * L_CTX // N_EXPERTS
CAP_LAT = 2 * L_LAT // N_EXPERTS
BLK = 256
N_BLK = SR_TOKENS // BLK
LO_COLS = 16
BF16_ROWS = 16
GW_ROWS = 80
CW_ROWS = 64
E_HALF = N_EXPERTS // 2
VMEM_LIMIT = 56 * 1024 * 1024


def _cparams(sem):
    return pltpu.CompilerParams(dimension_semantics=sem, vmem_limit_bytes=VMEM_LIMIT)


def _cond_of_tile(i):
    return jnp.where(i < CTX_TILES, 0, 1 + (i - CTX_TILES) // LAT_TILES_PER_REQ)


def _silu(v):
    h = 0.5 * v
    return h + h * jnp.tanh(h)


def _softplus(v):
    return jnp.maximum(v, 0.0) + jnp.log1p(jnp.exp(-jnp.abs(v)))


def _norm_mod(x, nw, scale, shift):
    y = x * lax.rsqrt(jnp.mean(x * x, axis=-1, keepdims=True) + EPS)
    return y * nw * (1.0 + scale) + shift


def _conv3_rows(p, w_ref, tile_idx):
    n = p.shape[0]
    period = jnp.where(tile_idx < CTX_TILES, L_CTX, GRID_W)
    r = lax.broadcasted_iota(jnp.int32, (n, 1), 0) & (period - 1)
    prev = jnp.where(r == 0, 0.0, pltpu.roll(p, 1, axis=0))
    nxt = jnp.where(r == period - 1, 0.0, pltpu.roll(p, n - 1, axis=0))
    return prev * w_ref[0:1, :] + p * w_ref[1:2, :] + nxt * w_ref[2:3, :]


def _split_pieces(v):
    pieces = []
    r = v
    for _ in range(N_SPLIT):
        p = r.astype(BF16)
        pieces.append(p)
        r = r - p.astype(F32)
    return pieces


def _modulation_kernel(condT_ref, w_ref, b_ref, o_ref):
    s = _silu(condT_ref[...])
    w = w_ref[...]
    rows = [jnp.sum(w * s[:, r:r + 1], axis=0, keepdims=True) + b_ref[...] for r in range(N_COND)]
    rows.append(jnp.zeros((8 - N_COND, w.shape[1]), F32))
    o_ref[...] = jnp.concatenate(rows, axis=0)


def _modulation(condT, w_mod, b_mod):
    tn = 1536
    n = MOD_CHUNKS * D
    return pl.pallas_call(
        _modulation_kernel,
        out_shape=jax.ShapeDtypeStruct((DEPTH, 8, n), F32),
        grid=(DEPTH, n // tn),
        in_specs=[
            pl.BlockSpec((D, 8), lambda l, j: (0, 0)),
            pl.BlockSpec((None, D, tn), lambda l, j: (l, 0, j)),
            pl.BlockSpec((None, 1, tn), lambda l, j: (l, 0, j)),
        ],
        out_specs=pl.BlockSpec((None, 8, tn), lambda l, j: (l, 0, j)),
        compiler_params=_cparams(("arbitrary", "arbitrary")),
        name="modulation",
    )(condT, w_mod, b_mod.reshape(DEPTH, 1, n))


def _conv_mixer_kernel(*refs, split_input):
    i = pl.program_id(0)
    if split_input:
        xc_ref, xl_ref, mod_ref, nw_ref, win_ref, cw_ref, wout_ref, o_ref = refs
        x = jnp.where(i < CTX_TILES, xc_ref[...], xl_ref[...])
    else:
        x_ref, mod_ref, nw_ref, win_ref, cw_ref, wout_ref, o_ref = refs
        x = x_ref[...]
    hn = _norm_mod(x, nw_ref[...], mod_ref[1:2, :], mod_ref[0:1, :]).astype(BF16)
    gb = jnp.dot(hn, win_ref[:, 0:D], preferred_element_type=F32)
    gc = jnp.dot(hn, win_ref[:, D:2 * D], preferred_element_type=F32)
    v = jnp.dot(hn, win_ref[:, 2 * D:3 * D], preferred_element_type=F32)
    q = (gb * _conv3_rows(gc * v, cw_ref, i)).astype(BF16)
    mix = jnp.dot(q, wout_ref[...], preferred_element_type=F32)
    o_ref[...] = x + mod_ref[2:3, :] * mix


def _conv_mixer(xs, mod_l, nw, w_in, cw, w_out, j):
    split_input = isinstance(xs, tuple)
    if split_input:
        x_specs = [pl.BlockSpec((TM, D), lambda i: (jnp.minimum(i, CTX_TILES - 1), 0)),
                   pl.BlockSpec((TM, D), lambda i: (jnp.maximum(i - CTX_TILES, 0), 0))]
    else:
        xs = (xs,)
        x_specs = [pl.BlockSpec((TM, D), lambda i: (i, 0))]
    return pl.pallas_call(
        functools.partial(_conv_mixer_kernel, split_input=split_input),
        out_shape=jax.ShapeDtypeStruct((R, D), F32),
        grid=(R // TM,),
        in_specs=x_specs + [
            pl.BlockSpec((None, MOD_CHUNKS, D), lambda i: (_cond_of_tile(i), 0, 0)),
            pl.BlockSpec((1, D), lambda i: (0, 0)),
            pl.BlockSpec((None, D, 3 * D), lambda i: (j, 0, 0)),
            pl.BlockSpec((None, 3, D), lambda i: (j, 0, 0)),
            pl.BlockSpec((None, D, D), lambda i: (j, 0, 0)),
        ],
        out_specs=pl.BlockSpec((TM, D), lambda i: (i, 0)),
        compiler_params=_cparams(("arbitrary",)),
        name="conv_mixer",
    )(*xs, mod_l, nw, w_in, cw, w_out)


def _ssd_in_kernel(x_ref, mod_ref, nw_ref, win_ref, wdt_ref, wdtT_ref, cw_ref, cb_ref, dtb_ref, dtbT_ref,
                   z_ref, xs_ref, bm_ref, cm_ref, dt_ref, dtT_ref):
    i = pl.program_id(0)
    hn = _norm_mod(x_ref[...], nw_ref[...], mod_ref[1:2, :], mod_ref[0:1, :]).astype(BF16)
    for k in range(D_INNER // D):
        z_ref[:, k * D:(k + 1) * D] = jnp.dot(hn, win_ref[:, k * D:(k + 1) * D], preferred_element_type=F32)
    for k in range(CONV_DIM // D):
        lo = D_INNER + k * D
        u = jnp.dot(hn, win_ref[:, lo:lo + D], preferred_element_type=F32)
        u = _silu(_conv3_rows(u, cw_ref.at[:, k * D:(k + 1) * D], i) + cb_ref[:, k * D:(k + 1) * D])
        if k < D_INNER // D:
            xs_ref[:, k * D:(k + 1) * D] = u
        else:
            bm_ref[...] = u[:, :GROUPS * STATE]
            cm_ref[...] = u[:, GROUPS * STATE:]
    dt = _softplus(jnp.dot(hn, wdt_ref[...], preferred_element_type=F32) + dtb_ref[...])
    dtT = _softplus(
        lax.dot_general(wdtT_ref[...], hn, (((1,), (1,)), ((), ())), preferred_element_type=F32) + dtbT_ref[...])
    for g in range(GROUPS):
        dt_ref[g] = dt[:, g * DT_LANES:(g + 1) * DT_LANES].reshape(TM // CHUNK, CHUNK, DT_LANES)
        for k in range(TM // CHUNK):
            dtT_ref[g, k] = dtT[g * 2 * HPG:(g + 1) * 2 * HPG, k * CHUNK:(k + 1) * CHUNK]


def _ssd_in(x, mod_l, nw, w_in, w_dt, w_dtT, cw, cb, dtb, dtbT, j):
    n_in = w_in.shape[2]
    nck = TM // CHUNK
    return pl.pallas_call(
        _ssd_in_kernel,
        out_shape=(
            jax.ShapeDtypeStruct((R, D_INNER), F32),
            jax.ShapeDtypeStruct((R, D_INNER), F32),
            jax.ShapeDtypeStruct((R, GROUPS * STATE), F32),
            jax.ShapeDtypeStruct((R, GROUPS * STATE), F32),
            jax.ShapeDtypeStruct((GROUPS, R // CHUNK, CHUNK, DT_LANES), F32),
            jax.ShapeDtypeStruct((GROUPS, R // CHUNK, 2 * HPG, CHUNK), F32),
        ),
        grid=(R // TM,),
        in_specs=[
            pl.BlockSpec((TM, D), lambda i: (i, 0)),
            pl.BlockSpec((None, MOD_CHUNKS, D), lambda i: (_cond_of_tile(i), 0, 0)),
            pl.BlockSpec((1, D), lambda i: (0, 0)),
            pl.BlockSpec((None, D, n_in), lambda i: (j, 0, 0)),
            pl.BlockSpec((D, GROUPS * DT_LANES), lambda i: (0, 0)),
            pl.BlockSpec((2 * HEADS, D), lambda i: (0, 0)),
            pl.BlockSpec((None, 3, CONV_DIM), lambda i: (j, 0, 0)),
            pl.BlockSpec((None, 1, CONV_DIM), lambda i: (j, 0, 0)),
            pl.BlockSpec((1, GROUPS * DT_LANES), lambda i: (0, 0)),
            pl.BlockSpec((2 * HEADS, 1), lambda i: (0, 0)),
        ],
        out_specs=(
            pl.BlockSpec((TM, D_INNER), lambda i: (i, 0)),
            pl.BlockSpec((TM, D_INNER), lambda i: (i, 0)),
            pl.BlockSpec((TM, GROUPS * STATE), lambda i: (i, 0)),
            pl.BlockSpec((TM, GROUPS * STATE), lambda i: (i, 0)),
            pl.BlockSpec((GROUPS, nck, CHUNK, DT_LANES), lambda i: (0, i, 0, 0)),
            pl.BlockSpec((GROUPS, nck, 2 * HPG, CHUNK), lambda i: (0, i, 0, 0)),
        ),
        compiler_params=_cparams(("arbitrary",)),
        name="ssd_in",
    )(x, mod_l, nw, w_in, w_dt, w_dtT, cw, cb, dtb, dtbT)


def _ssd_scan_kernel(*refs, n_chunks, has_h0, n_kept, state_slots):
    emit_state = state_slots is not None
    xs_ref, bm_ref, cm_ref, dt_ref, dtT_ref, alog_ref, alogT_ref, dsk_ref = refs[:8]
    tri_k_ref, tri_r_ref, expf_ref, expb_ref, colsel_ref = refs[8:13]
    k = 13
    h0_ref = None
    if has_h0:
        h0_ref = refs[k]
        k += 1
    k += n_kept
    y_ref = refs[k]
    k += 1
    st_out_ref = None
    if emit_state:
        st_out_ref = refs[k]
        k += 1
    st_ref = refs[k]

    a_row = -jnp.exp(alog_ref[...]) * LOG2E
    a_col = -jnp.exp(alogT_ref[...]) * LOG2E
    row_is_fwd = lax.broadcasted_iota(jnp.int32, (2 * HPG, 1), 0) < HPG
    qi = lax.broadcasted_iota(jnp.int32, (CHUNK, CHUNK), 0)
    si = lax.broadcasted_iota(jnp.int32, (CHUNK, CHUNK), 1)
    lower = si <= qi
    lane = lax.broadcasted_iota(jnp.int32, (1, DT_LANES), 1)
    piece_of_lane = lane // (2 * HPG)
    lane_is_fwd = lane % (2 * HPG) < HPG
    lane_c = lax.broadcasted_iota(jnp.int32, (1, CHUNK), 1)
    first_half = lane_c < HEADDIM

    def lane_pieces(v):
        pieces = _split_pieces(v)
        out = pieces[N_SPLIT - 1]
        for r in range(N_SPLIT - 2, -1, -1):
            out = jnp.where(piece_of_lane == r, pieces[r], out)
        return out

    def expand_many(vs, e):
        out = jnp.dot(jnp.concatenate([lane_pieces(v) for v in vs], axis=0), e, preferred_element_type=F32)
        res, r0 = [], 0
        for v in vs:
            res.append(out[r0:r0 + v.shape[0], :])
            r0 += v.shape[0]
        return res

    for d in range(2):
        if has_h0:
            st_ref[d] = h0_ref[d].T
        else:
            st_ref[d] = jnp.zeros((STATE, GW), F32)

    y_ref[...] = jnp.zeros(y_ref.shape, F32)

    def body(k_, carry):
        cf = k_
        x = xs_ref[cf]
        b = bm_ref[cf]
        cm = cm_ref[cf]
        dt = dt_ref[cf]
        dtT = dtT_ref[cf]
        cr = n_chunks - 1 - k_
        xr = xs_ref[cr]
        br = bm_ref[cr]
        dtr = dt_ref[cr]

        dta_rows = jnp.concatenate(_split_pieces(dt * a_row), axis=0)
        dtaT_lanes = jnp.concatenate(_split_pieces(dtT * a_col), axis=1)
        acs2 = jnp.dot(tri_k_ref[...], dta_rows, preferred_element_type=F32)
        acs_lo, acs_up = acs2[:CHUNK, :], acs2[CHUNK:, :]
        acs_t2 = jnp.dot(dtaT_lanes, tri_r_ref[...], preferred_element_type=F32)
        acs_up_r = jnp.dot(tri_k_ref[CHUNK:, :], jnp.concatenate(_split_pieces(dtr * a_row), axis=0),
                           preferred_element_type=F32)
        r_t = jnp.log(dtT) * LOG2E - jnp.where(row_is_fwd, acs_t2[:, :CHUNK], acs_t2[:, CHUNK:])
        col_b = jnp.dot(lane_pieces(jnp.where(lane_is_fwd, acs_lo, acs_up)), colsel_ref[...],
                        preferred_element_type=F32)

        a_last = acs_lo[CHUNK - 1:CHUNK, :]
        a_tot = acs_up_r[0:1, :]
        cb_diag = jnp.sum(cm * b, axis=1, keepdims=True)
        e_out_f, e_in_f = expand_many([jnp.exp2(acs_lo), jnp.exp2(a_last - acs_lo) * dt], expf_ref[...])
        e_self, e_out_b, e_in_b = expand_many(
            [dsk_ref[...] + cb_diag * dt, jnp.exp2(acs_up_r), jnp.exp2(a_tot - acs_up_r) * dtr], expb_ref[...])
        e_keep_f = e_out_f[CHUNK - 1:CHUNK, :]
        e_keep_b = e_out_b[0:1, :]

        cmb = cm.astype(BF16)
        cb = lax.dot_general(cmb, b.astype(BF16), (((1,), (1,)), ((), ())), preferred_element_type=F32)
        parts = []
        for hp in range(HPG // 2):
            ws = []
            for h in (2 * hp, 2 * hp + 1):
                arg = jnp.where(lower, col_b[:, h * CHUNK:(h + 1) * CHUNK] + r_t[h:h + 1, :],
                                col_b[:, (HPG + h) * CHUNK:(HPG + h + 1) * CHUNK] + r_t[HPG + h:HPG + h + 1, :])
                ws.append((cb * jnp.exp2(arg)).astype(BF16))
            xp = x[:, hp * CHUNK:(hp + 1) * CHUNK]
            x2 = jnp.concatenate([jnp.where(first_half, xp, 0.0), jnp.where(first_half, 0.0, xp)], axis=0)
            parts.append(jnp.dot(jnp.concatenate(ws, axis=1), x2.astype(BF16), preferred_element_type=F32))
        st_f = st_ref[0]
        y = (jnp.concatenate(parts, axis=1) + e_self * x
             + jnp.dot(cmb, st_f.astype(BF16), preferred_element_type=F32) * e_out_f)
        y_ref[cf] = y_ref[cf] + y
        st_ref[0] = st_f * e_keep_f + jnp.dot(
            b.T.astype(BF16), (x * e_in_f).astype(BF16), preferred_element_type=F32)

        st_b = st_ref[1]
        y_ref[cr] = y_ref[cr] + jnp.dot(cm_ref[cr].astype(BF16), st_b.astype(BF16),
                                        preferred_element_type=F32) * e_out_b
        st_ref[1] = st_b * e_keep_b + jnp.dot(
            br.T.astype(BF16), (xr * e_in_b).astype(BF16), preferred_element_type=F32)
        return carry

    lax.fori_loop(0, n_chunks, body, 0)
    if emit_state:
        for slot, own in enumerate(state_slots):
            for d in range(2):
                st_out_ref[slot, d] = st_ref[d].T if own else jnp.zeros((GW, STATE), F32)


def _scan_constants():
    q = np.arange(CHUNK)
    tri_lo = (q[None, :] <= q[:, None]).astype(np.float32)
    tri_up = tri_lo.T
    tri_k = np.concatenate([np.tile(tri_lo, (1, N_SPLIT)), np.tile(tri_up, (1, N_SPLIT))], axis=0)
    tri_r = np.concatenate([np.tile(tri_up, (N_SPLIT, 1)), np.tile(tri_lo, (N_SPLIT, 1))], axis=1)
    head = np.arange(DT_LANES) % (2 * HPG)
    chan_head = np.arange(GW) // HEADDIM
    exp_f = (head[:, None] == chan_head[None, :]).astype(np.float32)
    exp_b = (head[:, None] == chan_head[None, :] + HPG).astype(np.float32)
    col_sel = (head[:, None] == (np.arange(2 * HPG * CHUNK) // CHUNK)[None, :]).astype(np.float32)
    return [jnp.asarray(m, BF16) for m in (tri_k, tri_r, exp_f, exp_b, col_sel)]


def _ssd_scan(xs3, bm3, cm3, dt4, dtT4, alog4, alogT4, dskx, h0, st_prev, *, n_seq, seq_len, row_block0, state_layer):
    nck = seq_len // CHUNK
    has_h0 = h0 is not None
    emit_state = state_layer is not None
    keep_state = st_prev is not None
    rb = lambda s: s + row_block0
    in_specs = [
        pl.BlockSpec((nck, CHUNK, GW), lambda s, g: (rb(s), 0, g)),
        pl.BlockSpec((nck, CHUNK, STATE), lambda s, g: (rb(s), 0, g)),
        pl.BlockSpec((nck, CHUNK, STATE), lambda s, g: (rb(s), 0, g)),
        pl.BlockSpec((None, nck, CHUNK, DT_LANES), lambda s, g: (g, rb(s), 0, 0)),
        pl.BlockSpec((None, nck, 2 * HPG, CHUNK), lambda s, g: (g, rb(s), 0, 0)),
        pl.BlockSpec((None, 1, DT_LANES), lambda s, g: (g, 0, 0)),
        pl.BlockSpec((None, 2 * HPG, 1), lambda s, g: (g, 0, 0)),
        pl.BlockSpec((None, 1, DT_LANES), lambda s, g: (g, 0, 0)),
    ]
    args = [xs3, bm3, cm3, dt4, dtT4, alog4, alogT4, dskx]
    for const in _scan_constants():
        in_specs.append(pl.BlockSpec(const.shape, lambda s, g: (0, 0)))
        args.append(const)
    if has_h0:
        in_specs.append(pl.BlockSpec((None, 2, GW, STATE), lambda s, g: (s, 0, g, 0)))
        args.append(h0)
    aliases = {}
    if keep_state:
        in_specs.append(pl.BlockSpec(memory_space=pl.ANY))
        aliases[len(args)] = 1
        args.append(st_prev)
    y_shape = jax.ShapeDtypeStruct((n_seq * nck, CHUNK, D_INNER), F32)
    y_spec = pl.BlockSpec((nck, CHUNK, GW), lambda s, g: (s, 0, g))
    n_layers = DEPTH // 2
    state_slots = None
    if emit_state:
        out_shape = (y_shape, jax.ShapeDtypeStruct((n_seq, n_layers, 2, HEADS * HEADDIM, STATE), F32))
        if keep_state:
            state_slots = (True,)
            st_spec = pl.BlockSpec((None, 1, 2, GW, STATE), lambda s, g: (s, state_layer, 0, g, 0))
        else:
            state_slots = tuple(l == state_layer for l in range(n_layers))
            st_spec = pl.BlockSpec((None, n_layers, 2, GW, STATE), lambda s, g: (s, 0, 0, g, 0))
        out_specs = (y_spec, st_spec)
    else:
        out_shape, out_specs = y_shape, y_spec
    return pl.pallas_call(
        functools.partial(_ssd_scan_kernel, n_chunks=nck, has_h0=has_h0, n_kept=len(aliases),
                          state_slots=state_slots),
        out_shape=out_shape,
        grid=(n_seq, GROUPS),
        in_specs=in_specs,
        out_specs=out_specs,
        scratch_shapes=[pltpu.VMEM((2, STATE, GW), F32)],
        input_output_aliases=aliases,
        compiler_params=_cparams(("arbitrary", "arbitrary")),
        name="ssd_scan",
    )(*args)


def _ssd_out_kernel(x_ref, yc_ref, yl_ref, z_ref, mod_ref, nw_ref, wout_ref, o_ref):
    y = jnp.where(pl.program_id(0) < R_CTX // TM_OUT, yc_ref[...], yl_ref[...])
    v = y * _silu(z_ref[...])
    v = v * lax.rsqrt(jnp.mean(v * v, axis=-1, keepdims=True) + EPS) * nw_ref[...]
    mix = jnp.dot(v.astype(BF16), wout_ref[...], preferred_element_type=F32)
    o_ref[...] = x_ref[...] + mod_ref[2:3, :] * mix


def _ssd_out(x, y_ctx, y_lat, z, mod_l, nw, w_out, j):
    n_ctx = R_CTX // TM_OUT
    return pl.pallas_call(
        _ssd_out_kernel,
        out_shape=jax.ShapeDtypeStruct((R, D), F32),
        grid=(R // TM_OUT,),
        in_specs=[
            pl.BlockSpec((TM_OUT, D), lambda i: (i, 0)),
            pl.BlockSpec((TM_OUT, D_INNER), lambda i: (jnp.minimum(i, n_ctx - 1), 0)),
            pl.BlockSpec((TM_OUT, D_INNER), lambda i: (jnp.maximum(i - n_ctx, 0), 0)),
            pl.BlockSpec((TM_OUT, D_INNER), lambda i: (i, 0)),
            pl.BlockSpec((None, MOD_CHUNKS, D), lambda i: (_cond_of_tile(i * TM_OUT // TM), 0, 0)),
            pl.BlockSpec((None, 1, D_INNER), lambda i: (j, 0, 0)),
            pl.BlockSpec((None, D_INNER, D), lambda i: (j, 0, 0)),
        ],
        out_specs=pl.BlockSpec((TM_OUT, D), lambda i: (i, 0)),
        compiler_params=_cparams(("arbitrary",)),
        name="ssd_out",
    )(x, y_ctx, y_lat, z, mod_l, nw, w_out)


def _router_kernel(x_ref, mod_ref, nw_ref, wrT_ref, hn_ref, affT_ref):
    hn = _norm_mod(x_ref[...], nw_ref[...], mod_ref[4:5, :], mod_ref[3:4, :])
    hn_hi = hn.astype(BF16)
    hn_ref[...] = hn_hi
    hn_lo = (hn - hn_hi.astype(F32)).astype(BF16)
    w = wrT_ref[...]
    w_hi = w.astype(BF16)
    w_lo = (w - w_hi.astype(F32)).astype(BF16)
    logits = lax.dot_general(jnp.concatenate([w_hi, w_lo, w_hi], axis=1), jnp.concatenate([hn_hi, hn_hi, hn_lo], axis=1),
                             (((1,), (1,)), ((), ())), preferred_element_type=F32)
    e = jnp.exp(logits - jnp.max(logits, axis=0, keepdims=True))
    affT_ref[...] = e / jnp.sum(e, axis=0, keepdims=True)


def _router(x, mod_l, nw, wrT):
    return pl.pallas_call(
        _router_kernel,
        out_shape=(jax.ShapeDtypeStruct((R, D), BF16), jax.ShapeDtypeStruct((N_EXPERTS, R), F32)),
        grid=(R // TM,),
        in_specs=[
            pl.BlockSpec((TM, D), lambda i: (i, 0)),
            pl.BlockSpec((None, MOD_CHUNKS, D), lambda i: (_cond_of_tile(i), 0, 0)),
            pl.BlockSpec((1, D), lambda i: (0, 0)),
            pl.BlockSpec((N_EXPERTS, D), lambda i: (0, 0)),
        ],
        out_specs=(pl.BlockSpec((TM, D), lambda i: (i, 0)), pl.BlockSpec((N_EXPERTS, TM), lambda i: (0, i))),
        compiler_params=_cparams(("arbitrary",)),
        name="router",
    )(x, mod_l, nw, wrT)


def _excl_cumsum_lanes(m):
    blk = 256
    t = m.shape[1]
    a = lax.broadcasted_iota(jnp.int32, (blk, blk), 0)
    b = lax.broadcasted_iota(jnp.int32, (blk, blk), 1)
    strict = jnp.where(a < b, 1.0, 0.0).astype(BF16)
    carry = jnp.zeros((m.shape[0], 1), F32)
    outs = []
    for k in range(t // blk):
        mk = m[:, k * blk:(k + 1) * blk]
        outs.append(jnp.dot(mk.astype(BF16), strict, preferred_element_type=F32) + carry)
        carry = carry + jnp.sum(mk, axis=1, keepdims=True)
    return outs[0] if len(outs) == 1 else jnp.concatenate(outs, axis=1)


def _select_request(aff, cap, base):
    thr_bits = jnp.zeros((N_EXPERTS, 1), jnp.int32)
    for k in range(30, -1, -1):
        trial = thr_bits | (1 << k)
        cnt = jnp.sum(jnp.where(aff >= pltpu.bitcast(trial, F32), 1.0, 0.0), axis=1, keepdims=True)
        thr_bits = jnp.where(cnt >= cap, trial, thr_bits)
    thr = pltpu.bitcast(thr_bits, F32)
    gt = aff > thr
    eq = jnp.where(aff == thr, 1.0, 0.0)
    need = cap - jnp.sum(jnp.where(gt, 1.0, 0.0), axis=1, keepdims=True)
    sel = gt | ((eq > 0.0) & (_excl_cumsum_lanes(eq) < need))
    pos = _excl_cumsum_lanes(jnp.where(sel, 1.0, 0.0))
    return jnp.where(sel, pos + base, -1.0), jnp.where(sel, aff, 0.0)


def _select_kernel(affT_ref, slotT_ref, slot_tok_ref, gate_tok_ref, lo_ref):
    s = pl.program_id(0)

    def emit(slot, gate):
        slotT_ref[...] = slot.astype(jnp.int32)
        pad = jnp.zeros((128 - N_EXPERTS, SR_TOKENS), F32)
        slot_tok_ref[...] = jnp.concatenate([slot, pad - 1.0], axis=0).T.astype(jnp.int32)
        gate_tok_ref[...] = jnp.concatenate([gate, pad], axis=0).T
        t = lax.broadcasted_iota(jnp.int32, (SR_TOKENS, 128), 0)
        jcol = lax.broadcasted_iota(jnp.int32, (SR_TOKENS, 128), 1)
        before = jnp.where(t < jcol * BLK, 1.0, 0.0).astype(BF16)
        chosen = jnp.where(slot >= 0.0, 1.0, 0.0).astype(BF16)
        lo_ref[...] = jnp.dot(chosen, before, preferred_element_type=F32).astype(jnp.int32)

    @pl.when(s < N_SR_CTX)
    def _():
        outs = [_select_request(affT_ref[:, r * L_CTX:(r + 1) * L_CTX], CAP_CTX, float(r * CAP_CTX))
                for r in range(CTX_PER_SR)]
        emit(jnp.concatenate([o[0] for o in outs], axis=1), jnp.concatenate([o[1] for o in outs], axis=1))

    @pl.when(s >= N_SR_CTX)
    def _():
        emit(*_select_request(affT_ref[...], CAP_LAT, 0.0))


def _select(affT):
    return pl.pallas_call(
        _select_kernel,
        out_shape=(
            jax.ShapeDtypeStruct((N_EXPERTS, R), jnp.int32),
            jax.ShapeDtypeStruct((R, 128), jnp.int32),
            jax.ShapeDtypeStruct((R, 128), F32),
            jax.ShapeDtypeStruct((N_SR, N_EXPERTS, 128), jnp.int32),
        ),
        grid=(N_SR,),
        in_specs=[pl.BlockSpec((N_EXPERTS, SR_TOKENS), lambda s: (0, s))],
        out_specs=(
            pl.BlockSpec((N_EXPERTS, SR_TOKENS), lambda s: (0, s)),
            pl.BlockSpec((SR_TOKENS, 128), lambda s: (s, 0)),
            pl.BlockSpec((SR_TOKENS, 128), lambda s: (s, 0)),
            pl.BlockSpec((None, N_EXPERTS, 128), lambda s: (s, 0, 0)),
        ),
        compiler_params=_cparams(("arbitrary",)),
        name="select",
    )(affT)


def _lo_at(lo_ref, sr, e, j):
    return lo_ref[(sr * N_EXPERTS + e) * LO_COLS + j]


def _gather_kernel(lo_ref, hn_ref, slotT_ref, xe_ref):
    sr = pl.program_id(0)
    e0 = pl.program_id(1) * E_HALF
    starts = {}
    fits = None
    for e in range(E_HALF):
        for j in range(N_BLK):
            start = jnp.minimum((_lo_at(lo_ref, sr, e0 + e, j) // BF16_ROWS) * BF16_ROWS, SR_SLOTS - GW_ROWS)
            ok = _lo_at(lo_ref, sr, e0 + e, j + 1) - start <= GW_ROWS
            fits = ok if fits is None else jnp.logical_and(fits, ok)
            starts[e, j] = start

    @pl.when(fits)
    def _():
        xe_ref[...] = jnp.zeros(xe_ref.shape, BF16)
        w_iota = lax.broadcasted_iota(jnp.int32, (GW_ROWS, BLK), 0)
        for j in range(N_BLK):
            rows = [jnp.where(w_iota == slotT_ref[e:e + 1, j * BLK:(j + 1) * BLK] - starts[e, j], 1.0, 0.0).astype(BF16)
                    for e in range(E_HALF)]
            part = jnp.dot(jnp.concatenate(rows, axis=0), hn_ref[j * BLK:(j + 1) * BLK, :],
                           preferred_element_type=F32).astype(BF16)
            for e in range(E_HALF):
                win = pl.ds(pl.multiple_of(starts[e, j], BF16_ROWS), GW_ROWS)
                xe_ref[e, win, :] = xe_ref[e, win, :] + part[e * GW_ROWS:(e + 1) * GW_ROWS, :]

    @pl.when(jnp.logical_not(fits))
    def _():
        s_iota = lax.broadcasted_iota(jnp.int32, (SR_SLOTS, SR_TOKENS), 0)
        for e in range(E_HALF):
            onehot = jnp.where(s_iota == slotT_ref[e:e + 1, :], 1.0, 0.0).astype(BF16)
            xe_ref[e] = jnp.dot(onehot, hn_ref[...], preferred_element_type=F32).astype(BF16)


def _gather(lo_flat, hn, slotT):
    return pl.pallas_call(
        _gather_kernel,
        out_shape=jax.ShapeDtypeStruct((N_EXPERTS, N_SR * SR_SLOTS, D), BF16),
        grid_spec=pltpu.PrefetchScalarGridSpec(
            num_scalar_prefetch=1,
            grid=(N_SR, N_EXPERTS // E_HALF),
            in_specs=[
                pl.BlockSpec((SR_TOKENS, D), lambda s, h, lo: (s, 0)),
                pl.BlockSpec((E_HALF, SR_TOKENS), lambda s, h, lo: (h, s)),
            ],
            out_specs=pl.BlockSpec((E_HALF, SR_SLOTS, D), lambda s, h, lo: (h, s, 0)),
        ),
        compiler_params=_cparams(("arbitrary", "arbitrary")),
        name="gather",
    )(lo_flat, hn, slotT)


def _ffn_kernel(xe_ref, wg_ref, wu_ref, wd_ref, ye_ref):
    xe = xe_ref[...]
    hg = jnp.dot(xe, wg_ref[...].astype(BF16), preferred_element_type=F32)
    hu = jnp.dot(xe, wu_ref[...].astype(BF16), preferred_element_type=F32)
    hid = (_silu(hg) * hu).astype(BF16)
    ye_ref[...] = jnp.dot(hid, wd_ref[...].astype(BF16), preferred_element_type=F32).astype(BF16)


def _ffn(xe, wg, wu, wd, layer):
    m = xe.shape[1]
    w_spec = pl.BlockSpec((None, None, D, D), lambda e: (layer, e, 0, 0))
    return pl.pallas_call(
        _ffn_kernel,
        out_shape=jax.ShapeDtypeStruct((N_EXPERTS, m, D), BF16),
        grid=(N_EXPERTS,),
        in_specs=[pl.BlockSpec((None, m, D), lambda e: (e, 0, 0)), w_spec, w_spec, w_spec],
        out_specs=pl.BlockSpec((None, m, D), lambda e: (e, 0, 0)),
        compiler_params=_cparams(("arbitrary",)),
        name="expert_ffn",
    )(xe, wg, wu, wd)


def _combine_kernel(lo_ref, x_ref, ye_ref, slot_ref, gate_ref, mod_ref, fnw_ref, *rest, final):
    if final:
        oc_ref, ol_ref, yw_ref, acc_ref = rest
    else:
        o_ref, yw_ref, acc_ref = rest
    sr = pl.program_id(0)
    j = pl.program_id(1)
    starts = []
    fits = None
    for e in range(N_EXPERTS):
        start = jnp.minimum((_lo_at(lo_ref, sr, e, j) // BF16_ROWS) * BF16_ROWS, SR_SLOTS - CW_ROWS)
        ok = _lo_at(lo_ref, sr, e, j + 1) - start <= CW_ROWS
        fits = ok if fits is None else jnp.logical_and(fits, ok)
        starts.append(start)
    slot = slot_ref[...]
    gate = gate_ref[...]

    @pl.when(fits)
    def _():
        for e in range(N_EXPERTS):
            yw_ref[e * CW_ROWS:(e + 1) * CW_ROWS, :] = ye_ref[e, pl.ds(pl.multiple_of(starts[e], BF16_ROWS), CW_ROWS), :]
        n_k = N_EXPERTS * CW_ROWS
        k_exp = lax.broadcasted_iota(jnp.int32, (1, n_k), 1) // CW_ROWS
        k_row = (lax.broadcasted_iota(jnp.int32, (1, n_k), 1) % CW_ROWS).astype(F32)
        start_k = jnp.zeros((1, n_k), F32)
        for e in range(N_EXPERTS):
            start_k = jnp.where(k_exp == e, starts[e].astype(F32), start_k)
        spread = jnp.where(lax.broadcasted_iota(jnp.int32, (128, n_k), 0) == k_exp, 1.0, 0.0).astype(BF16)
        slot_k = jnp.dot(slot.astype(F32).astype(BF16), spread, preferred_element_type=F32)
        hit = slot_k - start_k == k_row
        gate_k = jnp.dot(gate.astype(BF16), spread, preferred_element_type=F32)
        acc_ref[...] = jnp.dot(jnp.where(hit, gate_k, 0.0).astype(BF16), yw_ref[...], preferred_element_type=F32)

    @pl.when(jnp.logical_not(fits))
    def _():
        s_iota = lax.broadcasted_iota(jnp.int32, (BLK, SR_SLOTS), 1)
        acc = jnp.zeros((BLK, D), F32)
        for e in range(N_EXPERTS):
            onehot = jnp.where(slot[:, e:e + 1] == s_iota, 1.0, 0.0).astype(BF16)
            acc = acc + gate[:, e:e + 1] * jnp.dot(onehot, ye_ref[e], preferred_element_type=F32)
        acc_ref[...] = acc

    out = x_ref[...] + mod_ref[5:6, :] * acc_ref[...]
    if not final:
        o_ref[...] = out
    else:
        out = out * lax.rsqrt(jnp.mean(out * out, axis=-1, keepdims=True) + EPS) * fnw_ref[...]
        ol_ref[...] = out

        @pl.when(sr < N_SR_CTX)
        def _():
            oc_ref[...] = out


def _combine(lo_flat, x, ye, slot_tok, gate_tok, mod_l, fnw, *, final):
    blk_of = lambda s, j: s * N_BLK + j
    n_ctx_blk = R_CTX // BLK
    if final:
        out_shape = (jax.ShapeDtypeStruct((R_CTX, D), F32), jax.ShapeDtypeStruct((R_LAT, D), F32))
        out_specs = (pl.BlockSpec((BLK, D), lambda s, j, lo: (jnp.minimum(blk_of(s, j), n_ctx_blk - 1), 0)),
                     pl.BlockSpec((BLK, D), lambda s, j, lo: (jnp.maximum(blk_of(s, j) - n_ctx_blk, 0), 0)))
    else:
        out_shape = jax.ShapeDtypeStruct((R, D), F32)
        out_specs = pl.BlockSpec((BLK, D), lambda s, j, lo: (blk_of(s, j), 0))
    return pl.pallas_call(
        functools.partial(_combine_kernel, final=final),
        out_shape=out_shape,
        grid_spec=pltpu.PrefetchScalarGridSpec(
            num_scalar_prefetch=1,
            grid=(N_SR, N_BLK),
            in_specs=[
                pl.BlockSpec((BLK, D), lambda s, j, lo: (blk_of(s, j), 0)),
                pl.BlockSpec((N_EXPERTS, SR_SLOTS, D), lambda s, j, lo: (0, s, 0)),
                pl.BlockSpec((BLK, 128), lambda s, j, lo: (blk_of(s, j), 0)),
                pl.BlockSpec((BLK, 128), lambda s, j, lo: (blk_of(s, j), 0)),
                pl.BlockSpec((None, MOD_CHUNKS, D),
                             lambda s, j, lo: (_cond_of_tile(blk_of(s, j) * BLK // TM), 0, 0)),
                pl.BlockSpec((1, D), lambda s, j, lo: (0, 0)),
            ],
            out_specs=out_specs,
            scratch_shapes=[pltpu.VMEM((N_EXPERTS * CW_ROWS, D), BF16), pltpu.VMEM((BLK, D), F32)],
        ),
        compiler_params=_cparams(("arbitrary", "arbitrary")),
        name="combine",
    )(lo_flat, x, ye, slot_tok, gate_tok, mod_l, fnw)


def _moe(x, mod_l, nw, wrT, wg, wu, wd, fnw, layer, *, final):
    hn, affT = _router(x, mod_l, nw, wrT)
    slotT, slot_tok, gate_tok, lo = _select(affT)
    lo_flat = lo[:, :, :LO_COLS].reshape(-1)
    xe = _gather(lo_flat, hn, slotT)
    ye = _ffn(xe, wg, wu, wd, layer)
    return _combine(lo_flat, x, ye, slot_tok, gate_tok, mod_l, fnw, final=final)


def _group_major(p):
    return p.reshape(2, GROUPS, HPG).transpose(1, 0, 2).reshape(GROUPS, 2 * HPG)


def kernel(x_prompt, x_sample, state_ssm, c, c_ctx, norm1_w, norm2_w, w_mod, b_mod, conv_in_w, conv_w, conv_out_w, ssd_in_w, ssd_conv_w, ssd_conv_b, ssd_dt_bias, ssd_a_log, ssd_d, ssd_norm_w, ssd_out_w, router_w, exp_w_gate, exp_w_up, exp_w_down, final_norm_w):
    x = (x_prompt.reshape(R_CTX, D), x_sample.reshape(R_LAT, D))
    cond = jnp.concatenate([c_ctx[None, :], c, jnp.zeros((8 - N_COND, D), F32)], axis=0)
    mods = _modulation(cond.T, w_mod, b_mod)
    mods = mods[:, :N_COND].reshape(DEPTH, N_COND, MOD_CHUNKS, D)
    fnw = final_norm_w.reshape(1, D)
    conv_in_b, conv_out_b = conv_in_w.astype(BF16), conv_out_w.astype(BF16)
    ssd_in_b, ssd_out_b = ssd_in_w.astype(BF16), ssd_out_w.astype(BF16)
    ssd_conv_b3 = ssd_conv_b.reshape(DEPTH // 2, 1, CONV_DIM)
    ssd_norm_w3 = ssd_norm_w.reshape(DEPTH // 2, 1, D_INNER)

    states = None
    for layer in range(DEPTH):
        mod_l = mods[layer]
        j = layer // 2
        nw1 = norm1_w[layer].reshape(1, D)
        if layer % 2 == 0:
            x = _conv_mixer(x, mod_l, nw1, conv_in_b, conv_w, conv_out_b, j)
        else:
            w_dt = ssd_in_b[j, :, D_INNER + CONV_DIM:].reshape(D, 2, GROUPS, HPG).transpose(0, 2, 1, 3)
            w_dt_rep = jnp.broadcast_to(w_dt.reshape(D, GROUPS, 1, 2 * HPG), (D, GROUPS, N_SPLIT, 2 * HPG))
            dtb = _group_major(ssd_dt_bias[j])
            dtb_rep = jnp.tile(dtb, (1, N_SPLIT)).reshape(1, GROUPS * DT_LANES)
            z, xs, bm, cm, dt4, dtT4 = _ssd_in(
                x, mod_l, nw1, ssd_in_b, w_dt_rep.reshape(D, GROUPS * DT_LANES), w_dt.reshape(D, 2 * HEADS).T,
                ssd_conv_w, ssd_conv_b3, dtb_rep, dtb.reshape(2 * HEADS, 1), j)
            alog4 = _group_major(ssd_a_log[j])
            dsum = (ssd_d[j][0] + ssd_d[j][1]).reshape(GROUPS, HPG)
            dsk4 = jnp.tile(jnp.concatenate([jnp.zeros_like(dsum), dsum], axis=1), (1, N_SPLIT))
            scan_args = (xs.reshape(R // CHUNK, CHUNK, D_INNER), bm.reshape(R // CHUNK, CHUNK, GROUPS * STATE),
                         cm.reshape(R // CHUNK, CHUNK, GROUPS * STATE), dt4, dtT4,
                         jnp.tile(alog4, (1, N_SPLIT)).reshape(GROUPS, 1, DT_LANES),
                         alog4.reshape(GROUPS, 2 * HPG, 1), dsk4.reshape(GROUPS, 1, DT_LANES))
            y_ctx, states = _ssd_scan(*scan_args, None, states, n_seq=N_CTX, seq_len=L_CTX, row_block0=0,
                                      state_layer=j)
            h0 = state_ssm[:, j].reshape(N_LAT, 2, HEADS * HEADDIM, STATE)
            y_lat = _ssd_scan(*scan_args, h0, None, n_seq=N_LAT, seq_len=L_LAT, row_block0=R_CTX // L_LAT,
                              state_layer=None)
            x = _ssd_out(x, y_ctx.reshape(R_CTX, D_INNER), y_lat.reshape(R_LAT, D_INNER), z, mod_l,
                         ssd_norm_w3, ssd_out_b, j)
        x = _moe(x, mod_l, norm2_w[layer].reshape(1, D), router_w[layer].T, exp_w_gate, exp_w_up, exp_w_down, fnw,
                 layer, final=(layer == DEPTH - 1))

    y_prompt, y_sample = x
    return (y_prompt.reshape(N_CTX, L_CTX, D), y_sample.reshape(N_LAT, L_LAT, D),
            states.reshape(N_CTX, DEPTH // 2, 2, HEADS, HEADDIM, STATE))
```

```python
import functools

import jax
import jax.numpy as jnp
import numpy as np
from jax import lax
from jax.experimental import pallas as pl
from jax.experimental.pallas import tpu as pltpu

F32 = jnp.float32
BF16 = jnp.bfloat16
LOG2E = 1.4426950408889634

D = 1024
DEPTH = 4
N_CTX, L_CTX = 16, 256
N_LAT, L_LAT = 2, 2048
GRID_W = 64
R_CTX = N_CTX * L_CTX
R_LAT = N_LAT * L_LAT
R = R_CTX + R_LAT
N_COND = 1 + N_LAT
MOD_CHUNKS = 6
D_INNER = 2048
HEADDIM = 64
HEADS = 32
GROUPS = 4
HPG = HEADS // GROUPS
GW = HPG * HEADDIM
STATE = 128
CHUNK = 128
CONV_DIM = D_INNER + 2 * GROUPS * STATE
N_SPLIT = 2
DT_LANES = N_SPLIT * 2 * HPG
N_EXPERTS = 16
EPS = 1e-6

TM = 512
TM_OUT = 512
CTX_TILES = R_CTX // TM
LAT_TILES_PER_REQ = L_LAT // TM
SR_TOKENS = 2048
N_SR = R // SR_TOKENS
N_SR_CTX = R_CTX // SR_TOKENS
SR_SLOTS = 2 * SR_TOKENS // N_EXPERTS
CTX_PER_SR = SR_TOKENS // L_CTX
CAP_CTX = 2 * L_CTX // N_EXPERTS
CAP_LAT = 2 * L_LAT // N_EXPERTS
BLK = 256
N_BLK = SR_TOKENS // BLK
LO_COLS = 16
BF16_ROWS = 16
GW_ROWS = 64
CW_ROWS = 64
E_HALF = N_EXPERTS // 2
VMEM_LIMIT = 56 * 1024 * 1024


def _cparams(sem):
    return pltpu.CompilerParams(dimension_semantics=sem, vmem_limit_bytes=VMEM_LIMIT)


def _cond_of_tile(i):
    return jnp.where(i < CTX_TILES, 0, 1 + (i - CTX_TILES) // LAT_TILES_PER_REQ)


def _silu(v):
    h = 0.5 * v
    return h + h * jnp.tanh(h)


def _softplus(v):
    return jnp.maximum(v, 0.0) + jnp.log1p(jnp.exp(-jnp.abs(v)))


def _norm_mod(x, nw, scale, shift):
    y = x * lax.rsqrt(jnp.mean(x * x, axis=-1, keepdims=True) + EPS)
    return y * nw * (1.0 + scale) + shift


def _conv3_rows(p, w_ref, tile_idx):
    n = p.shape[0]
    period = jnp.where(tile_idx < CTX_TILES, L_CTX, GRID_W)
    r = lax.broadcasted_iota(jnp.int32, (n, 1), 0) & (period - 1)
    prev = jnp.where(r == 0, 0.0, pltpu.roll(p, 1, axis=0))
    nxt = jnp.where(r == period - 1, 0.0, pltpu.roll(p, n - 1, axis=0))
    return prev * w_ref[0:1, :] + p * w_ref[1:2, :] + nxt * w_ref[2:3, :]


def _split_pieces(v):
    pieces = []
    r = v
    for _ in range(N_SPLIT):
        p = r.astype(BF16)
        pieces.append(p)
        r = r - p.astype(F32)
    return pieces


def _modulation_kernel(condT_ref, w_ref, b_ref, o_ref):
    s = _silu(condT_ref[...])
    w = w_ref[...]
    rows = [jnp.sum(w * s[:, r:r + 1], axis=0, keepdims=True) + b_ref[...] for r in range(N_COND)]
    rows.append(jnp.zeros((8 - N_COND, w.shape[1]), F32))
    o_ref[...] = jnp.concatenate(rows, axis=0)


def _modulation(condT, w_mod, b_mod):
    tn = 1536
    n = MOD_CHUNKS * D
    return pl.pallas_call(
        _modulation_kernel,
        out_shape=jax.ShapeDtypeStruct((DEPTH, 8, n), F32),
        grid=(DEPTH, n // tn),
        in_specs=[
            pl.BlockSpec((D, 8), lambda l, j: (0, 0)),
            pl.BlockSpec((None, D, tn), lambda l, j: (l, 0, j)),
            pl.BlockSpec((None, 1, tn), lambda l, j: (l, 0, j)),
        ],
        out_specs=pl.BlockSpec((None, 8, tn), lambda l, j: (l, 0, j)),
        compiler_params=_cparams(("arbitrary", "arbitrary")),
        name="modulation",
    )(condT, w_mod, b_mod.reshape(DEPTH, 1, n))


def _conv_mixer_kernel(*refs, split_input):
    i = pl.program_id(0)
    if split_input:
        xc_ref, xl_ref, mod_ref, nw_ref, win_ref, cw_ref, wout_ref, o_ref = refs
        x = jnp.where(i < CTX_TILES, xc_ref[...], xl_ref[...])
    else:
        x_ref, mod_ref, nw_ref, win_ref, cw_ref, wout_ref, o_ref = refs
        x = x_ref[...]
    hn = _norm_mod(x, nw_ref[...], mod_ref[1:2, :], mod_ref[0:1, :]).astype(BF16)
    gb = jnp.dot(hn, win_ref[:, 0:D], preferred_element_type=F32)
    gc = jnp.dot(hn, win_ref[:, D:2 * D], preferred_element_type=F32)
    v = jnp.dot(hn, win_ref[:, 2 * D:3 * D], preferred_element_type=F32)
    q = (gb * _conv3_rows(gc * v, cw_ref, i)).astype(BF16)
    mix = jnp.dot(q, wout_ref[...], preferred_element_type=F32)
    o_ref[...] = x + mod_ref[2:3, :] * mix


def _conv_mixer(xs, mod_l, nw, w_in, cw, w_out, j):
    split_input = isinstance(xs, tuple)
    if split_input:
        x_specs = [pl.BlockSpec((TM, D), lambda i: (jnp.minimum(i, CTX_TILES - 1), 0)),
                   pl.BlockSpec((TM, D), lambda i: (jnp.maximum(i - CTX_TILES, 0), 0))]
    else:
        xs = (xs,)
        x_specs = [pl.BlockSpec((TM, D), lambda i: (i, 0))]
    return pl.pallas_call(
        functools.partial(_conv_mixer_kernel, split_input=split_input),
        out_shape=jax.ShapeDtypeStruct((R, D), F32),
        grid=(R // TM,),
        in_specs=x_specs + [
            pl.BlockSpec((None, MOD_CHUNKS, D), lambda i: (_cond_of_tile(i), 0, 0)),
            pl.BlockSpec((1, D), lambda i: (0, 0)),
            pl.BlockSpec((None, D, 3 * D), lambda i: (j, 0, 0)),
            pl.BlockSpec((None, 3, D), lambda i: (j, 0, 0)),
            pl.BlockSpec((None, D, D), lambda i: (j, 0, 0)),
        ],
        out_specs=pl.BlockSpec((TM, D), lambda i: (i, 0)),
        compiler_params=_cparams(("arbitrary",)),
        name="conv_mixer",
    )(*xs, mod_l, nw, w_in, cw, w_out)


def _ssd_in_kernel(x_ref, mod_ref, nw_ref, win_ref, wdt_ref, wdtT_ref, cw_ref, cb_ref, dtb_ref, dtbT_ref,
                   z_ref, xs_ref, bm_ref, cm_ref, dt_ref, dtT_ref):
    i = pl.program_id(0)
    hn = _norm_mod(x_ref[...], nw_ref[...], mod_ref[1:2, :], mod_ref[0:1, :]).astype(BF16)
    for k in range(D_INNER // D):
        z_ref[:, k * D:(k + 1) * D] = jnp.dot(hn, win_ref[:, k * D:(k + 1) * D], preferred_element_type=F32)
    for k in range(CONV_DIM // D):
        lo = D_INNER + k * D
        u = jnp.dot(hn, win_ref[:, lo:lo + D], preferred_element_type=F32)
        u = _silu(_conv3_rows(u, cw_ref.at[:, k * D:(k + 1) * D], i) + cb_ref[:, k * D:(k + 1) * D])
        if k < D_INNER // D:
            xs_ref[:, k * D:(k + 1) * D] = u
        else:
            bm_ref[...] = u[:, :GROUPS * STATE]
            cm_ref[...] = u[:, GROUPS * STATE:]
    dt = _softplus(jnp.dot(hn, wdt_ref[...], preferred_element_type=F32) + dtb_ref[...])
    dtT = _softplus(
        lax.dot_general(wdtT_ref[...], hn, (((1,), (1,)), ((), ())), preferred_element_type=F32) + dtbT_ref[...])
    for g in range(GROUPS):
        dt_ref[g] = dt[:, g * DT_LANES:(g + 1) * DT_LANES].reshape(TM // CHUNK, CHUNK, DT_LANES)
        for k in range(TM // CHUNK):
            dtT_ref[g, k] = dtT[g * 2 * HPG:(g + 1) * 2 * HPG, k * CHUNK:(k + 1) * CHUNK]


def _ssd_in(x, mod_l, nw, w_in, w_dt, w_dtT, cw, cb, dtb, dtbT, j):
    n_in = w_in.shape[2]
    nck = TM // CHUNK
    return pl.pallas_call(
        _ssd_in_kernel,
        out_shape=(
            jax.ShapeDtypeStruct((R, D_INNER), F32),
            jax.ShapeDtypeStruct((R, D_INNER), F32),
            jax.ShapeDtypeStruct((R, GROUPS * STATE), F32),
            jax.ShapeDtypeStruct((R, GROUPS * STATE), F32),
            jax.ShapeDtypeStruct((GROUPS, R // CHUNK, CHUNK, DT_LANES), F32),
            jax.ShapeDtypeStruct((GROUPS, R // CHUNK, 2 * HPG, CHUNK), F32),
        ),
        grid=(R // TM,),
        in_specs=[
            pl.BlockSpec((TM, D), lambda i: (i, 0)),
            pl.BlockSpec((None, MOD_CHUNKS, D), lambda i: (_cond_of_tile(i), 0, 0)),
            pl.BlockSpec((1, D), lambda i: (0, 0)),
            pl.BlockSpec((None, D, n_in), lambda i: (j, 0, 0)),
            pl.BlockSpec((D, GROUPS * DT_LANES), lambda i: (0, 0)),
            pl.BlockSpec((2 * HEADS, D), lambda i: (0, 0)),
            pl.BlockSpec((None, 3, CONV_DIM), lambda i: (j, 0, 0)),
            pl.BlockSpec((None, 1, CONV_DIM), lambda i: (j, 0, 0)),
            pl.BlockSpec((1, GROUPS * DT_LANES), lambda i: (0, 0)),
            pl.BlockSpec((2 * HEADS, 1), lambda i: (0, 0)),
        ],
        out_specs=(
            pl.BlockSpec((TM, D_INNER), lambda i: (i, 0)),
            pl.BlockSpec((TM, D_INNER), lambda i: (i, 0)),
            pl.BlockSpec((TM, GROUPS * STATE), lambda i: (i, 0)),
            pl.BlockSpec((TM, GROUPS * STATE), lambda i: (i, 0)),
            pl.BlockSpec((GROUPS, nck, CHUNK, DT_LANES), lambda i: (0, i, 0, 0)),
            pl.BlockSpec((GROUPS, nck, 2 * HPG, CHUNK), lambda i: (0, i, 0, 0)),
        ),
        compiler_params=_cparams(("arbitrary",)),
        name="ssd_in",
    )(x, mod_l, nw, w_in, w_dt, w_dtT, cw, cb, dtb, dtbT)


def _ssd_scan_kernel(*refs, n_chunks, has_h0, n_kept, state_slots):
    emit_state = state_slots is not None
    xs_ref, bm_ref, cm_ref, dt_ref, dtT_ref, alog_ref, alogT_ref, dsk_ref = refs[:8]
    tri_k_ref, tri_r_ref, expf_ref, expb_ref, colsel_ref = refs[8:13]
    k = 13
    h0_ref = None
    if has_h0:
        h0_ref = refs[k]
        k += 1
    k += n_kept
    y_ref = refs[k]
    k += 1
    st_out_ref = None
    if emit_state:
        st_out_ref = refs[k]
        k += 1
    st_ref = refs[k]

    a_row = -jnp.exp(alog_ref[...]) * LOG2E
    a_col = -jnp.exp(alogT_ref[...]) * LOG2E
    row_is_fwd = lax.broadcasted_iota(jnp.int32, (2 * HPG, 1), 0) < HPG
    qi = lax.broadcasted_iota(jnp.int32, (CHUNK, CHUNK), 0)
    si = lax.broadcasted_iota(jnp.int32, (CHUNK, CHUNK), 1)
    lower = si <= qi
    lane = lax.broadcasted_iota(jnp.int32, (1, DT_LANES), 1)
    piece_of_lane = lane // (2 * HPG)
    lane_is_fwd = lane % (2 * HPG) < HPG
    lane_c = lax.broadcasted_iota(jnp.int32, (1, CHUNK), 1)
    first_half = lane_c < HEADDIM

    def lane_pieces(v):
        pieces = _split_pieces(v)
        out = pieces[N_SPLIT - 1]
        for r in range(N_SPLIT - 2, -1, -1):
            out = jnp.where(piece_of_lane == r, pieces[r], out)
        return out

    def expand_many(vs, e):
        out = jnp.dot(jnp.concatenate([lane_pieces(v) for v in vs], axis=0), e, preferred_element_type=F32)
        res, r0 = [], 0
        for v in vs:
            res.append(out[r0:r0 + v.shape[0], :])
            r0 += v.shape[0]
        return res

    for d in range(2):
        if has_h0:
            st_ref[d] = h0_ref[d].T
        else:
            st_ref[d] = jnp.zeros((STATE, GW), F32)

    y_ref[...] = jnp.zeros(y_ref.shape, F32)

    def body(k_, carry):
        cf = k_
        x = xs_ref[cf]
        b = bm_ref[cf]
        cm = cm_ref[cf]
        dt = dt_ref[cf]
        dtT = dtT_ref[cf]
        cr = n_chunks - 1 - k_
        xr = xs_ref[cr]
        br = bm_ref[cr]
        dtr = dt_ref[cr]

        dta_rows = jnp.concatenate(_split_pieces(dt * a_row), axis=0)
        dtaT_lanes = jnp.concatenate(_split_pieces(dtT * a_col), axis=1)
        acs2 = jnp.dot(tri_k_ref[...], dta_rows, preferred_element_type=F32)
        acs_lo, acs_up = acs2[:CHUNK, :], acs2[CHUNK:, :]
        acs_t2 = jnp.dot(dtaT_lanes, tri_r_ref[...], preferred_element_type=F32)
        acs_up_r = jnp.dot(tri_k_ref[CHUNK:, :], jnp.concatenate(_split_pieces(dtr * a_row), axis=0),
                           preferred_element_type=F32)
        r_t = jnp.log(dtT) * LOG2E - jnp.where(row_is_fwd, acs_t2[:, :CHUNK], acs_t2[:, CHUNK:])
        col_b = jnp.dot(lane_pieces(jnp.where(lane_is_fwd, acs_lo, acs_up)), colsel_ref[...],
                        preferred_element_type=F32)

        a_last = acs_lo[CHUNK - 1:CHUNK, :]
        a_tot = acs_up_r[0:1, :]
        cb_diag = jnp.sum(cm * b, axis=1, keepdims=True)
        e_out_f, e_in_f = expand_many([jnp.exp2(acs_lo), jnp.exp2(a_last - acs_lo) * dt], expf_ref[...])
        e_self, e_out_b, e_in_b = expand_many(
            [dsk_ref[...] + cb_diag * dt, jnp.exp2(acs_up_r), jnp.exp2(a_tot - acs_up_r) * dtr], expb_ref[...])
        e_keep_f = e_out_f[CHUNK - 1:CHUNK, :]
        e_keep_b = e_out_b[0:1, :]

        cmb = cm.astype(BF16)
        cb = lax.dot_general(cmb, b.astype(BF16), (((1,), (1,)), ((), ())), preferred_element_type=F32)
        parts = []
        for hp in range(HPG // 2):
            ws = []
            for h in (2 * hp, 2 * hp + 1):
                arg = jnp.where(lower, col_b[:, h * CHUNK:(h + 1) * CHUNK] + r_t[h:h + 1, :],
                                col_b[:, (HPG + h) * CHUNK:(HPG + h + 1) * CHUNK] + r_t[HPG + h:HPG + h + 1, :])
                ws.append((cb * jnp.exp2(arg)).astype(BF16))
            xp = x[:, hp * CHUNK:(hp + 1) * CHUNK]
            x2 = jnp.concatenate([jnp.where(first_half, xp, 0.0), jnp.where(first_half, 0.0, xp)], axis=0)
            parts.append(jnp.dot(jnp.concatenate(ws, axis=1), x2.astype(BF16), preferred_element_type=F32))
        st_f = st_ref[0]
        y = (jnp.concatenate(parts, axis=1) + e_self * x
             + jnp.dot(cmb, st_f.astype(BF16), preferred_element_type=F32) * e_out_f)
        y_ref[cf] = y_ref[cf] + y
        st_ref[0] = st_f * e_keep_f + jnp.dot(
            b.T.astype(BF16), (x * e_in_f).astype(BF16), preferred_element_type=F32)

        st_b = st_ref[1]
        y_ref[cr] = y_ref[cr] + jnp.dot(cm_ref[cr].astype(BF16), st_b.astype(BF16),
                                        preferred_element_type=F32) * e_out_b
        st_ref[1] = st_b * e_keep_b + jnp.dot(
            br.T.astype(BF16), (xr * e_in_b).astype(BF16), preferred_element_type=F32)
        return carry

    lax.fori_loop(0, n_chunks, body, 0)
    if emit_state:
        for slot, own in enumerate(state_slots):
            for d in range(2):
                st_out_ref[slot, d] = st_ref[d].T if own else jnp.zeros((GW, STATE), F32)


def _scan_constants():
    q = np.arange(CHUNK)
    tri_lo = (q[None, :] <= q[:, None]).astype(np.float32)
    tri_up = tri_lo.T
    tri_k = np.concatenate([np.tile(tri_lo, (1, N_SPLIT)), np.tile(tri_up, (1, N_SPLIT))], axis=0)
    tri_r = np.concatenate([np.tile(tri_up, (N_SPLIT, 1)), np.tile(tri_lo, (N_SPLIT, 1))], axis=1)
    head = np.arange(DT_LANES) % (2 * HPG)
    chan_head = np.arange(GW) // HEADDIM
    exp_f = (head[:, None] == chan_head[None, :]).astype(np.float32)
    exp_b = (head[:, None] == chan_head[None, :] + HPG).astype(np.float32)
    col_sel = (head[:, None] == (np.arange(2 * HPG * CHUNK) // CHUNK)[None, :]).astype(np.float32)
    return [jnp.asarray(m, BF16) for m in (tri_k, tri_r, exp_f, exp_b, col_sel)]


def _ssd_scan(xs3, bm3, cm3, dt4, dtT4, alog4, alogT4, dskx, h0, st_prev, *, n_seq, seq_len, row_block0, state_layer):
    nck = seq_len // CHUNK
    has_h0 = h0 is not None
    emit_state = state_layer is not None
    keep_state = st_prev is not None
    rb = lambda s: s + row_block0
    in_specs = [
        pl.BlockSpec((nck, CHUNK, GW), lambda s, g: (rb(s), 0, g)),
        pl.BlockSpec((nck, CHUNK, STATE), lambda s, g: (rb(s), 0, g)),
        pl.BlockSpec((nck, CHUNK, STATE), lambda s, g: (rb(s), 0, g)),
        pl.BlockSpec((None, nck, CHUNK, DT_LANES), lambda s, g: (g, rb(s), 0, 0)),
        pl.BlockSpec((None, nck, 2 * HPG, CHUNK), lambda s, g: (g, rb(s), 0, 0)),
        pl.BlockSpec((None, 1, DT_LANES), lambda s, g: (g, 0, 0)),
        pl.BlockSpec((None, 2 * HPG, 1), lambda s, g: (g, 0, 0)),
        pl.BlockSpec((None, 1, DT_LANES), lambda s, g: (g, 0, 0)),
    ]
    args = [xs3, bm3, cm3, dt4, dtT4, alog4, alogT4, dskx]
    for const in _scan_constants():
        in_specs.append(pl.BlockSpec(const.shape, lambda s, g: (0, 0)))
        args.append(const)
    if has_h0:
        in_specs.append(pl.BlockSpec((None, 2, GW, STATE), lambda s, g: (s, 0, g, 0)))
        args.append(h0)
    aliases = {}
    if keep_state:
        in_specs.append(pl.BlockSpec(memory_space=pl.ANY))
        aliases[len(args)] = 1
        args.append(st_prev)
    y_shape = jax.ShapeDtypeStruct((n_seq * nck, CHUNK, D_INNER), F32)
    y_spec = pl.BlockSpec((nck, CHUNK, GW), lambda s, g: (s, 0, g))
    n_layers = DEPTH // 2
    state_slots = None
    if emit_state:
        out_shape = (y_shape, jax.ShapeDtypeStruct((n_seq, n_layers, 2, HEADS * HEADDIM, STATE), F32))
        if keep_state:
            state_slots = (True,)
            st_spec = pl.BlockSpec((None, 1, 2, GW, STATE), lambda s, g: (s, state_layer, 0, g, 0))
        else:
            state_slots = tuple(l == state_layer for l in range(n_layers))
            st_spec = pl.BlockSpec((None, n_layers, 2, GW, STATE), lambda s, g: (s, 0, 0, g, 0))
        out_specs = (y_spec, st_spec)
    else:
        out_shape, out_specs = y_shape, y_spec
    return pl.pallas_call(
        functools.partial(_ssd_scan_kernel, n_chunks=nck, has_h0=has_h0, n_kept=len(aliases),
                          state_slots=state_slots),
        out_shape=out_shape,
        grid=(n_seq, GROUPS),
        in_specs=in_specs,
        out_specs=out_specs,
        scratch_shapes=[pltpu.VMEM((2, STATE, GW), F32)],
        input_output_aliases=aliases,
        compiler_params=_cparams(("arbitrary", "arbitrary")),
        name="ssd_scan",
    )(*args)


def _ssd_out_kernel(x_ref, yc_ref, yl_ref, z_ref, mod_ref, nw_ref, wout_ref, o_ref):
    y = jnp.where(pl.program_id(0) < R_CTX // TM_OUT, yc_ref[...], yl_ref[...])
    v = y * _silu(z_ref[...])
    v = v * lax.rsqrt(jnp.mean(v * v, axis=-1, keepdims=True) + EPS) * nw_ref[...]
    mix = jnp.dot(v.astype(BF16), wout_ref[...], preferred_element_type=F32)
    o_ref[...] = x_ref[...] + mod_ref[2:3, :] * mix


def _ssd_out(x, y_ctx, y_lat, z, mod_l, nw, w_out, j):
    n_ctx = R_CTX // TM_OUT
    return pl.pallas_call(
        _ssd_out_kernel,
        out_shape=jax.ShapeDtypeStruct((R, D), F32),
        grid=(R // TM_OUT,),
        in_specs=[
            pl.BlockSpec((TM_OUT, D), lambda i: (i, 0)),
            pl.BlockSpec((TM_OUT, D_INNER), lambda i: (jnp.minimum(i, n_ctx - 1), 0)),
            pl.BlockSpec((TM_OUT, D_INNER), lambda i: (jnp.maximum(i - n_ctx, 0), 0)),
            pl.BlockSpec((TM_OUT, D_INNER), lambda i: (i, 0)),
            pl.BlockSpec((None, MOD_CHUNKS, D), lambda i: (_cond_of_tile(i * TM_OUT // TM), 0, 0)),
            pl.BlockSpec((None, 1, D_INNER), lambda i: (j, 0, 0)),
            pl.BlockSpec((None, D_INNER, D), lambda i: (j, 0, 0)),
        ],
        out_specs=pl.BlockSpec((TM_OUT, D), lambda i: (i, 0)),
        compiler_params=_cparams(("arbitrary",)),
        name="ssd_out",
    )(x, y_ctx, y_lat, z, mod_l, nw, w_out)


def _router_kernel(x_ref, mod_ref, nw_ref, wrT_ref, hn_ref, affT_ref):
    hn = _norm_mod(x_ref[...], nw_ref[...], mod_ref[4:5, :], mod_ref[3:4, :])
    hn_hi = hn.astype(BF16)
    hn_ref[...] = hn_hi
    hn_lo = (hn - hn_hi.astype(F32)).astype(BF16)
    w = wrT_ref[...]
    w_hi = w.astype(BF16)
    w_lo = (w - w_hi.astype(F32)).astype(BF16)
    logits = lax.dot_general(jnp.concatenate([w_hi, w_lo, w_hi], axis=1), jnp.concatenate([hn_hi, hn_hi, hn_lo], axis=1),
                             (((1,), (1,)), ((), ())), preferred_element_type=F32)
    e = jnp.exp(logits - jnp.max(logits, axis=0, keepdims=True))
    affT_ref[...] = e / jnp.sum(e, axis=0, keepdims=True)


def _router(x, mod_l, nw, wrT):
    return pl.pallas_call(
        _router_kernel,
        out_shape=(jax.ShapeDtypeStruct((R, D), BF16), jax.ShapeDtypeStruct((N_EXPERTS, R), F32)),
        grid=(R // TM,),
        in_specs=[
            pl.BlockSpec((TM, D), lambda i: (i, 0)),
            pl.BlockSpec((None, MOD_CHUNKS, D), lambda i: (_cond_of_tile(i), 0, 0)),
            pl.BlockSpec((1, D), lambda i: (0, 0)),
            pl.BlockSpec((N_EXPERTS, D), lambda i: (0, 0)),
        ],
        out_specs=(pl.BlockSpec((TM, D), lambda i: (i, 0)), pl.BlockSpec((N_EXPERTS, TM), lambda i: (0, i))),
        compiler_params=_cparams(("arbitrary",)),
        name="router",
    )(x, mod_l, nw, wrT)


def _excl_cumsum_lanes(m):
    blk = 256
    t = m.shape[1]
    a = lax.broadcasted_iota(jnp.int32, (blk, blk), 0)
    b = lax.broadcasted_iota(jnp.int32, (blk, blk), 1)
    strict = jnp.where(a < b, 1.0, 0.0).astype(BF16)
    carry = jnp.zeros((m.shape[0], 1), F32)
    outs = []
    for k in range(t // blk):
        mk = m[:, k * blk:(k + 1) * blk]
        outs.append(jnp.dot(mk.astype(BF16), strict, preferred_element_type=F32) + carry)
        carry = carry + jnp.sum(mk, axis=1, keepdims=True)
    return outs[0] if len(outs) == 1 else jnp.concatenate(outs, axis=1)


def _select_request(aff, cap, base):
    thr_bits = jnp.zeros((N_EXPERTS, 1), jnp.int32)
    for k in range(30, -1, -1):
        trial = thr_bits | (1 << k)
        cnt = jnp.sum(jnp.where(aff >= pltpu.bitcast(trial, F32), 1.0, 0.0), axis=1, keepdims=True)
        thr_bits = jnp.where(cnt >= cap, trial, thr_bits)
    thr = pltpu.bitcast(thr_bits, F32)
    gt = aff > thr
    eq = jnp.where(aff == thr, 1.0, 0.0)
    need = cap - jnp.sum(jnp.where(gt, 1.0, 0.0), axis=1, keepdims=True)
    sel = gt | ((eq > 0.0) & (_excl_cumsum_lanes(eq) < need))
    pos = _excl_cumsum_lanes(jnp.where(sel, 1.0, 0.0))
    return jnp.where(sel, pos + base, -1.0), jnp.where(sel, aff, 0.0)


def _select_kernel(affT_ref, slotT_ref, slot_tok_ref, gate_tok_ref, lo_ref):
    s = pl.program_id(0)

    def emit(slot, gate):
        slotT_ref[...] = slot.astype(jnp.int32)
        pad = jnp.zeros((128 - N_EXPERTS, SR_TOKENS), F32)
        slot_tok_ref[...] = jnp.concatenate([slot, pad - 1.0], axis=0).T.astype(jnp.int32)
        gate_tok_ref[...] = jnp.concatenate([gate, pad], axis=0).T
        t = lax.broadcasted_iota(jnp.int32, (SR_TOKENS, 128), 0)
        jcol = lax.broadcasted_iota(jnp.int32, (SR_TOKENS, 128), 1)
        before = jnp.where(t < jcol * BLK, 1.0, 0.0).astype(BF16)
        chosen = jnp.where(slot >= 0.0, 1.0, 0.0).astype(BF16)
        lo_ref[...] = jnp.dot(chosen, before, preferred_element_type=F32).astype(jnp.int32)

    @pl.when(s < N_SR_CTX)
    def _():
        outs = [_select_request(affT_ref[:, r * L_CTX:(r + 1) * L_CTX], CAP_CTX, float(r * CAP_CTX))
                for r in range(CTX_PER_SR)]
        emit(jnp.concatenate([o[0] for o in outs], axis=1), jnp.concatenate([o[1] for o in outs], axis=1))

    @pl.when(s >= N_SR_CTX)
    def _():
        emit(*_select_request(affT_ref[...], CAP_LAT, 0.0))


def _select(affT):
    return pl.pallas_call(
        _select_kernel,
        out_shape=(
            jax.ShapeDtypeStruct((N_EXPERTS, R), jnp.int32),
            jax.ShapeDtypeStruct((R, 128), jnp.int32),
            jax.ShapeDtypeStruct((R, 128), F32),
            jax.ShapeDtypeStruct((N_SR, N_EXPERTS, 128), jnp.int32),
        ),
        grid=(N_SR,),
        in_specs=[pl.BlockSpec((N_EXPERTS, SR_TOKENS), lambda s: (0, s))],
        out_specs=(
            pl.BlockSpec((N_EXPERTS, SR_TOKENS), lambda s: (0, s)),
            pl.BlockSpec((SR_TOKENS, 128), lambda s: (s, 0)),
            pl.BlockSpec((SR_TOKENS, 128), lambda s: (s, 0)),
            pl.BlockSpec((None, N_EXPERTS, 128), lambda s: (s, 0, 0)),
        ),
        compiler_params=_cparams(("arbitrary",)),
        name="select",
    )(affT)


def _lo_at(lo_ref, sr, e, j):
    return lo_ref[(sr * N_EXPERTS + e) * LO_COLS + j]


def _gather_kernel(lo_ref, hn_ref, slotT_ref, xe_ref):
    sr = pl.program_id(0)
    e0 = pl.program_id(1) * E_HALF
    starts = {}
    fits = None
    for e in range(E_HALF):
        for j in range(N_BLK):
            start = jnp.minimum((_lo_at(lo_ref, sr, e0 + e, j) // BF16_ROWS) * BF16_ROWS, SR_SLOTS - GW_ROWS)
            ok = _lo_at(lo_ref, sr, e0 + e, j + 1) - start <= GW_ROWS
            fits = ok if fits is None else jnp.logical_and(fits, ok)
            starts[e, j] = start

    @pl.when(fits)
    def _():
        xe_ref[...] = jnp.zeros(xe_ref.shape, BF16)
        w_iota = lax.broadcasted_iota(jnp.int32, (GW_ROWS, BLK), 0)
        for j in range(N_BLK):
            rows = [jnp.where(w_iota == slotT_ref[e:e + 1, j * BLK:(j + 1) * BLK] - starts[e, j], 1.0, 0.0).astype(BF16)
                    for e in range(E_HALF)]
            part = jnp.dot(jnp.concatenate(rows, axis=0), hn_ref[j * BLK:(j + 1) * BLK, :],
                           preferred_element_type=F32).astype(BF16)
            for e in range(E_HALF):
                win = pl.ds(pl.multiple_of(starts[e, j], BF16_ROWS), GW_ROWS)
                xe_ref[e, win, :] = xe_ref[e, win, :] + part[e * GW_ROWS:(e + 1) * GW_ROWS, :]

    @pl.when(jnp.logical_not(fits))
    def _():
        s_iota = lax.broadcasted_iota(jnp.int32, (SR_SLOTS, SR_TOKENS), 0)
        for e in range(E_HALF):
            onehot = jnp.where(s_iota == slotT_ref[e:e + 1, :], 1.0, 0.0).astype(BF16)
            xe_ref[e] = jnp.dot(onehot, hn_ref[...], preferred_element_type=F32).astype(BF16)


def _gather(lo_flat, hn, slotT):
    return pl.pallas_call(
        _gather_kernel,
        out_shape=jax.ShapeDtypeStruct((N_EXPERTS, N_SR * SR_SLOTS, D), BF16),
        grid_spec=pltpu.PrefetchScalarGridSpec(
            num_scalar_prefetch=1,
            grid=(N_SR, N_EXPERTS // E_HALF),
            in_specs=[
                pl.BlockSpec((SR_TOKENS, D), lambda s, h, lo: (s, 0)),
                pl.BlockSpec((E_HALF, SR_TOKENS), lambda s, h, lo: (h, s)),
            ],
            out_specs=pl.BlockSpec((E_HALF, SR_SLOTS, D), lambda s, h, lo: (h, s, 0)),
        ),
        compiler_params=_cparams(("arbitrary", "arbitrary")),
        name="gather",
    )(lo_flat, hn, slotT)


def _ffn_kernel(xe_ref, wg_ref, wu_ref, wd_ref, ye_ref):
    xe = xe_ref[...]
    hg = jnp.dot(xe, wg_ref[...].astype(BF16), preferred_element_type=F32)
    hu = jnp.dot(xe, wu_ref[...].astype(BF16), preferred_element_type=F32)
    hid = (_silu(hg) * hu).astype(BF16)
    ye_ref[...] = jnp.dot(hid, wd_ref[...].astype(BF16), preferred_element_type=F32).astype(BF16)


def _ffn(xe, wg, wu, wd, layer):
    m = xe.shape[1]
    w_spec = pl.BlockSpec((None, None, D, D), lambda e: (layer, e, 0, 0))
    return pl.pallas_call(
        _ffn_kernel,
        out_shape=jax.ShapeDtypeStruct((N_EXPERTS, m, D), BF16),
        grid=(N_EXPERTS,),
        in_specs=[pl.BlockSpec((None, m, D), lambda e: (e, 0, 0)), w_spec, w_spec, w_spec],
        out_specs=pl.BlockSpec((None, m, D), lambda e: (e, 0, 0)),
        compiler_params=_cparams(("arbitrary",)),
        name="expert_ffn",
    )(xe, wg, wu, wd)


def _combine_kernel(lo_ref, x_ref, ye_ref, slot_ref, gate_ref, mod_ref, fnw_ref, spread_ref, *rest, final):
    if final:
        oc_ref, ol_ref, yw_ref, acc_ref = rest
    else:
        o_ref, yw_ref, acc_ref = rest
    sr = pl.program_id(0)
    j = pl.program_id(1)
    starts = []
    fits = None
    for e in range(N_EXPERTS):
        start = jnp.minimum((_lo_at(lo_ref, sr, e, j) // BF16_ROWS) * BF16_ROWS, SR_SLOTS - CW_ROWS)
        ok = _lo_at(lo_ref, sr, e, j + 1) - start <= CW_ROWS
        fits = ok if fits is None else jnp.logical_and(fits, ok)
        starts.append(start)
    slot = slot_ref[...]
    gate = gate_ref[...]

    @pl.when(fits)
    def _():
        for e in range(N_EXPERTS):
            yw_ref[e * CW_ROWS:(e + 1) * CW_ROWS, :] = ye_ref[e, pl.ds(pl.multiple_of(starts[e], BF16_ROWS), CW_ROWS), :]
        n_k = N_EXPERTS * CW_ROWS
        k_exp = lax.broadcasted_iota(jnp.int32, (1, n_k), 1) // CW_ROWS
        k_row = (lax.broadcasted_iota(jnp.int32, (1, n_k), 1) % CW_ROWS).astype(F32)
        start_k = jnp.zeros((1, n_k), F32)
        for e in range(N_EXPERTS):
            start_k = jnp.where(k_exp == e, starts[e].astype(F32), start_k)
        spread = spread_ref[...]
        slot_k = jnp.dot(slot.astype(F32).astype(BF16), spread, preferred_element_type=F32)
        hit = slot_k - start_k == k_row
        gate_k = jnp.dot(gate.astype(BF16), spread, preferred_element_type=F32)
        acc_ref[...] = jnp.dot(jnp.where(hit, gate_k, 0.0).astype(BF16), yw_ref[...], preferred_element_type=F32)

    @pl.when(jnp.logical_not(fits))
    def _():
        s_iota = lax.broadcasted_iota(jnp.int32, (BLK, SR_SLOTS), 1)
        acc = jnp.zeros((BLK, D), F32)
        for e in range(N_EXPERTS):
            onehot = jnp.where(slot[:, e:e + 1] == s_iota, 1.0, 0.0).astype(BF16)
            acc = acc + gate[:, e:e + 1] * jnp.dot(onehot, ye_ref[e], preferred_element_type=F32)
        acc_ref[...] = acc

    out = x_ref[...] + mod_ref[5:6, :] * acc_ref[...]
    if not final:
        o_ref[...] = out
    else:
        out = out * lax.rsqrt(jnp.mean(out * out, axis=-1, keepdims=True) + EPS) * fnw_ref[...]
        ol_ref[...] = out

        @pl.when(sr < N_SR_CTX)
        def _():
            oc_ref[...] = out


def _combine(lo_flat, x, ye, slot_tok, gate_tok, mod_l, fnw, *, final):
    blk_of = lambda s, j: s * N_BLK + j
    n_ctx_blk = R_CTX // BLK
    spread = jnp.asarray(np.arange(128)[:, None] == (np.arange(N_EXPERTS * CW_ROWS) // CW_ROWS)[None, :], BF16)
    if final:
        out_shape = (jax.ShapeDtypeStruct((R_CTX, D), F32), jax.ShapeDtypeStruct((R_LAT, D), F32))
        out_specs = (pl.BlockSpec((BLK, D), lambda s, j, lo: (jnp.minimum(blk_of(s, j), n_ctx_blk - 1), 0)),
                     pl.BlockSpec((BLK, D), lambda s, j, lo: (jnp.maximum(blk_of(s, j) - n_ctx_blk, 0), 0)))
    else:
        out_shape = jax.ShapeDtypeStruct((R, D), F32)
        out_specs = pl.BlockSpec((BLK, D), lambda s, j, lo: (blk_of(s, j), 0))
    return pl.pallas_call(
        functools.partial(_combine_kernel, final=final),
        out_shape=out_shape,
        grid_spec=pltpu.PrefetchScalarGridSpec(
            num_scalar_prefetch=1,
            grid=(N_SR, N_BLK),
            in_specs=[
                pl.BlockSpec((BLK, D), lambda s, j, lo: (blk_of(s, j), 0)),
                pl.BlockSpec((N_EXPERTS, SR_SLOTS, D), lambda s, j, lo: (0, s, 0)),
                pl.BlockSpec((BLK, 128), lambda s, j, lo: (blk_of(s, j), 0)),
                pl.BlockSpec((BLK, 128), lambda s, j, lo: (blk_of(s, j), 0)),
                pl.BlockSpec((None, MOD_CHUNKS, D),
                             lambda s, j, lo: (_cond_of_tile(blk_of(s, j) * BLK // TM), 0, 0)),
                pl.BlockSpec((1, D), lambda s, j, lo: (0, 0)),
                pl.BlockSpec((128, N_EXPERTS * CW_ROWS), lambda s, j, lo: (0, 0)),
            ],
            out_specs=out_specs,
            scratch_shapes=[pltpu.VMEM((N_EXPERTS * CW_ROWS, D), BF16), pltpu.VMEM((BLK, D), F32)],
        ),
        compiler_params=_cparams(("arbitrary", "arbitrary")),
        name="combine",
    )(lo_flat, x, ye, slot_tok, gate_tok, mod_l, fnw, spread)


def _moe(x, mod_l, nw, wrT, wg, wu, wd, fnw, layer, *, final):
    hn, affT = _router(x, mod_l, nw, wrT)
    slotT, slot_tok, gate_tok, lo = _select(affT)
    lo_flat = lo[:, :, :LO_COLS].reshape(-1)
    xe = _gather(lo_flat, hn, slotT)
    ye = _ffn(xe, wg, wu, wd, layer)
    return _combine(lo_flat, x, ye, slot_tok, gate_tok, mod_l, fnw, final=final)


def _group_major(p):
    return p.reshape(2, GROUPS, HPG).transpose(1, 0, 2).reshape(GROUPS, 2 * HPG)


def kernel(x_prompt, x_sample, state_ssm, c, c_ctx, norm1_w, norm2_w, w_mod, b_mod, conv_in_w, conv_w, conv_out_w, ssd_in_w, ssd_conv_w, ssd_conv_b, ssd_dt_bias, ssd_a_log, ssd_d, ssd_norm_w, ssd_out_w, router_w, exp_w_gate, exp_w_up, exp_w_down, final_norm_w):
    x = (x_prompt.reshape(R_CTX, D), x_sample.reshape(R_LAT, D))
    cond = jnp.concatenate([c_ctx[None, :], c, jnp.zeros((8 - N_COND, D), F32)], axis=0)
    mods = _modulation(cond.T, w_mod, b_mod)
    mods = mods[:, :N_COND].reshape(DEPTH, N_COND, MOD_CHUNKS, D)
    fnw = final_norm_w.reshape(1, D)
    conv_in_b, conv_out_b = conv_in_w.astype(BF16), conv_out_w.astype(BF16)
    ssd_in_b, ssd_out_b = ssd_in_w.astype(BF16), ssd_out_w.astype(BF16)
    ssd_conv_b3 = ssd_conv_b.reshape(DEPTH // 2, 1, CONV_DIM)
    ssd_norm_w3 = ssd_norm_w.reshape(DEPTH // 2, 1, D_INNER)

    states = None
    for layer in range(DEPTH):
        mod_l = mods[layer]
        j = layer // 2
        nw1 = norm1_w[layer].reshape(1, D)
        if layer % 2 == 0:
            x = _conv_mixer(x, mod_l, nw1, conv_in_b, conv_w, conv_out_b, j)
        else:
            w_dt = ssd_in_b[j, :, D_INNER + CONV_DIM:].reshape(D, 2, GROUPS, HPG).transpose(0, 2, 1, 3)
            w_dt_rep = jnp.broadcast_to(w_dt.reshape(D, GROUPS, 1, 2 * HPG), (D, GROUPS, N_SPLIT, 2 * HPG))
            dtb = _group_major(ssd_dt_bias[j])
            dtb_rep = jnp.tile(dtb, (1, N_SPLIT)).reshape(1, GROUPS * DT_LANES)
            z, xs, bm, cm, dt4, dtT4 = _ssd_in(
                x, mod_l, nw1, ssd_in_b, w_dt_rep.reshape(D, GROUPS * DT_LANES), w_dt.reshape(D, 2 * HEADS).T,
                ssd_conv_w, ssd_conv_b3, dtb_rep, dtb.reshape(2 * HEADS, 1), j)
            alog4 = _group_major(ssd_a_log[j])
            dsum = (ssd_d[j][0] + ssd_d[j][1]).reshape(GROUPS, HPG)
            dsk4 = jnp.tile(jnp.concatenate([jnp.zeros_like(dsum), dsum], axis=1), (1, N_SPLIT))
            scan_args = (xs.reshape(R // CHUNK, CHUNK, D_INNER), bm.reshape(R // CHUNK, CHUNK, GROUPS * STATE),
                         cm.reshape(R // CHUNK, CHUNK, GROUPS * STATE), dt4, dtT4,
                         jnp.tile(alog4, (1, N_SPLIT)).reshape(GROUPS, 1, DT_LANES),
                         alog4.reshape(GROUPS, 2 * HPG, 1), dsk4.reshape(GROUPS, 1, DT_LANES))
            y_ctx, states = _ssd_scan(*scan_args, None, states, n_seq=N_CTX, seq_len=L_CTX, row_block0=0,
                                      state_layer=j)
            h0 = state_ssm[:, j].reshape(N_LAT, 2, HEADS * HEADDIM, STATE)
            y_lat = _ssd_scan(*scan_args, h0, None, n_seq=N_LAT, seq_len=L_LAT, row_block0=R_CTX // L_LAT,
                              state_layer=None)
            x = _ssd_out(x, y_ctx.reshape(R_CTX, D_INNER), y_lat.reshape(R_LAT, D_INNER), z, mod_l,
                         ssd_norm_w3, ssd_out_b, j)
        x = _moe(x, mod_l, norm2_w[layer].reshape(1, D), router_w[layer].T, exp_w_gate, exp_w_up, exp_w_down, fnw,
                 layer, final=(layer == DEPTH - 1))

    y_prompt, y_sample = x
    return (y_prompt.reshape(N_CTX, L_CTX, D), y_sample.reshape(N_LAT, L_LAT, D),
            states.reshape(N_CTX, DEPTH // 2, 2, HEADS, HEADDIM, STATE))
```

```python
import functools

import jax
import jax.numpy as jnp
import numpy as np
from jax import lax
from jax.experimental import pallas as pl
from jax.experimental.pallas import tpu as pltpu

F32 = jnp.float32
BF16 = jnp.bfloat16
LOG2E = 1.4426950408889634

D = 1024
DEPTH = 4
N_CTX, L_CTX = 16, 256
N_LAT, L_LAT = 2, 2048
GRID_W = 64
R_CTX = N_CTX * L_CTX
R_LAT = N_LAT * L_LAT
R = R_CTX + R_LAT
N_COND = 1 + N_LAT
MOD_CHUNKS = 6
D_INNER = 2048
HEADDIM = 64
HEADS = 32
GROUPS = 4
HPG = HEADS // GROUPS
GW = HPG * HEADDIM
STATE = 128
CHUNK = 128
CONV_DIM = D_INNER + 2 * GROUPS * STATE
N_SPLIT = 2
DT_LANES = N_SPLIT * 2 * HPG
N_EXPERTS = 16
EPS = 1e-6

TM = 512
TM_OUT = 512
CTX_TILES = R_CTX // TM
LAT_TILES_PER_REQ = L_LAT // TM
SR_TOKENS = 2048
N_SR = R // SR_TOKENS
N_SR_CTX = R_CTX // SR_TOKENS
SR_SLOTS = 2 * SR_TOKENS // N_EXPERTS
CTX_PER_SR = SR_TOKENS // L_CTX
CAP_CTX = 2 * L_CTX // N_EXPERTS
CAP_LAT = 2 * L_LAT // N_EXPERTS
BLK = 256
N_BLK = SR_TOKENS // BLK
LO_COLS = 16
BF16_ROWS = 16
GW_ROWS = 64
CW_ROWS = 64
E_STEP = N_EXPERTS
VMEM_LIMIT = 56 * 1024 * 1024


def _cparams(sem):
    return pltpu.CompilerParams(dimension_semantics=sem, vmem_limit_bytes=VMEM_LIMIT)


def _cond_of_tile(i):
    return jnp.where(i < CTX_TILES, 0, 1 + (i - CTX_TILES) // LAT_TILES_PER_REQ)


def _silu(v):
    h = 0.5 * v
    return h + h * jnp.tanh(h)


def _softplus(v):
    return jnp.maximum(v, 0.0) + jnp.log1p(jnp.exp(-jnp.abs(v)))


def _norm_mod(x, nw, scale, shift):
    y = x * lax.rsqrt(jnp.mean(x * x, axis=-1, keepdims=True) + EPS)
    return y * nw * (1.0 + scale) + shift


def _conv3_rows(p, w_ref, tile_idx):
    n = p.shape[0]
    period = jnp.where(tile_idx < CTX_TILES, L_CTX, GRID_W)
    r = lax.broadcasted_iota(jnp.int32, (n, 1), 0) & (period - 1)
    prev = jnp.where(r == 0, 0.0, pltpu.roll(p, 1, axis=0))
    nxt = jnp.where(r == period - 1, 0.0, pltpu.roll(p, n - 1, axis=0))
    return prev * w_ref[0:1, :] + p * w_ref[1:2, :] + nxt * w_ref[2:3, :]


def _split_pieces(v):
    pieces = []
    r = v
    for _ in range(N_SPLIT):
        p = r.astype(BF16)
        pieces.append(p)
        r = r - p.astype(F32)
    return pieces


def _modulation_kernel(condT_ref, w_ref, b_ref, o_ref):
    s = _silu(condT_ref[...])
    w = w_ref[...]
    rows = [jnp.sum(w * s[:, r:r + 1], axis=0, keepdims=True) + b_ref[...] for r in range(N_COND)]
    rows.append(jnp.zeros((8 - N_COND, w.shape[1]), F32))
    o_ref[...] = jnp.concatenate(rows, axis=0)


def _modulation(condT, w_mod, b_mod):
    tn = 1536
    n = MOD_CHUNKS * D
    return pl.pallas_call(
        _modulation_kernel,
        out_shape=jax.ShapeDtypeStruct((DEPTH, 8, n), F32),
        grid=(DEPTH, n // tn),
        in_specs=[
            pl.BlockSpec((D, 8), lambda l, j: (0, 0)),
            pl.BlockSpec((None, D, tn), lambda l, j: (l, 0, j)),
            pl.BlockSpec((None, 1, tn), lambda l, j: (l, 0, j)),
        ],
        out_specs=pl.BlockSpec((None, 8, tn), lambda l, j: (l, 0, j)),
        compiler_params=_cparams(("arbitrary", "arbitrary")),
        name="modulation",
    )(condT, w_mod, b_mod.reshape(DEPTH, 1, n))


def _conv_mixer_kernel(*refs, split_input):
    i = pl.program_id(0)
    if split_input:
        xc_ref, xl_ref, mod_ref, nw_ref, win_ref, cw_ref, wout_ref, o_ref = refs
        x = jnp.where(i < CTX_TILES, xc_ref[...], xl_ref[...])
    else:
        x_ref, mod_ref, nw_ref, win_ref, cw_ref, wout_ref, o_ref = refs
        x = x_ref[...]
    hn = _norm_mod(x, nw_ref[...], mod_ref[1:2, :], mod_ref[0:1, :]).astype(BF16)
    gb = jnp.dot(hn, win_ref[:, 0:D], preferred_element_type=F32)
    gc = jnp.dot(hn, win_ref[:, D:2 * D], preferred_element_type=F32)
    v = jnp.dot(hn, win_ref[:, 2 * D:3 * D], preferred_element_type=F32)
    q = (gb * _conv3_rows(gc * v, cw_ref, i)).astype(BF16)
    mix = jnp.dot(q, wout_ref[...], preferred_element_type=F32)
    o_ref[...] = x + mod_ref[2:3, :] * mix


def _conv_mixer(xs, mod_l, nw, w_in, cw, w_out, j):
    split_input = isinstance(xs, tuple)
    if split_input:
        x_specs = [pl.BlockSpec((TM, D), lambda i: (jnp.minimum(i, CTX_TILES - 1), 0)),
                   pl.BlockSpec((TM, D), lambda i: (jnp.maximum(i - CTX_TILES, 0), 0))]
    else:
        xs = (xs,)
        x_specs = [pl.BlockSpec((TM, D), lambda i: (i, 0))]
    return pl.pallas_call(
        functools.partial(_conv_mixer_kernel, split_input=split_input),
        out_shape=jax.ShapeDtypeStruct((R, D), F32),
        grid=(R // TM,),
        in_specs=x_specs + [
            pl.BlockSpec((None, MOD_CHUNKS, D), lambda i: (_cond_of_tile(i), 0, 0)),
            pl.BlockSpec((1, D), lambda i: (0, 0)),
            pl.BlockSpec((None, D, 3 * D), lambda i: (j, 0, 0)),
            pl.BlockSpec((None, 3, D), lambda i: (j, 0, 0)),
            pl.BlockSpec((None, D, D), lambda i: (j, 0, 0)),
        ],
        out_specs=pl.BlockSpec((TM, D), lambda i: (i, 0)),
        compiler_params=_cparams(("arbitrary",)),
        name="conv_mixer",
    )(*xs, mod_l, nw, w_in, cw, w_out)


def _ssd_in_kernel(x_ref, mod_ref, nw_ref, win_ref, wdt_ref, wdtT_ref, cw_ref, cb_ref, dtb_ref, dtbT_ref,
                   z_ref, xs_ref, bm_ref, cm_ref, dt_ref, dtT_ref):
    i = pl.program_id(0)
    hn = _norm_mod(x_ref[...], nw_ref[...], mod_ref[1:2, :], mod_ref[0:1, :]).astype(BF16)
    for k in range(D_INNER // D):
        z_ref[:, k * D:(k + 1) * D] = jnp.dot(hn, win_ref[:, k * D:(k + 1) * D], preferred_element_type=F32)
    for k in range(CONV_DIM // D):
        lo = D_INNER + k * D
        u = jnp.dot(hn, win_ref[:, lo:lo + D], preferred_element_type=F32)
        u = _silu(_conv3_rows(u, cw_ref.at[:, k * D:(k + 1) * D], i) + cb_ref[:, k * D:(k + 1) * D])
        if k < D_INNER // D:
            xs_ref[:, k * D:(k + 1) * D] = u
        else:
            bm_ref[...] = u[:, :GROUPS * STATE]
            cm_ref[...] = u[:, GROUPS * STATE:]
    dt = _softplus(jnp.dot(hn, wdt_ref[...], preferred_element_type=F32) + dtb_ref[...])
    dtT = _softplus(
        lax.dot_general(wdtT_ref[...], hn, (((1,), (1,)), ((), ())), preferred_element_type=F32) + dtbT_ref[...])
    for g in range(GROUPS):
        dt_ref[g] = dt[:, g * DT_LANES:(g + 1) * DT_LANES].reshape(TM // CHUNK, CHUNK, DT_LANES)
        for k in range(TM // CHUNK):
            dtT_ref[g, k] = dtT[g * 2 * HPG:(g + 1) * 2 * HPG, k * CHUNK:(k + 1) * CHUNK]


def _ssd_in(x, mod_l, nw, w_in, w_dt, w_dtT, cw, cb, dtb, dtbT, j):
    n_in = w_in.shape[2]
    nck = TM // CHUNK
    return pl.pallas_call(
        _ssd_in_kernel,
        out_shape=(
            jax.ShapeDtypeStruct((R, D_INNER), F32),
            jax.ShapeDtypeStruct((R, D_INNER), F32),
            jax.ShapeDtypeStruct((R, GROUPS * STATE), F32),
            jax.ShapeDtypeStruct((R, GROUPS * STATE), F32),
            jax.ShapeDtypeStruct((GROUPS, R // CHUNK, CHUNK, DT_LANES), F32),
            jax.ShapeDtypeStruct((GROUPS, R // CHUNK, 2 * HPG, CHUNK), F32),
        ),
        grid=(R // TM,),
        in_specs=[
            pl.BlockSpec((TM, D), lambda i: (i, 0)),
            pl.BlockSpec((None, MOD_CHUNKS, D), lambda i: (_cond_of_tile(i), 0, 0)),
            pl.BlockSpec((1, D), lambda i: (0, 0)),
            pl.BlockSpec((None, D, n_in), lambda i: (j, 0, 0)),
            pl.BlockSpec((D, GROUPS * DT_LANES), lambda i: (0, 0)),
            pl.BlockSpec((2 * HEADS, D), lambda i: (0, 0)),
            pl.BlockSpec((None, 3, CONV_DIM), lambda i: (j, 0, 0)),
            pl.BlockSpec((None, 1, CONV_DIM), lambda i: (j, 0, 0)),
            pl.BlockSpec((1, GROUPS * DT_LANES), lambda i: (0, 0)),
            pl.BlockSpec((2 * HEADS, 1), lambda i: (0, 0)),
        ],
        out_specs=(
            pl.BlockSpec((TM, D_INNER), lambda i: (i, 0)),
            pl.BlockSpec((TM, D_INNER), lambda i: (i, 0)),
            pl.BlockSpec((TM, GROUPS * STATE), lambda i: (i, 0)),
            pl.BlockSpec((TM, GROUPS * STATE), lambda i: (i, 0)),
            pl.BlockSpec((GROUPS, nck, CHUNK, DT_LANES), lambda i: (0, i, 0, 0)),
            pl.BlockSpec((GROUPS, nck, 2 * HPG, CHUNK), lambda i: (0, i, 0, 0)),
        ),
        compiler_params=_cparams(("arbitrary",)),
        name="ssd_in",
    )(x, mod_l, nw, w_in, w_dt, w_dtT, cw, cb, dtb, dtbT)


def _ssd_scan_kernel(*refs, n_chunks, has_h0, n_kept, state_slots):
    emit_state = state_slots is not None
    xs_ref, bm_ref, cm_ref, dt_ref, dtT_ref, alog_ref, alogT_ref, dsk_ref = refs[:8]
    tri_k_ref, tri_r_ref, expf_ref, expb_ref, colsel_ref = refs[8:13]
    k = 13
    h0_ref = None
    if has_h0:
        h0_ref = refs[k]
        k += 1
    k += n_kept
    y_ref = refs[k]
    k += 1
    st_out_ref = None
    if emit_state:
        st_out_ref = refs[k]
        k += 1
    st_ref = refs[k]

    a_row = -jnp.exp(alog_ref[...]) * LOG2E
    a_col = -jnp.exp(alogT_ref[...]) * LOG2E
    row_is_fwd = lax.broadcasted_iota(jnp.int32, (2 * HPG, 1), 0) < HPG
    qi = lax.broadcasted_iota(jnp.int32, (CHUNK, CHUNK), 0)
    si = lax.broadcasted_iota(jnp.int32, (CHUNK, CHUNK), 1)
    lower = si <= qi
    lane = lax.broadcasted_iota(jnp.int32, (1, DT_LANES), 1)
    piece_of_lane = lane // (2 * HPG)
    lane_is_fwd = lane % (2 * HPG) < HPG
    lane_c = lax.broadcasted_iota(jnp.int32, (1, CHUNK), 1)
    first_half = lane_c < HEADDIM

    def lane_pieces(v):
        pieces = _split_pieces(v)
        out = pieces[N_SPLIT - 1]
        for r in range(N_SPLIT - 2, -1, -1):
            out = jnp.where(piece_of_lane == r, pieces[r], out)
        return out

    def expand_many(vs, e):
        out = jnp.dot(jnp.concatenate([lane_pieces(v) for v in vs], axis=0), e, preferred_element_type=F32)
        res, r0 = [], 0
        for v in vs:
            res.append(out[r0:r0 + v.shape[0], :])
            r0 += v.shape[0]
        return res

    for d in range(2):
        if has_h0:
            st_ref[d] = h0_ref[d].T
        else:
            st_ref[d] = jnp.zeros((STATE, GW), F32)

    y_ref[...] = jnp.zeros(y_ref.shape, F32)

    def body(k_, carry):
        cf = k_
        x = xs_ref[cf]
        b = bm_ref[cf]
        cm = cm_ref[cf]
        dt = dt_ref[cf]
        dtT = dtT_ref[cf]
        cr = n_chunks - 1 - k_
        xr = xs_ref[cr]
        br = bm_ref[cr]
        dtr = dt_ref[cr]

        dta_rows = jnp.concatenate(_split_pieces(dt * a_row), axis=0)
        dtaT_lanes = jnp.concatenate(_split_pieces(dtT * a_col), axis=1)
        acs2 = jnp.dot(tri_k_ref[...], dta_rows, preferred_element_type=F32)
        acs_lo, acs_up = acs2[:CHUNK, :], acs2[CHUNK:, :]
        acs_t2 = jnp.dot(dtaT_lanes, tri_r_ref[...], preferred_element_type=F32)
        acs_up_r = jnp.dot(tri_k_ref[CHUNK:, :], jnp.concatenate(_split_pieces(dtr * a_row), axis=0),
                           preferred_element_type=F32)
        r_t = jnp.log(dtT) * LOG2E - jnp.where(row_is_fwd, acs_t2[:, :CHUNK], acs_t2[:, CHUNK:])
        col_b = jnp.dot(lane_pieces(jnp.where(lane_is_fwd, acs_lo, acs_up)), colsel_ref[...],
                        preferred_element_type=F32)

        a_last = acs_lo[CHUNK - 1:CHUNK, :]
        a_tot = acs_up_r[0:1, :]
        cb_diag = jnp.sum(cm * b, axis=1, keepdims=True)
        e_out_f, e_in_f = expand_many([jnp.exp2(acs_lo), jnp.exp2(a_last - acs_lo) * dt], expf_ref[...])
        e_self, e_out_b, e_in_b = expand_many(
            [dsk_ref[...] + cb_diag * dt, jnp.exp2(acs_up_r), jnp.exp2(a_tot - acs_up_r) * dtr], expb_ref[...])
        e_keep_f = e_out_f[CHUNK - 1:CHUNK, :]
        e_keep_b = e_out_b[0:1, :]

        cmb = cm.astype(BF16)
        cb = lax.dot_general(cmb, b.astype(BF16), (((1,), (1,)), ((), ())), preferred_element_type=F32)
        parts = []
        for hp in range(HPG // 2):
            ws = []
            for h in (2 * hp, 2 * hp + 1):
                arg = jnp.where(lower, col_b[:, h * CHUNK:(h + 1) * CHUNK] + r_t[h:h + 1, :],
                                col_b[:, (HPG + h) * CHUNK:(HPG + h + 1) * CHUNK] + r_t[HPG + h:HPG + h + 1, :])
                ws.append((cb * jnp.exp2(arg)).astype(BF16))
            xp = x[:, hp * CHUNK:(hp + 1) * CHUNK]
            x2 = jnp.concatenate([jnp.where(first_half, xp, 0.0), jnp.where(first_half, 0.0, xp)], axis=0)
            parts.append(jnp.dot(jnp.concatenate(ws, axis=1), x2.astype(BF16), preferred_element_type=F32))
        st_f = st_ref[0]
        y = (jnp.concatenate(parts, axis=1) + e_self * x
             + jnp.dot(cmb, st_f.astype(BF16), preferred_element_type=F32) * e_out_f)
        y_ref[cf] = y_ref[cf] + y
        st_ref[0] = st_f * e_keep_f + jnp.dot(
            b.T.astype(BF16), (x * e_in_f).astype(BF16), preferred_element_type=F32)

        st_b = st_ref[1]
        y_ref[cr] = y_ref[cr] + jnp.dot(cm_ref[cr].astype(BF16), st_b.astype(BF16),
                                        preferred_element_type=F32) * e_out_b
        st_ref[1] = st_b * e_keep_b + jnp.dot(
            br.T.astype(BF16), (xr * e_in_b).astype(BF16), preferred_element_type=F32)
        return carry

    lax.fori_loop(0, n_chunks, body, 0)
    if emit_state:
        for slot, own in enumerate(state_slots):
            for d in range(2):
                st_out_ref[slot, d] = st_ref[d].T if own else jnp.zeros((GW, STATE), F32)


def _scan_constants():
    q = np.arange(CHUNK)
    tri_lo = (q[None, :] <= q[:, None]).astype(np.float32)
    tri_up = tri_lo.T
    tri_k = np.concatenate([np.tile(tri_lo, (1, N_SPLIT)), np.tile(tri_up, (1, N_SPLIT))], axis=0)
    tri_r = np.concatenate([np.tile(tri_up, (N_SPLIT, 1)), np.tile(tri_lo, (N_SPLIT, 1))], axis=1)
    head = np.arange(DT_LANES) % (2 * HPG)
    chan_head = np.arange(GW) // HEADDIM
    exp_f = (head[:, None] == chan_head[None, :]).astype(np.float32)
    exp_b = (head[:, None] == chan_head[None, :] + HPG).astype(np.float32)
    col_sel = (head[:, None] == (np.arange(2 * HPG * CHUNK) // CHUNK)[None, :]).astype(np.float32)
    return [jnp.asarray(m, BF16) for m in (tri_k, tri_r, exp_f, exp_b, col_sel)]


def _ssd_scan(xs3, bm3, cm3, dt4, dtT4, alog4, alogT4, dskx, h0, st_prev, *, n_seq, seq_len, row_block0, state_layer):
    nck = seq_len // CHUNK
    has_h0 = h0 is not None
    emit_state = state_layer is not None
    keep_state = st_prev is not None
    rb = lambda s: s + row_block0
    in_specs = [
        pl.BlockSpec((nck, CHUNK, GW), lambda s, g: (rb(s), 0, g)),
        pl.BlockSpec((nck, CHUNK, STATE), lambda s, g: (rb(s), 0, g)),
        pl.BlockSpec((nck, CHUNK, STATE), lambda s, g: (rb(s), 0, g)),
        pl.BlockSpec((None, nck, CHUNK, DT_LANES), lambda s, g: (g, rb(s), 0, 0)),
        pl.BlockSpec((None, nck, 2 * HPG, CHUNK), lambda s, g: (g, rb(s), 0, 0)),
        pl.BlockSpec((None, 1, DT_LANES), lambda s, g: (g, 0, 0)),
        pl.BlockSpec((None, 2 * HPG, 1), lambda s, g: (g, 0, 0)),
        pl.BlockSpec((None, 1, DT_LANES), lambda s, g: (g, 0, 0)),
    ]
    args = [xs3, bm3, cm3, dt4, dtT4, alog4, alogT4, dskx]
    for const in _scan_constants():
        in_specs.append(pl.BlockSpec(const.shape, lambda s, g: (0, 0)))
        args.append(const)
    if has_h0:
        in_specs.append(pl.BlockSpec((None, 2, GW, STATE), lambda s, g: (s, 0, g, 0)))
        args.append(h0)
    aliases = {}
    if keep_state:
        in_specs.append(pl.BlockSpec(memory_space=pl.ANY))
        aliases[len(args)] = 1
        args.append(st_prev)
    y_shape = jax.ShapeDtypeStruct((n_seq * nck, CHUNK, D_INNER), F32)
    y_spec = pl.BlockSpec((nck, CHUNK, GW), lambda s, g: (s, 0, g))
    n_layers = DEPTH // 2
    state_slots = None
    if emit_state:
        out_shape = (y_shape, jax.ShapeDtypeStruct((n_seq, n_layers, 2, HEADS * HEADDIM, STATE), F32))
        if keep_state:
            state_slots = (True,)
            st_spec = pl.BlockSpec((None, 1, 2, GW, STATE), lambda s, g: (s, state_layer, 0, g, 0))
        else:
            state_slots = tuple(l == state_layer for l in range(n_layers))
            st_spec = pl.BlockSpec((None, n_layers, 2, GW, STATE), lambda s, g: (s, 0, 0, g, 0))
        out_specs = (y_spec, st_spec)
    else:
        out_shape, out_specs = y_shape, y_spec
    return pl.pallas_call(
        functools.partial(_ssd_scan_kernel, n_chunks=nck, has_h0=has_h0, n_kept=len(aliases),
                          state_slots=state_slots),
        out_shape=out_shape,
        grid=(n_seq, GROUPS),
        in_specs=in_specs,
        out_specs=out_specs,
        scratch_shapes=[pltpu.VMEM((2, STATE, GW), F32)],
        input_output_aliases=aliases,
        compiler_params=_cparams(("arbitrary", "arbitrary")),
        name="ssd_scan",
    )(*args)


def _ssd_out_kernel(x_ref, yc_ref, yl_ref, z_ref, mod_ref, nw_ref, wout_ref, o_ref):
    y = jnp.where(pl.program_id(0) < R_CTX // TM_OUT, yc_ref[...], yl_ref[...])
    v = y * _silu(z_ref[...])
    v = v * lax.rsqrt(jnp.mean(v * v, axis=-1, keepdims=True) + EPS) * nw_ref[...]
    mix = jnp.dot(v.astype(BF16), wout_ref[...], preferred_element_type=F32)
    o_ref[...] = x_ref[...] + mod_ref[2:3, :] * mix


def _ssd_out(x, y_ctx, y_lat, z, mod_l, nw, w_out, j):
    n_ctx = R_CTX // TM_OUT
    return pl.pallas_call(
        _ssd_out_kernel,
        out_shape=jax.ShapeDtypeStruct((R, D), F32),
        grid=(R // TM_OUT,),
        in_specs=[
            pl.BlockSpec((TM_OUT, D), lambda i: (i, 0)),
            pl.BlockSpec((TM_OUT, D_INNER), lambda i: (jnp.minimum(i, n_ctx - 1), 0)),
            pl.BlockSpec((TM_OUT, D_INNER), lambda i: (jnp.maximum(i - n_ctx, 0), 0)),
            pl.BlockSpec((TM_OUT, D_INNER), lambda i: (i, 0)),
            pl.BlockSpec((None, MOD_CHUNKS, D), lambda i: (_cond_of_tile(i * TM_OUT // TM), 0, 0)),
            pl.BlockSpec((None, 1, D_INNER), lambda i: (j, 0, 0)),
            pl.BlockSpec((None, D_INNER, D), lambda i: (j, 0, 0)),
        ],
        out_specs=pl.BlockSpec((TM_OUT, D), lambda i: (i, 0)),
        compiler_params=_cparams(("arbitrary",)),
        name="ssd_out",
    )(x, y_ctx, y_lat, z, mod_l, nw, w_out)


def _router_kernel(x_ref, mod_ref, nw_ref, wrT_ref, hn_ref, affT_ref):
    hn = _norm_mod(x_ref[...], nw_ref[...], mod_ref[4:5, :], mod_ref[3:4, :])
    hn_hi = hn.astype(BF16)
    hn_ref[...] = hn_hi
    hn_lo = (hn - hn_hi.astype(F32)).astype(BF16)
    w = wrT_ref[...]
    w_hi = w.astype(BF16)
    w_lo = (w - w_hi.astype(F32)).astype(BF16)
    logits = lax.dot_general(jnp.concatenate([w_hi, w_lo, w_hi], axis=1), jnp.concatenate([hn_hi, hn_hi, hn_lo], axis=1),
                             (((1,), (1,)), ((), ())), preferred_element_type=F32)
    e = jnp.exp(logits - jnp.max(logits, axis=0, keepdims=True))
    affT_ref[...] = e / jnp.sum(e, axis=0, keepdims=True)


def _router(x, mod_l, nw, wrT):
    return pl.pallas_call(
        _router_kernel,
        out_shape=(jax.ShapeDtypeStruct((R, D), BF16), jax.ShapeDtypeStruct((N_EXPERTS, R), F32)),
        grid=(R // TM,),
        in_specs=[
            pl.BlockSpec((TM, D), lambda i: (i, 0)),
            pl.BlockSpec((None, MOD_CHUNKS, D), lambda i: (_cond_of_tile(i), 0, 0)),
            pl.BlockSpec((1, D), lambda i: (0, 0)),
            pl.BlockSpec((N_EXPERTS, D), lambda i: (0, 0)),
        ],
        out_specs=(pl.BlockSpec((TM, D), lambda i: (i, 0)), pl.BlockSpec((N_EXPERTS, TM), lambda i: (0, i))),
        compiler_params=_cparams(("arbitrary",)),
        name="router",
    )(x, mod_l, nw, wrT)


def _excl_cumsum_lanes(m):
    blk = 256
    t = m.shape[1]
    a = lax.broadcasted_iota(jnp.int32, (blk, blk), 0)
    b = lax.broadcasted_iota(jnp.int32, (blk, blk), 1)
    strict = jnp.where(a < b, 1.0, 0.0).astype(BF16)
    carry = jnp.zeros((m.shape[0], 1), F32)
    outs = []
    for k in range(t // blk):
        mk = m[:, k * blk:(k + 1) * blk]
        outs.append(jnp.dot(mk.astype(BF16), strict, preferred_element_type=F32) + carry)
        carry = carry + jnp.sum(mk, axis=1, keepdims=True)
    return outs[0] if len(outs) == 1 else jnp.concatenate(outs, axis=1)


def _select_request(aff, cap, base):
    thr_bits = jnp.zeros((N_EXPERTS, 1), jnp.int32)
    for k in range(30, -1, -1):
        trial = thr_bits | (1 << k)
        cnt = jnp.sum(jnp.where(aff >= pltpu.bitcast(trial, F32), 1.0, 0.0), axis=1, keepdims=True)
        thr_bits = jnp.where(cnt >= cap, trial, thr_bits)
    thr = pltpu.bitcast(thr_bits, F32)
    gt = aff > thr
    eq = jnp.where(aff == thr, 1.0, 0.0)
    need = cap - jnp.sum(jnp.where(gt, 1.0, 0.0), axis=1, keepdims=True)
    sel = gt | ((eq > 0.0) & (_excl_cumsum_lanes(eq) < need))
    pos = _excl_cumsum_lanes(jnp.where(sel, 1.0, 0.0))
    return jnp.where(sel, pos + base, -1.0), jnp.where(sel, aff, 0.0)


def _select_kernel(affT_ref, slotT_ref, slot_tok_ref, gate_tok_ref, lo_ref):
    s = pl.program_id(0)

    def emit(slot, gate):
        slotT_ref[...] = slot.astype(jnp.int32)
        pad = jnp.zeros((128 - N_EXPERTS, SR_TOKENS), F32)
        slot_tok_ref[...] = jnp.concatenate([slot, pad - 1.0], axis=0).T.astype(jnp.int32)
        gate_tok_ref[...] = jnp.concatenate([gate, pad], axis=0).T
        t = lax.broadcasted_iota(jnp.int32, (SR_TOKENS, 128), 0)
        jcol = lax.broadcasted_iota(jnp.int32, (SR_TOKENS, 128), 1)
        before = jnp.where(t < jcol * BLK, 1.0, 0.0).astype(BF16)
        chosen = jnp.where(slot >= 0.0, 1.0, 0.0).astype(BF16)
        lo_ref[...] = jnp.dot(chosen, before, preferred_element_type=F32).astype(jnp.int32)

    @pl.when(s < N_SR_CTX)
    def _():
        outs = [_select_request(affT_ref[:, r * L_CTX:(r + 1) * L_CTX], CAP_CTX, float(r * CAP_CTX))
                for r in range(CTX_PER_SR)]
        emit(jnp.concatenate([o[0] for o in outs], axis=1), jnp.concatenate([o[1] for o in outs], axis=1))

    @pl.when(s >= N_SR_CTX)
    def _():
        emit(*_select_request(affT_ref[...], CAP_LAT, 0.0))


def _select(affT):
    return pl.pallas_call(
        _select_kernel,
        out_shape=(
            jax.ShapeDtypeStruct((N_EXPERTS, R), jnp.int32),
            jax.ShapeDtypeStruct((R, 128), jnp.int32),
            jax.ShapeDtypeStruct((R, 128), F32),
            jax.ShapeDtypeStruct((N_SR, N_EXPERTS, 128), jnp.int32),
        ),
        grid=(N_SR,),
        in_specs=[pl.BlockSpec((N_EXPERTS, SR_TOKENS), lambda s: (0, s))],
        out_specs=(
            pl.BlockSpec((N_EXPERTS, SR_TOKENS), lambda s: (0, s)),
            pl.BlockSpec((SR_TOKENS, 128), lambda s: (s, 0)),
            pl.BlockSpec((SR_TOKENS, 128), lambda s: (s, 0)),
            pl.BlockSpec((None, N_EXPERTS, 128), lambda s: (s, 0, 0)),
        ),
        compiler_params=_cparams(("arbitrary",)),
        name="select",
    )(affT)


def _lo_at(lo_ref, sr, e, j):
    return lo_ref[(sr * N_EXPERTS + e) * LO_COLS + j]


def _gather_kernel(lo_ref, hn_ref, slotT_ref, xe_ref):
    sr = pl.program_id(0)
    e0 = pl.program_id(1) * E_STEP
    starts = {}
    fits = None
    for e in range(E_STEP):
        for j in range(N_BLK):
            start = jnp.minimum((_lo_at(lo_ref, sr, e0 + e, j) // BF16_ROWS) * BF16_ROWS, SR_SLOTS - GW_ROWS)
            ok = _lo_at(lo_ref, sr, e0 + e, j + 1) - start <= GW_ROWS
            fits = ok if fits is None else jnp.logical_and(fits, ok)
            starts[e, j] = start

    @pl.when(fits)
    def _():
        xe_ref[...] = jnp.zeros(xe_ref.shape, BF16)
        w_iota = lax.broadcasted_iota(jnp.int32, (GW_ROWS, BLK), 0)
        for j in range(N_BLK):
            rows = [jnp.where(w_iota == slotT_ref[e:e + 1, j * BLK:(j + 1) * BLK] - starts[e, j], 1.0, 0.0).astype(BF16)
                    for e in range(E_STEP)]
            part = jnp.dot(jnp.concatenate(rows, axis=0), hn_ref[j * BLK:(j + 1) * BLK, :],
                           preferred_element_type=F32).astype(BF16)
            for e in range(E_STEP):
                win = pl.ds(pl.multiple_of(starts[e, j], BF16_ROWS), GW_ROWS)
                xe_ref[e, win, :] = xe_ref[e, win, :] + part[e * GW_ROWS:(e + 1) * GW_ROWS, :]

    @pl.when(jnp.logical_not(fits))
    def _():
        s_iota = lax.broadcasted_iota(jnp.int32, (SR_SLOTS, SR_TOKENS), 0)
        for e in range(E_STEP):
            onehot = jnp.where(s_iota == slotT_ref[e:e + 1, :], 1.0, 0.0).astype(BF16)
            xe_ref[e] = jnp.dot(onehot, hn_ref[...], preferred_element_type=F32).astype(BF16)


def _gather(lo_flat, hn, slotT):
    return pl.pallas_call(
        _gather_kernel,
        out_shape=jax.ShapeDtypeStruct((N_EXPERTS, N_SR * SR_SLOTS, D), BF16),
        grid_spec=pltpu.PrefetchScalarGridSpec(
            num_scalar_prefetch=1,
            grid=(N_SR, N_EXPERTS // E_STEP),
            in_specs=[
                pl.BlockSpec((SR_TOKENS, D), lambda s, h, lo: (s, 0)),
                pl.BlockSpec((E_STEP, SR_TOKENS), lambda s, h, lo: (h, s)),
            ],
            out_specs=pl.BlockSpec((E_STEP, SR_SLOTS, D), lambda s, h, lo: (h, s, 0)),
        ),
        compiler_params=_cparams(("arbitrary", "arbitrary")),
        name="gather",
    )(lo_flat, hn, slotT)


def _ffn_kernel(xe_ref, wg_ref, wu_ref, wd_ref, ye_ref):
    xe = xe_ref[...]
    hg = jnp.dot(xe, wg_ref[...].astype(BF16), preferred_element_type=F32)
    hu = jnp.dot(xe, wu_ref[...].astype(BF16), preferred_element_type=F32)
    hid = (_silu(hg) * hu).astype(BF16)
    ye_ref[...] = jnp.dot(hid, wd_ref[...].astype(BF16), preferred_element_type=F32).astype(BF16)


def _ffn(xe, wg, wu, wd, layer):
    m = xe.shape[1]
    w_spec = pl.BlockSpec((None, None, D, D), lambda e: (layer, e, 0, 0))
    return pl.pallas_call(
        _ffn_kernel,
        out_shape=jax.ShapeDtypeStruct((N_EXPERTS, m, D), BF16),
        grid=(N_EXPERTS,),
        in_specs=[pl.BlockSpec((None, m, D), lambda e: (e, 0, 0)), w_spec, w_spec, w_spec],
        out_specs=pl.BlockSpec((None, m, D), lambda e: (e, 0, 0)),
        compiler_params=_cparams(("arbitrary",)),
        name="expert_ffn",
    )(xe, wg, wu, wd)


def _combine_kernel(lo_ref, x_ref, ye_ref, slot_ref, gate_ref, mod_ref, fnw_ref, spread_ref, *rest, final):
    if final:
        oc_ref, ol_ref, yw_ref, acc_ref = rest
    else:
        o_ref, yw_ref, acc_ref = rest
    sr = pl.program_id(0)
    j = pl.program_id(1)
    starts = []
    fits = None
    for e in range(N_EXPERTS):
        start = jnp.minimum((_lo_at(lo_ref, sr, e, j) // BF16_ROWS) * BF16_ROWS, SR_SLOTS - CW_ROWS)
        ok = _lo_at(lo_ref, sr, e, j + 1) - start <= CW_ROWS
        fits = ok if fits is None else jnp.logical_and(fits, ok)
        starts.append(start)
    slot = slot_ref[...]
    gate = gate_ref[...]

    @pl.when(fits)
    def _():
        for e in range(N_EXPERTS):
            yw_ref[e * CW_ROWS:(e + 1) * CW_ROWS, :] = ye_ref[e, pl.ds(pl.multiple_of(starts[e], BF16_ROWS), CW_ROWS), :]
        n_k = N_EXPERTS * CW_ROWS
        k_exp = lax.broadcasted_iota(jnp.int32, (1, n_k), 1) // CW_ROWS
        k_row = (lax.broadcasted_iota(jnp.int32, (1, n_k), 1) % CW_ROWS).astype(F32)
        start_k = jnp.zeros((1, n_k), F32)
        for e in range(N_EXPERTS):
            start_k = jnp.where(k_exp == e, starts[e].astype(F32), start_k)
        spread = spread_ref[...]
        slot_k = jnp.dot(slot.astype(F32).astype(BF16), spread, preferred_element_type=F32)
        hit = slot_k - start_k == k_row
        gate_k = jnp.dot(gate.astype(BF16), spread, preferred_element_type=F32)
        acc_ref[...] = jnp.dot(jnp.where(hit, gate_k, 0.0).astype(BF16), yw_ref[...], preferred_element_type=F32)

    @pl.when(jnp.logical_not(fits))
    def _():
        s_iota = lax.broadcasted_iota(jnp.int32, (BLK, SR_SLOTS), 1)
        acc = jnp.zeros((BLK, D), F32)
        for e in range(N_EXPERTS):
            onehot = jnp.where(slot[:, e:e + 1] == s_iota, 1.0, 0.0).astype(BF16)
            acc = acc + gate[:, e:e + 1] * jnp.dot(onehot, ye_ref[e], preferred_element_type=F32)
        acc_ref[...] = acc

    out = x_ref[...] + mod_ref[5:6, :] * acc_ref[...]
    if not final:
        o_ref[...] = out
    else:
        out = out * lax.rsqrt(jnp.mean(out * out, axis=-1, keepdims=True) + EPS) * fnw_ref[...]
        ol_ref[...] = out

        @pl.when(sr < N_SR_CTX)
        def _():
            oc_ref[...] = out


def _combine(lo_flat, x, ye, slot_tok, gate_tok, mod_l, fnw, *, final):
    blk_of = lambda s, j: s * N_BLK + j
    n_ctx_blk = R_CTX // BLK
    spread = jnp.asarray(np.arange(128)[:, None] == (np.arange(N_EXPERTS * CW_ROWS) // CW_ROWS)[None, :], BF16)
    if final:
        out_shape = (jax.ShapeDtypeStruct((R_CTX, D), F32), jax.ShapeDtypeStruct((R_LAT, D), F32))
        out_specs = (pl.BlockSpec((BLK, D), lambda s, j, lo: (jnp.minimum(blk_of(s, j), n_ctx_blk - 1), 0)),
                     pl.BlockSpec((BLK, D), lambda s, j, lo: (jnp.maximum(blk_of(s, j) - n_ctx_blk, 0), 0)))
    else:
        out_shape = jax.ShapeDtypeStruct((R, D), F32)
        out_specs = pl.BlockSpec((BLK, D), lambda s, j, lo: (blk_of(s, j), 0))
    return pl.pallas_call(
        functools.partial(_combine_kernel, final=final),
        out_shape=out_shape,
        grid_spec=pltpu.PrefetchScalarGridSpec(
            num_scalar_prefetch=1,
            grid=(N_SR, N_BLK),
            in_specs=[
                pl.BlockSpec((BLK, D), lambda s, j, lo: (blk_of(s, j), 0)),
                pl.BlockSpec((N_EXPERTS, SR_SLOTS, D), lambda s, j, lo: (0, s, 0)),
                pl.BlockSpec((BLK, 128), lambda s, j, lo: (blk_of(s, j), 0)),
                pl.BlockSpec((BLK, 128), lambda s, j, lo: (blk_of(s, j), 0)),
                pl.BlockSpec((None, MOD_CHUNKS, D),
                             lambda s, j, lo: (_cond_of_tile(blk_of(s, j) * BLK // TM), 0, 0)),
                pl.BlockSpec((1, D), lambda s, j, lo: (0, 0)),
                pl.BlockSpec((128, N_EXPERTS * CW_ROWS), lambda s, j, lo: (0, 0)),
            ],
            out_specs=out_specs,
            scratch_shapes=[pltpu.VMEM((N_EXPERTS * CW_ROWS, D), BF16), pltpu.VMEM((BLK, D), F32)],
        ),
        compiler_params=_cparams(("arbitrary", "arbitrary")),
        name="combine",
    )(lo_flat, x, ye, slot_tok, gate_tok, mod_l, fnw, spread)


def _moe(x, mod_l, nw, wrT, wg, wu, wd, fnw, layer, *, final):
    hn, affT = _router(x, mod_l, nw, wrT)
    slotT, slot_tok, gate_tok, lo = _select(affT)
    lo_flat = lo[:, :, :LO_COLS].reshape(-1)
    xe = _gather(lo_flat, hn, slotT)
    ye = _ffn(xe, wg, wu, wd, layer)
    return _combine(lo_flat, x, ye, slot_tok, gate_tok, mod_l, fnw, final=final)


def _group_major(p):
    return p.reshape(2, GROUPS, HPG).transpose(1, 0, 2).reshape(GROUPS, 2 * HPG)


def kernel(x_prompt, x_sample, state_ssm, c, c_ctx, norm1_w, norm2_w, w_mod, b_mod, conv_in_w, conv_w, conv_out_w, ssd_in_w, ssd_conv_w, ssd_conv_b, ssd_dt_bias, ssd_a_log, ssd_d, ssd_norm_w, ssd_out_w, router_w, exp_w_gate, exp_w_up, exp_w_down, final_norm_w):
    x = (x_prompt.reshape(R_CTX, D), x_sample.reshape(R_LAT, D))
    cond = jnp.concatenate([c_ctx[None, :], c, jnp.zeros((8 - N_COND, D), F32)], axis=0)
    mods = _modulation(cond.T, w_mod, b_mod)
    mods = mods[:, :N_COND].reshape(DEPTH, N_COND, MOD_CHUNKS, D)
    fnw = final_norm_w.reshape(1, D)
    conv_in_b, conv_out_b = conv_in_w.astype(BF16), conv_out_w.astype(BF16)
    ssd_in_b, ssd_out_b = ssd_in_w.astype(BF16), ssd_out_w.astype(BF16)
    ssd_conv_b3 = ssd_conv_b.reshape(DEPTH // 2, 1, CONV_DIM)
    ssd_norm_w3 = ssd_norm_w.reshape(DEPTH // 2, 1, D_INNER)

    states = None
    for layer in range(DEPTH):
        mod_l = mods[layer]
        j = layer // 2
        nw1 = norm1_w[layer].reshape(1, D)
        if layer % 2 == 0:
            x = _conv_mixer(x, mod_l, nw1, conv_in_b, conv_w, conv_out_b, j)
        else:
            w_dt = ssd_in_b[j, :, D_INNER + CONV_DIM:].reshape(D, 2, GROUPS, HPG).transpose(0, 2, 1, 3)
            w_dt_rep = jnp.broadcast_to(w_dt.reshape(D, GROUPS, 1, 2 * HPG), (D, GROUPS, N_SPLIT, 2 * HPG))
            dtb = _group_major(ssd_dt_bias[j])
            dtb_rep = jnp.tile(dtb, (1, N_SPLIT)).reshape(1, GROUPS * DT_LANES)
            z, xs, bm, cm, dt4, dtT4 = _ssd_in(
                x, mod_l, nw1, ssd_in_b, w_dt_rep.reshape(D, GROUPS * DT_LANES), w_dt.reshape(D, 2 * HEADS).T,
                ssd_conv_w, ssd_conv_b3, dtb_rep, dtb.reshape(2 * HEADS, 1), j)
            alog4 = _group_major(ssd_a_log[j])
            dsum = (ssd_d[j][0] + ssd_d[j][1]).reshape(GROUPS, HPG)
            dsk4 = jnp.tile(jnp.concatenate([jnp.zeros_like(dsum), dsum], axis=1), (1, N_SPLIT))
            scan_args = (xs.reshape(R // CHUNK, CHUNK, D_INNER), bm.reshape(R // CHUNK, CHUNK, GROUPS * STATE),
                         cm.reshape(R // CHUNK, CHUNK, GROUPS * STATE), dt4, dtT4,
                         jnp.tile(alog4, (1, N_SPLIT)).reshape(GROUPS, 1, DT_LANES),
                         alog4.reshape(GROUPS, 2 * HPG, 1), dsk4.reshape(GROUPS, 1, DT_LANES))
            y_ctx, states = _ssd_scan(*scan_args, None, states, n_seq=N_CTX, seq_len=L_CTX, row_block0=0,
                                      state_layer=j)
            h0 = state_ssm[:, j].reshape(N_LAT, 2, HEADS * HEADDIM, STATE)
            y_lat = _ssd_scan(*scan_args, h0, None, n_seq=N_LAT, seq_len=L_LAT, row_block0=R_CTX // L_LAT,
                              state_layer=None)
            x = _ssd_out(x, y_ctx.reshape(R_CTX, D_INNER), y_lat.reshape(R_LAT, D_INNER), z, mod_l,
                         ssd_norm_w3, ssd_out_b, j)
        x = _moe(x, mod_l, norm2_w[layer].reshape(1, D), router_w[layer].T, exp_w_gate, exp_w_up, exp_w_down, fnw,
                 layer, final=(layer == DEPTH - 1))

    y_prompt, y_sample = x
    return (y_prompt.reshape(N_CTX, L_CTX, D), y_sample.reshape(N_LAT, L_LAT, D),
            states.reshape(N_CTX, DEPTH // 2, 2, HEADS, HEADDIM, STATE))
```

```python
import functools

import jax
import jax.numpy as jnp
import numpy as np
from jax import lax
from jax.experimental import pallas as pl
from jax.experimental.pallas import tpu as pltpu

F32 = jnp.float32
BF16 = jnp.bfloat16
LOG2E = 1.4426950408889634

D = 1024
DEPTH = 4
N_CTX, L_CTX = 16, 256
N_LAT, L_LAT = 2, 2048
GRID_W = 64
R_CTX = N_CTX * L_CTX
R_LAT = N_LAT * L_LAT
R = R_CTX + R_LAT
N_COND = 1 + N_LAT
MOD_CHUNKS = 6
D_INNER = 2048
HEADDIM = 64
HEADS = 32
GROUPS = 4
HPG = HEADS // GROUPS
GW = HPG * HEADDIM
STATE = 128
CHUNK = 128
CONV_DIM = D_INNER + 2 * GROUPS * STATE
N_SPLIT = 2
DT_LANES = N_SPLIT * 2 * HPG
N_EXPERTS = 16
EPS = 1e-6

TM = 512
TM_OUT = 512
CTX_TILES = R_CTX // TM
LAT_TILES_PER_REQ = L_LAT // TM
SR_TOKENS = 2048
N_SR = R // SR_TOKENS
N_SR_CTX = R_CTX // SR_TOKENS
SR_SLOTS = 2 * SR_TOKENS // N_EXPERTS
CTX_PER_SR = SR_TOKENS // L_CTX
CAP_CTX = 2 * L_CTX // N_EXPERTS
CAP_LAT = 2 * L_LAT // N_EXPERTS
BLK = 256
N_BLK = SR_TOKENS // BLK
LO_COLS = 16
BF16_ROWS = 16
GW_ROWS = 64
CW_ROWS = 64
E_HALF = N_EXPERTS // 2
VMEM_LIMIT = 56 * 1024 * 1024


def _cparams(sem):
    return pltpu.CompilerParams(dimension_semantics=sem, vmem_limit_bytes=VMEM_LIMIT)


def _cond_of_tile(i):
    return jnp.where(i < CTX_TILES, 0, 1 + (i - CTX_TILES) // LAT_TILES_PER_REQ)


def _silu(v):
    h = 0.5 * v
    return h + h * jnp.tanh(h)


def _softplus(v):
    return jnp.maximum(v, 0.0) + jnp.log1p(jnp.exp(-jnp.abs(v)))


def _norm_mod(x, nw, scale, shift):
    y = x * lax.rsqrt(jnp.mean(x * x, axis=-1, keepdims=True) + EPS)
    return y * nw * (1.0 + scale) + shift


def _conv3_rows(p, w_ref, tile_idx):
    n = p.shape[0]
    period = jnp.where(tile_idx < CTX_TILES, L_CTX, GRID_W)
    r = lax.broadcasted_iota(jnp.int32, (n, 1), 0) & (period - 1)
    prev = jnp.where(r == 0, 0.0, pltpu.roll(p, 1, axis=0))
    nxt = jnp.where(r == period - 1, 0.0, pltpu.roll(p, n - 1, axis=0))
    return prev * w_ref[0:1, :] + p * w_ref[1:2, :] + nxt * w_ref[2:3, :]


def _split_pieces(v):
    pieces = []
    r = v
    for _ in range(N_SPLIT):
        p = r.astype(BF16)
        pieces.append(p)
        r = r - p.astype(F32)
    return pieces


def _modulation_kernel(condT_ref, w_ref, b_ref, o_ref):
    s = _silu(condT_ref[...])
    w = w_ref[...]
    rows = [jnp.sum(w * s[:, r:r + 1], axis=0, keepdims=True) + b_ref[...] for r in range(N_COND)]
    rows.append(jnp.zeros((8 - N_COND, w.shape[1]), F32))
    o_ref[...] = jnp.concatenate(rows, axis=0)


def _modulation(condT, w_mod, b_mod):
    tn = 1536
    n = MOD_CHUNKS * D
    return pl.pallas_call(
        _modulation_kernel,
        out_shape=jax.ShapeDtypeStruct((DEPTH, 8, n), F32),
        grid=(DEPTH, n // tn),
        in_specs=[
            pl.BlockSpec((D, 8), lambda l, j: (0, 0)),
            pl.BlockSpec((None, D, tn), lambda l, j: (l, 0, j)),
            pl.BlockSpec((None, 1, tn), lambda l, j: (l, 0, j)),
        ],
        out_specs=pl.BlockSpec((None, 8, tn), lambda l, j: (l, 0, j)),
        compiler_params=_cparams(("arbitrary", "arbitrary")),
        name="modulation",
    )(condT, w_mod, b_mod.reshape(DEPTH, 1, n))


def _conv_mixer_kernel(*refs, split_input):
    i = pl.program_id(0)
    if split_input:
        xc_ref, xl_ref, mod_ref, nw_ref, win_ref, cw_ref, wout_ref, o_ref = refs
        x = jnp.where(i < CTX_TILES, xc_ref[...], xl_ref[...])
    else:
        x_ref, mod_ref, nw_ref, win_ref, cw_ref, wout_ref, o_ref = refs
        x = x_ref[...]
    hn = _norm_mod(x, nw_ref[...], mod_ref[1:2, :], mod_ref[0:1, :]).astype(BF16)
    gb = jnp.dot(hn, win_ref[:, 0:D], preferred_element_type=F32)
    gc = jnp.dot(hn, win_ref[:, D:2 * D], preferred_element_type=F32)
    v = jnp.dot(hn, win_ref[:, 2 * D:3 * D], preferred_element_type=F32)
    q = (gb * _conv3_rows(gc * v, cw_ref, i)).astype(BF16)
    mix = jnp.dot(q, wout_ref[...], preferred_element_type=F32)
    o_ref[...] = x + mod_ref[2:3, :] * mix


def _conv_mixer(xs, mod_l, nw, w_in, cw, w_out, j):
    split_input = isinstance(xs, tuple)
    if split_input:
        x_specs = [pl.BlockSpec((TM, D), lambda i: (jnp.minimum(i, CTX_TILES - 1), 0)),
                   pl.BlockSpec((TM, D), lambda i: (jnp.maximum(i - CTX_TILES, 0), 0))]
    else:
        xs = (xs,)
        x_specs = [pl.BlockSpec((TM, D), lambda i: (i, 0))]
    return pl.pallas_call(
        functools.partial(_conv_mixer_kernel, split_input=split_input),
        out_shape=jax.ShapeDtypeStruct((R, D), F32),
        grid=(R // TM,),
        in_specs=x_specs + [
            pl.BlockSpec((None, MOD_CHUNKS, D), lambda i: (_cond_of_tile(i), 0, 0)),
            pl.BlockSpec((1, D), lambda i: (0, 0)),
            pl.BlockSpec((None, D, 3 * D), lambda i: (j, 0, 0)),
            pl.BlockSpec((None, 3, D), lambda i: (j, 0, 0)),
            pl.BlockSpec((None, D, D), lambda i: (j, 0, 0)),
        ],
        out_specs=pl.BlockSpec((TM, D), lambda i: (i, 0)),
        compiler_params=_cparams(("arbitrary",)),
        name="conv_mixer",
    )(*xs, mod_l, nw, w_in, cw, w_out)


def _ssd_in_kernel(x_ref, mod_ref, nw_ref, win_ref, wdt_ref, wdtT_ref, cw_ref, cb_ref, dtb_ref, dtbT_ref,
                   z_ref, xs_ref, bm_ref, cm_ref, dt_ref, dtT_ref):
    i = pl.program_id(0)
    hn = _norm_mod(x_ref[...], nw_ref[...], mod_ref[1:2, :], mod_ref[0:1, :]).astype(BF16)
    for k in range(D_INNER // D):
        z_ref[:, k * D:(k + 1) * D] = jnp.dot(hn, win_ref[:, k * D:(k + 1) * D], preferred_element_type=F32)
    for k in range(CONV_DIM // D):
        lo = D_INNER + k * D
        u = jnp.dot(hn, win_ref[:, lo:lo + D], preferred_element_type=F32)
        u = _silu(_conv3_rows(u, cw_ref.at[:, k * D:(k + 1) * D], i) + cb_ref[:, k * D:(k + 1) * D])
        if k < D_INNER // D:
            xs_ref[:, k * D:(k + 1) * D] = u
        else:
            bm_ref[...] = u[:, :GROUPS * STATE]
            cm_ref[...] = u[:, GROUPS * STATE:]
    dt = _softplus(jnp.dot(hn, wdt_ref[...], preferred_element_type=F32) + dtb_ref[...])
    dtT = _softplus(
        lax.dot_general(wdtT_ref[...], hn, (((1,), (1,)), ((), ())), preferred_element_type=F32) + dtbT_ref[...])
    for g in range(GROUPS):
        dt_ref[g] = dt[:, g * DT_LANES:(g + 1) * DT_LANES].reshape(TM // CHUNK, CHUNK, DT_LANES)
        for k in range(TM // CHUNK):
            dtT_ref[g, k] = dtT[g * 2 * HPG:(g + 1) * 2 * HPG, k * CHUNK:(k + 1) * CHUNK]


def _ssd_in(x, mod_l, nw, w_in, w_dt, w_dtT, cw, cb, dtb, dtbT, j):
    n_in = w_in.shape[2]
    nck = TM // CHUNK
    return pl.pallas_call(
        _ssd_in_kernel,
        out_shape=(
            jax.ShapeDtypeStruct((R, D_INNER), F32),
            jax.ShapeDtypeStruct((R, D_INNER), F32),
            jax.ShapeDtypeStruct((R, GROUPS * STATE), F32),
            jax.ShapeDtypeStruct((R, GROUPS * STATE), F32),
            jax.ShapeDtypeStruct((GROUPS, R // CHUNK, CHUNK, DT_LANES), F32),
            jax.ShapeDtypeStruct((GROUPS, R // CHUNK, 2 * HPG, CHUNK), F32),
        ),
        grid=(R // TM,),
        in_specs=[
            pl.BlockSpec((TM, D), lambda i: (i, 0)),
            pl.BlockSpec((None, MOD_CHUNKS, D), lambda i: (_cond_of_tile(i), 0, 0)),
            pl.BlockSpec((1, D), lambda i: (0, 0)),
            pl.BlockSpec((None, D, n_in), lambda i: (j, 0, 0)),
            pl.BlockSpec((D, GROUPS * DT_LANES), lambda i: (0, 0)),
            pl.BlockSpec((2 * HEADS, D), lambda i: (0, 0)),
            pl.BlockSpec((None, 3, CONV_DIM), lambda i: (j, 0, 0)),
            pl.BlockSpec((None, 1, CONV_DIM), lambda i: (j, 0, 0)),
            pl.BlockSpec((1, GROUPS * DT_LANES), lambda i: (0, 0)),
            pl.BlockSpec((2 * HEADS, 1), lambda i: (0, 0)),
        ],
        out_specs=(
            pl.BlockSpec((TM, D_INNER), lambda i: (i, 0)),
            pl.BlockSpec((TM, D_INNER), lambda i: (i, 0)),
            pl.BlockSpec((TM, GROUPS * STATE), lambda i: (i, 0)),
            pl.BlockSpec((TM, GROUPS * STATE), lambda i: (i, 0)),
            pl.BlockSpec((GROUPS, nck, CHUNK, DT_LANES), lambda i: (0, i, 0, 0)),
            pl.BlockSpec((GROUPS, nck, 2 * HPG, CHUNK), lambda i: (0, i, 0, 0)),
        ),
        compiler_params=_cparams(("arbitrary",)),
        name="ssd_in",
    )(x, mod_l, nw, w_in, w_dt, w_dtT, cw, cb, dtb, dtbT)


def _ssd_scan_kernel(*refs, n_chunks, has_h0, n_kept, state_slots):
    emit_state = state_slots is not None
    xs_ref, bm_ref, cm_ref, dt_ref, dtT_ref, alog_ref, alogT_ref, dsk_ref = refs[:8]
    tri_k_ref, tri_r_ref, expf_ref, expb_ref, colsel_ref = refs[8:13]
    k = 13
    h0_ref = None
    if has_h0:
        h0_ref = refs[k]
        k += 1
    k += n_kept
    y_ref = refs[k]
    k += 1
    st_out_ref = None
    if emit_state:
        st_out_ref = refs[k]
        k += 1
    st_ref = refs[k]

    a_row = -jnp.exp(alog_ref[...]) * LOG2E
    a_col = -jnp.exp(alogT_ref[...]) * LOG2E
    row_is_fwd = lax.broadcasted_iota(jnp.int32, (2 * HPG, 1), 0) < HPG
    qi = lax.broadcasted_iota(jnp.int32, (CHUNK, CHUNK), 0)
    si = lax.broadcasted_iota(jnp.int32, (CHUNK, CHUNK), 1)
    lower = si <= qi
    lane = lax.broadcasted_iota(jnp.int32, (1, DT_LANES), 1)
    piece_of_lane = lane // (2 * HPG)
    lane_is_fwd = lane % (2 * HPG) < HPG
    lane_c = lax.broadcasted_iota(jnp.int32, (1, CHUNK), 1)
    first_half = lane_c < HEADDIM

    def lane_pieces(v):
        pieces = _split_pieces(v)
        out = pieces[N_SPLIT - 1]
        for r in range(N_SPLIT - 2, -1, -1):
            out = jnp.where(piece_of_lane == r, pieces[r], out)
        return out

    def expand_many(vs, e):
        out = jnp.dot(jnp.concatenate([lane_pieces(v) for v in vs], axis=0), e, preferred_element_type=F32)
        res, r0 = [], 0
        for v in vs:
            res.append(out[r0:r0 + v.shape[0], :])
            r0 += v.shape[0]
        return res

    for d in range(2):
        if has_h0:
            st_ref[d] = h0_ref[d].T
        else:
            st_ref[d] = jnp.zeros((STATE, GW), F32)

    y_ref[...] = jnp.zeros(y_ref.shape, F32)

    def body(k_, carry):
        cf = k_
        x = xs_ref[cf]
        b = bm_ref[cf]
        cm = cm_ref[cf]
        dt = dt_ref[cf]
        dtT = dtT_ref[cf]
        cr = n_chunks - 1 - k_
        xr = xs_ref[cr]
        br = bm_ref[cr]
        dtr = dt_ref[cr]

        dta_rows = jnp.concatenate(_split_pieces(dt * a_row), axis=0)
        dtaT_lanes = jnp.concatenate(_split_pieces(dtT * a_col), axis=1)
        acs2 = jnp.dot(tri_k_ref[...], dta_rows, preferred_element_type=F32)
        acs_lo, acs_up = acs2[:CHUNK, :], acs2[CHUNK:, :]
        acs_t2 = jnp.dot(dtaT_lanes, tri_r_ref[...], preferred_element_type=F32)
        acs_up_r = jnp.dot(tri_k_ref[CHUNK:, :], jnp.concatenate(_split_pieces(dtr * a_row), axis=0),
                           preferred_element_type=F32)
        r_t = jnp.log(dtT) * LOG2E - jnp.where(row_is_fwd, acs_t2[:, :CHUNK], acs_t2[:, CHUNK:])
        col_b = jnp.dot(lane_pieces(jnp.where(lane_is_fwd, acs_lo, acs_up)), colsel_ref[...],
                        preferred_element_type=F32)

        a_last = acs_lo[CHUNK - 1:CHUNK, :]
        a_tot = acs_up_r[0:1, :]
        cb_diag = jnp.sum(cm * b, axis=1, keepdims=True)
        e_out_f, e_in_f = expand_many([jnp.exp2(acs_lo), jnp.exp2(a_last - acs_lo) * dt], expf_ref[...])
        e_self, e_out_b, e_in_b = expand_many(
            [dsk_ref[...] + cb_diag * dt, jnp.exp2(acs_up_r), jnp.exp2(a_tot - acs_up_r) * dtr], expb_ref[...])
        e_keep_f = e_out_f[CHUNK - 1:CHUNK, :]
        e_keep_b = e_out_b[0:1, :]

        cmb = cm.astype(BF16)
        cb = lax.dot_general(cmb, b.astype(BF16), (((1,), (1,)), ((), ())), preferred_element_type=F32)
        parts = []
        for hp in range(HPG // 2):
            ws = []
            for h in (2 * hp, 2 * hp + 1):
                arg = jnp.where(lower, col_b[:, h * CHUNK:(h + 1) * CHUNK] + r_t[h:h + 1, :],
                                col_b[:, (HPG + h) * CHUNK:(HPG + h + 1) * CHUNK] + r_t[HPG + h:HPG + h + 1, :])
                ws.append((cb * jnp.exp2(arg)).astype(BF16))
            xp = x[:, hp * CHUNK:(hp + 1) * CHUNK]
            x2 = jnp.concatenate([jnp.where(first_half, xp, 0.0), jnp.where(first_half, 0.0, xp)], axis=0)
            parts.append(jnp.dot(jnp.concatenate(ws, axis=1), x2.astype(BF16), preferred_element_type=F32))
        st_f = st_ref[0]
        y = (jnp.concatenate(parts, axis=1) + e_self * x
             + jnp.dot(cmb, st_f.astype(BF16), preferred_element_type=F32) * e_out_f)
        y_ref[cf] = y_ref[cf] + y
        st_ref[0] = st_f * e_keep_f + jnp.dot(
            b.T.astype(BF16), (x * e_in_f).astype(BF16), preferred_element_type=F32)

        st_b = st_ref[1]
        y_ref[cr] = y_ref[cr] + jnp.dot(cm_ref[cr].astype(BF16), st_b.astype(BF16),
                                        preferred_element_type=F32) * e_out_b
        st_ref[1] = st_b * e_keep_b + jnp.dot(
            br.T.astype(BF16), (xr * e_in_b).astype(BF16), preferred_element_type=F32)
        return carry

    lax.fori_loop(0, n_chunks, body, 0)
    if emit_state:
        for slot, own in enumerate(state_slots):
            for d in range(2):
                st_out_ref[slot, d] = st_ref[d].T if own else jnp.zeros((GW, STATE), F32)


def _scan_constants():
    q = np.arange(CHUNK)
    tri_lo = (q[None, :] <= q[:, None]).astype(np.float32)
    tri_up = tri_lo.T
    tri_k = np.concatenate([np.tile(tri_lo, (1, N_SPLIT)), np.tile(tri_up, (1, N_SPLIT))], axis=0)
    tri_r = np.concatenate([np.tile(tri_up, (N_SPLIT, 1)), np.tile(tri_lo, (N_SPLIT, 1))], axis=1)
    head = np.arange(DT_LANES) % (2 * HPG)
    chan_head = np.arange(GW) // HEADDIM
    exp_f = (head[:, None] == chan_head[None, :]).astype(np.float32)
    exp_b = (head[:, None] == chan_head[None, :] + HPG).astype(np.float32)
    col_sel = (head[:, None] == (np.arange(2 * HPG * CHUNK) // CHUNK)[None, :]).astype(np.float32)
    return [jnp.asarray(m, BF16) for m in (tri_k, tri_r, exp_f, exp_b, col_sel)]


def _ssd_scan(xs3, bm3, cm3, dt4, dtT4, alog4, alogT4, dskx, h0, st_prev, *, n_seq, seq_len, row_block0, state_layer):
    nck = seq_len // CHUNK
    has_h0 = h0 is not None
    emit_state = state_layer is not None
    keep_state = st_prev is not None
    rb = lambda s: s + row_block0
    in_specs = [
        pl.BlockSpec((nck, CHUNK, GW), lambda s, g: (rb(s), 0, g)),
        pl.BlockSpec((nck, CHUNK, STATE), lambda s, g: (rb(s), 0, g)),
        pl.BlockSpec((nck, CHUNK, STATE), lambda s, g: (rb(s), 0, g)),
        pl.BlockSpec((None, nck, CHUNK, DT_LANES), lambda s, g: (g, rb(s), 0, 0)),
        pl.BlockSpec((None, nck, 2 * HPG, CHUNK), lambda s, g: (g, rb(s), 0, 0)),
        pl.BlockSpec((None, 1, DT_LANES), lambda s, g: (g, 0, 0)),
        pl.BlockSpec((None, 2 * HPG, 1), lambda s, g: (g, 0, 0)),
        pl.BlockSpec((None, 1, DT_LANES), lambda s, g: (g, 0, 0)),
    ]
    args = [xs3, bm3, cm3, dt4, dtT4, alog4, alogT4, dskx]
    for const in _scan_constants():
        in_specs.append(pl.BlockSpec(const.shape, lambda s, g: (0, 0)))
        args.append(const)
    if has_h0:
        in_specs.append(pl.BlockSpec((None, 2, GW, STATE), lambda s, g: (s, 0, g, 0)))
        args.append(h0)
    aliases = {}
    if keep_state:
        in_specs.append(pl.BlockSpec(memory_space=pl.ANY))
        aliases[len(args)] = 1
        args.append(st_prev)
    y_shape = jax.ShapeDtypeStruct((n_seq * nck, CHUNK, D_INNER), F32)
    y_spec = pl.BlockSpec((nck, CHUNK, GW), lambda s, g: (s, 0, g))
    n_layers = DEPTH // 2
    state_slots = None
    if emit_state:
        out_shape = (y_shape, jax.ShapeDtypeStruct((n_seq, n_layers, 2, HEADS * HEADDIM, STATE), F32))
        if keep_state:
            state_slots = (True,)
            st_spec = pl.BlockSpec((None, 1, 2, GW, STATE), lambda s, g: (s, state_layer, 0, g, 0))
        else:
            state_slots = tuple(l == state_layer for l in range(n_layers))
            st_spec = pl.BlockSpec((None, n_layers, 2, GW, STATE), lambda s, g: (s, 0, 0, g, 0))
        out_specs = (y_spec, st_spec)
    else:
        out_shape, out_specs = y_shape, y_spec
    return pl.pallas_call(
        functools.partial(_ssd_scan_kernel, n_chunks=nck, has_h0=has_h0, n_kept=len(aliases),
                          state_slots=state_slots),
        out_shape=out_shape,
        grid=(n_seq, GROUPS),
        in_specs=in_specs,
        out_specs=out_specs,
        scratch_shapes=[pltpu.VMEM((2, STATE, GW), F32)],
        input_output_aliases=aliases,
        compiler_params=_cparams(("arbitrary", "arbitrary")),
        name="ssd_scan",
    )(*args)


def _ssd_out_kernel(x_ref, yc_ref, yl_ref, z_ref, mod_ref, nw_ref, wout_ref, o_ref):
    is_ctx = pl.program_id(0) < R_CTX // TM_OUT
    rows_per_trip = TM_OUT // 2

    def half(h, carry):
        rows = pl.ds(pl.multiple_of(h * rows_per_trip, rows_per_trip), rows_per_trip)
        y = jnp.where(is_ctx, yc_ref[rows, :], yl_ref[rows, :])
        v = y * _silu(z_ref[rows, :])
        v = v * lax.rsqrt(jnp.mean(v * v, axis=-1, keepdims=True) + EPS) * nw_ref[...]
        mix = jnp.dot(v.astype(BF16), wout_ref[...], preferred_element_type=F32)
        o_ref[rows, :] = x_ref[rows, :] + mod_ref[2:3, :] * mix
        return carry

    lax.fori_loop(0, 2, half, 0)


def _ssd_out(x, y_ctx, y_lat, z, mod_l, nw, w_out, j):
    n_ctx = R_CTX // TM_OUT
    return pl.pallas_call(
        _ssd_out_kernel,
        out_shape=jax.ShapeDtypeStruct((R, D), F32),
        grid=(R // TM_OUT,),
        in_specs=[
            pl.BlockSpec((TM_OUT, D), lambda i: (i, 0)),
            pl.BlockSpec((TM_OUT, D_INNER), lambda i: (jnp.minimum(i, n_ctx - 1), 0)),
            pl.BlockSpec((TM_OUT, D_INNER), lambda i: (jnp.maximum(i - n_ctx, 0), 0)),
            pl.BlockSpec((TM_OUT, D_INNER), lambda i: (i, 0)),
            pl.BlockSpec((None, MOD_CHUNKS, D), lambda i: (_cond_of_tile(i * TM_OUT // TM), 0, 0)),
            pl.BlockSpec((None, 1, D_INNER), lambda i: (j, 0, 0)),
            pl.BlockSpec((None, D_INNER, D), lambda i: (j, 0, 0)),
        ],
        out_specs=pl.BlockSpec((TM_OUT, D), lambda i: (i, 0)),
        compiler_params=_cparams(("arbitrary",)),
        name="ssd_out",
    )(x, y_ctx, y_lat, z, mod_l, nw, w_out)


def _router_kernel(x_ref, mod_ref, nw_ref, wrT_ref, hn_ref, affT_ref):
    hn = _norm_mod(x_ref[...], nw_ref[...], mod_ref[4:5, :], mod_ref[3:4, :])
    hn_hi = hn.astype(BF16)
    hn_ref[...] = hn_hi
    hn_lo = (hn - hn_hi.astype(F32)).astype(BF16)
    w = wrT_ref[...]
    w_hi = w.astype(BF16)
    w_lo = (w - w_hi.astype(F32)).astype(BF16)
    logits = lax.dot_general(jnp.concatenate([w_hi, w_lo, w_hi], axis=1), jnp.concatenate([hn_hi, hn_hi, hn_lo], axis=1),
                             (((1,), (1,)), ((), ())), preferred_element_type=F32)
    e = jnp.exp(logits - jnp.max(logits, axis=0, keepdims=True))
    affT_ref[...] = e / jnp.sum(e, axis=0, keepdims=True)


def _router(x, mod_l, nw, wrT):
    return pl.pallas_call(
        _router_kernel,
        out_shape=(jax.ShapeDtypeStruct((R, D), BF16), jax.ShapeDtypeStruct((N_EXPERTS, R), F32)),
        grid=(R // TM,),
        in_specs=[
            pl.BlockSpec((TM, D), lambda i: (i, 0)),
            pl.BlockSpec((None, MOD_CHUNKS, D), lambda i: (_cond_of_tile(i), 0, 0)),
            pl.BlockSpec((1, D), lambda i: (0, 0)),
            pl.BlockSpec((N_EXPERTS, D), lambda i: (0, 0)),
        ],
        out_specs=(pl.BlockSpec((TM, D), lambda i: (i, 0)), pl.BlockSpec((N_EXPERTS, TM), lambda i: (0, i))),
        compiler_params=_cparams(("arbitrary",)),
        name="router",
    )(x, mod_l, nw, wrT)


def _excl_cumsum_lanes(m):
    blk = 256
    t = m.shape[1]
    a = lax.broadcasted_iota(jnp.int32, (blk, blk), 0)
    b = lax.broadcasted_iota(jnp.int32, (blk, blk), 1)
    strict = jnp.where(a < b, 1.0, 0.0).astype(BF16)
    carry = jnp.zeros((m.shape[0], 1), F32)
    outs = []
    for k in range(t // blk):
        mk = m[:, k * blk:(k + 1) * blk]
        outs.append(jnp.dot(mk.astype(BF16), strict, preferred_element_type=F32) + carry)
        carry = carry + jnp.sum(mk, axis=1, keepdims=True)
    return outs[0] if len(outs) == 1 else jnp.concatenate(outs, axis=1)


def _select_request(aff, cap, base):
    thr_bits = jnp.zeros((N_EXPERTS, 1), jnp.int32)
    for k in range(30, -1, -1):
        trial = thr_bits | (1 << k)
        cnt = jnp.sum(jnp.where(aff >= pltpu.bitcast(trial, F32), 1.0, 0.0), axis=1, keepdims=True)
        thr_bits = jnp.where(cnt >= cap, trial, thr_bits)
    thr = pltpu.bitcast(thr_bits, F32)
    gt = aff > thr
    eq = jnp.where(aff == thr, 1.0, 0.0)
    need = cap - jnp.sum(jnp.where(gt, 1.0, 0.0), axis=1, keepdims=True)
    sel = gt | ((eq > 0.0) & (_excl_cumsum_lanes(eq) < need))
    pos = _excl_cumsum_lanes(jnp.where(sel, 1.0, 0.0))
    return jnp.where(sel, pos + base, -1.0), jnp.where(sel, aff, 0.0)


def _select_kernel(affT_ref, slotT_ref, slot_tok_ref, gate_tok_ref, lo_ref):
    s = pl.program_id(0)

    def emit(slot, gate):
        slotT_ref[...] = slot.astype(jnp.int32)
        pad = jnp.zeros((128 - N_EXPERTS, SR_TOKENS), F32)
        slot_tok_ref[...] = jnp.concatenate([slot, pad - 1.0], axis=0).T.astype(jnp.int32)
        gate_tok_ref[...] = jnp.concatenate([gate, pad], axis=0).T
        t = lax.broadcasted_iota(jnp.int32, (SR_TOKENS, 128), 0)
        jcol = lax.broadcasted_iota(jnp.int32, (SR_TOKENS, 128), 1)
        before = jnp.where(t < jcol * BLK, 1.0, 0.0).astype(BF16)
        chosen = jnp.where(slot >= 0.0, 1.0, 0.0).astype(BF16)
        lo_ref[...] = jnp.dot(chosen, before, preferred_element_type=F32).astype(jnp.int32)

    @pl.when(s < N_SR_CTX)
    def _():
        outs = [_select_request(affT_ref[:, r * L_CTX:(r + 1) * L_CTX], CAP_CTX, float(r * CAP_CTX))
                for r in range(CTX_PER_SR)]
        emit(jnp.concatenate([o[0] for o in outs], axis=1), jnp.concatenate([o[1] for o in outs], axis=1))

    @pl.when(s >= N_SR_CTX)
    def _():
        emit(*_select_request(affT_ref[...], CAP_LAT, 0.0))


def _select(affT):
    return pl.pallas_call(
        _select_kernel,
        out_shape=(
            jax.ShapeDtypeStruct((N_EXPERTS, R), jnp.int32),
            jax.ShapeDtypeStruct((R, 128), jnp.int32),
            jax.ShapeDtypeStruct((R, 128), F32),
            jax.ShapeDtypeStruct((N_SR, N_EXPERTS, 128), jnp.int32),
        ),
        grid=(N_SR,),
        in_specs=[pl.BlockSpec((N_EXPERTS, SR_TOKENS), lambda s: (0, s))],
        out_specs=(
            pl.BlockSpec((N_EXPERTS, SR_TOKENS), lambda s: (0, s)),
            pl.BlockSpec((SR_TOKENS, 128), lambda s: (s, 0)),
            pl.BlockSpec((SR_TOKENS, 128), lambda s: (s, 0)),
            pl.BlockSpec((None, N_EXPERTS, 128), lambda s: (s, 0, 0)),
        ),
        compiler_params=_cparams(("arbitrary",)),
        name="select",
    )(affT)


def _lo_at(lo_ref, sr, e, j):
    return lo_ref[(sr * N_EXPERTS + e) * LO_COLS + j]


def _gather_kernel(lo_ref, hn_ref, slotT_ref, xe_ref):
    sr = pl.program_id(0)
    e0 = pl.program_id(1) * E_HALF
    starts = {}
    fits = None
    for e in range(E_HALF):
        for j in range(N_BLK):
            start = jnp.minimum((_lo_at(lo_ref, sr, e0 + e, j) // BF16_ROWS) * BF16_ROWS, SR_SLOTS - GW_ROWS)
            ok = _lo_at(lo_ref, sr, e0 + e, j + 1) - start <= GW_ROWS
            fits = ok if fits is None else jnp.logical_and(fits, ok)
            starts[e, j] = start

    @pl.when(fits)
    def _():
        xe_ref[...] = jnp.zeros(xe_ref.shape, BF16)
        w_iota = lax.broadcasted_iota(jnp.int32, (GW_ROWS, BLK), 0)
        for j in range(N_BLK):
            rows = [jnp.where(w_iota == slotT_ref[e:e + 1, j * BLK:(j + 1) * BLK] - starts[e, j], 1.0, 0.0).astype(BF16)
                    for e in range(E_HALF)]
            part = jnp.dot(jnp.concatenate(rows, axis=0), hn_ref[j * BLK:(j + 1) * BLK, :],
                           preferred_element_type=F32).astype(BF16)
            for e in range(E_HALF):
                win = pl.ds(pl.multiple_of(starts[e, j], BF16_ROWS), GW_ROWS)
                xe_ref[e, win, :] = xe_ref[e, win, :] + part[e * GW_ROWS:(e + 1) * GW_ROWS, :]

    @pl.when(jnp.logical_not(fits))
    def _():
        s_iota = lax.broadcasted_iota(jnp.int32, (SR_SLOTS, SR_TOKENS), 0)
        for e in range(E_HALF):
            onehot = jnp.where(s_iota == slotT_ref[e:e + 1, :], 1.0, 0.0).astype(BF16)
            xe_ref[e] = jnp.dot(onehot, hn_ref[...], preferred_element_type=F32).astype(BF16)


def _gather(lo_flat, hn, slotT):
    return pl.pallas_call(
        _gather_kernel,
        out_shape=jax.ShapeDtypeStruct((N_EXPERTS, N_SR * SR_SLOTS, D), BF16),
        grid_spec=pltpu.PrefetchScalarGridSpec(
            num_scalar_prefetch=1,
            grid=(N_SR, N_EXPERTS // E_HALF),
            in_specs=[
                pl.BlockSpec((SR_TOKENS, D), lambda s, h, lo: (s, 0)),
                pl.BlockSpec((E_HALF, SR_TOKENS), lambda s, h, lo: (h, s)),
            ],
            out_specs=pl.BlockSpec((E_HALF, SR_SLOTS, D), lambda s, h, lo: (h, s, 0)),
        ),
        compiler_params=_cparams(("arbitrary", "arbitrary")),
        name="gather",
    )(lo_flat, hn, slotT)


def _ffn_kernel(xe_ref, wg_ref, wu_ref, wd_ref, ye_ref):
    xe = xe_ref[...]
    hg = jnp.dot(xe, wg_ref[...].astype(BF16), preferred_element_type=F32)
    hu = jnp.dot(xe, wu_ref[...].astype(BF16), preferred_element_type=F32)
    hid = (_silu(hg) * hu).astype(BF16)
    ye_ref[...] = jnp.dot(hid, wd_ref[...].astype(BF16), preferred_element_type=F32).astype(BF16)


def _ffn(xe, wg, wu, wd, layer):
    m = xe.shape[1]
    w_spec = pl.BlockSpec((None, None, D, D), lambda e: (layer, e, 0, 0))
    return pl.pallas_call(
        _ffn_kernel,
        out_shape=jax.ShapeDtypeStruct((N_EXPERTS, m, D), BF16),
        grid=(N_EXPERTS,),
        in_specs=[pl.BlockSpec((None, m, D), lambda e: (e, 0, 0)), w_spec, w_spec, w_spec],
        out_specs=pl.BlockSpec((None, m, D), lambda e: (e, 0, 0)),
        compiler_params=_cparams(("arbitrary",)),
        name="expert_ffn",
    )(xe, wg, wu, wd)


def _combine_kernel(lo_ref, x_ref, ye_ref, slot_ref, gate_ref, mod_ref, fnw_ref, spread_ref, *rest, final):
    if final:
        oc_ref, ol_ref, yw_ref, acc_ref = rest
    else:
        o_ref, yw_ref, acc_ref = rest
    sr = pl.program_id(0)
    j = pl.program_id(1)
    starts = []
    fits = None
    for e in range(N_EXPERTS):
        start = jnp.minimum((_lo_at(lo_ref, sr, e, j) // BF16_ROWS) * BF16_ROWS, SR_SLOTS - CW_ROWS)
        ok = _lo_at(lo_ref, sr, e, j + 1) - start <= CW_ROWS
        fits = ok if fits is None else jnp.logical_and(fits, ok)
        starts.append(start)
    slot = slot_ref[...]
    gate = gate_ref[...]

    @pl.when(fits)
    def _():
        for e in range(N_EXPERTS):
            yw_ref[e * CW_ROWS:(e + 1) * CW_ROWS, :] = ye_ref[e, pl.ds(pl.multiple_of(starts[e], BF16_ROWS), CW_ROWS), :]
        n_k = N_EXPERTS * CW_ROWS
        k_exp = lax.broadcasted_iota(jnp.int32, (1, n_k), 1) // CW_ROWS
        k_row = (lax.broadcasted_iota(jnp.int32, (1, n_k), 1) % CW_ROWS).astype(F32)
        start_k = jnp.zeros((1, n_k), F32)
        for e in range(N_EXPERTS):
            start_k = jnp.where(k_exp == e, starts[e].astype(F32), start_k)
        spread = spread_ref[...]
        slot_k = jnp.dot(slot.astype(F32).astype(BF16), spread, preferred_element_type=F32)
        hit = slot_k - start_k == k_row
        gate_k = jnp.dot(gate.astype(BF16), spread, preferred_element_type=F32)
        acc_ref[...] = jnp.dot(jnp.where(hit, gate_k, 0.0).astype(BF16), yw_ref[...], preferred_element_type=F32)

    @pl.when(jnp.logical_not(fits))
    def _():
        s_iota = lax.broadcasted_iota(jnp.int32, (BLK, SR_SLOTS), 1)
        acc = jnp.zeros((BLK, D), F32)
        for e in range(N_EXPERTS):
            onehot = jnp.where(slot[:, e:e + 1] == s_iota, 1.0, 0.0).astype(BF16)
            acc = acc + gate[:, e:e + 1] * jnp.dot(onehot, ye_ref[e], preferred_element_type=F32)
        acc_ref[...] = acc

    out = x_ref[...] + mod_ref[5:6, :] * acc_ref[...]
    if not final:
        o_ref[...] = out
    else:
        out = out * lax.rsqrt(jnp.mean(out * out, axis=-1, keepdims=True) + EPS) * fnw_ref[...]
        ol_ref[...] = out

        @pl.when(sr < N_SR_CTX)
        def _():
            oc_ref[...] = out


def _combine(lo_flat, x, ye, slot_tok, gate_tok, mod_l, fnw, *, final):
    blk_of = lambda s, j: s * N_BLK + j
    n_ctx_blk = R_CTX // BLK
    spread = jnp.asarray(np.arange(128)[:, None] == (np.arange(N_EXPERTS * CW_ROWS) // CW_ROWS)[None, :], BF16)
    if final:
        out_shape = (jax.ShapeDtypeStruct((R_CTX, D), F32), jax.ShapeDtypeStruct((R_LAT, D), F32))
        out_specs = (pl.BlockSpec((BLK, D), lambda s, j, lo: (jnp.minimum(blk_of(s, j), n_ctx_blk - 1), 0)),
                     pl.BlockSpec((BLK, D), lambda s, j, lo: (jnp.maximum(blk_of(s, j) - n_ctx_blk, 0), 0)))
    else:
        out_shape = jax.ShapeDtypeStruct((R, D), F32)
        out_specs = pl.BlockSpec((BLK, D), lambda s, j, lo: (blk_of(s, j), 0))
    return pl.pallas_call(
        functools.partial(_combine_kernel, final=final),
        out_shape=out_shape,
        grid_spec=pltpu.PrefetchScalarGridSpec(
            num_scalar_prefetch=1,
            grid=(N_SR, N_BLK),
            in_specs=[
                pl.BlockSpec((BLK, D), lambda s, j, lo: (blk_of(s, j), 0)),
                pl.BlockSpec((N_EXPERTS, SR_SLOTS, D), lambda s, j, lo: (0, s, 0)),
                pl.BlockSpec((BLK, 128), lambda s, j, lo: (blk_of(s, j), 0)),
                pl.BlockSpec((BLK, 128), lambda s, j, lo: (blk_of(s, j), 0)),
                pl.BlockSpec((None, MOD_CHUNKS, D),
                             lambda s, j, lo: (_cond_of_tile(blk_of(s, j) * BLK // TM), 0, 0)),
                pl.BlockSpec((1, D), lambda s, j, lo: (0, 0)),
                pl.BlockSpec((128, N_EXPERTS * CW_ROWS), lambda s, j, lo: (0, 0)),
            ],
            out_specs=out_specs,
            scratch_shapes=[pltpu.VMEM((N_EXPERTS * CW_ROWS, D), BF16), pltpu.VMEM((BLK, D), F32)],
        ),
        compiler_params=_cparams(("arbitrary", "arbitrary")),
        name="combine",
    )(lo_flat, x, ye, slot_tok, gate_tok, mod_l, fnw, spread)


def _moe(x, mod_l, nw, wrT, wg, wu, wd, fnw, layer, *, final):
    hn, affT = _router(x, mod_l, nw, wrT)
    slotT, slot_tok, gate_tok, lo = _select(affT)
    lo_flat = lo[:, :, :LO_COLS].reshape(-1)
    xe = _gather(lo_flat, hn, slotT)
    ye = _ffn(xe, wg, wu, wd, layer)
    return _combine(lo_flat, x, ye, slot_tok, gate_tok, mod_l, fnw, final=final)


def _group_major(p):
    return p.reshape(2, GROUPS, HPG).transpose(1, 0, 2).reshape(GROUPS, 2 * HPG)


def kernel(x_prompt, x_sample, state_ssm, c, c_ctx, norm1_w, norm2_w, w_mod, b_mod, conv_in_w, conv_w, conv_out_w, ssd_in_w, ssd_conv_w, ssd_conv_b, ssd_dt_bias, ssd_a_log, ssd_d, ssd_norm_w, ssd_out_w, router_w, exp_w_gate, exp_w_up, exp_w_down, final_norm_w):
    x = (x_prompt.reshape(R_CTX, D), x_sample.reshape(R_LAT, D))
    cond = jnp.concatenate([c_ctx[None, :], c, jnp.zeros((8 - N_COND, D), F32)], axis=0)
    mods = _modulation(cond.T, w_mod, b_mod)
    mods = mods[:, :N_COND].reshape(DEPTH, N_COND, MOD_CHUNKS, D)
    fnw = final_norm_w.reshape(1, D)
    conv_in_b, conv_out_b = conv_in_w.astype(BF16), conv_out_w.astype(BF16)
    ssd_in_b, ssd_out_b = ssd_in_w.astype(BF16), ssd_out_w.astype(BF16)
    ssd_conv_b3 = ssd_conv_b.reshape(DEPTH // 2, 1, CONV_DIM)
    ssd_norm_w3 = ssd_norm_w.reshape(DEPTH // 2, 1, D_INNER)

    states = None
    for layer in range(DEPTH):
        mod_l = mods[layer]
        j = layer // 2
        nw1 = norm1_w[layer].reshape(1, D)
        if layer % 2 == 0:
            x = _conv_mixer(x, mod_l, nw1, conv_in_b, conv_w, conv_out_b, j)
        else:
            w_dt = ssd_in_b[j, :, D_INNER + CONV_DIM:].reshape(D, 2, GROUPS, HPG).transpose(0, 2, 1, 3)
            w_dt_rep = jnp.broadcast_to(w_dt.reshape(D, GROUPS, 1, 2 * HPG), (D, GROUPS, N_SPLIT, 2 * HPG))
            dtb = _group_major(ssd_dt_bias[j])
            dtb_rep = jnp.tile(dtb, (1, N_SPLIT)).reshape(1, GROUPS * DT_LANES)
            z, xs, bm, cm, dt4, dtT4 = _ssd_in(
                x, mod_l, nw1, ssd_in_b, w_dt_rep.reshape(D, GROUPS * DT_LANES), w_dt.reshape(D, 2 * HEADS).T,
                ssd_conv_w, ssd_conv_b3, dtb_rep, dtb.reshape(2 * HEADS, 1), j)
            alog4 = _group_major(ssd_a_log[j])
            dsum = (ssd_d[j][0] + ssd_d[j][1]).reshape(GROUPS, HPG)
            dsk4 = jnp.tile(jnp.concatenate([jnp.zeros_like(dsum), dsum], axis=1), (1, N_SPLIT))
            scan_args = (xs.reshape(R // CHUNK, CHUNK, D_INNER), bm.reshape(R // CHUNK, CHUNK, GROUPS * STATE),
                         cm.reshape(R // CHUNK, CHUNK, GROUPS * STATE), dt4, dtT4,
                         jnp.tile(alog4, (1, N_SPLIT)).reshape(GROUPS, 1, DT_LANES),
                         alog4.reshape(GROUPS, 2 * HPG, 1), dsk4.reshape(GROUPS, 1, DT_LANES))
            y_ctx, states = _ssd_scan(*scan_args, None, states, n_seq=N_CTX, seq_len=L_CTX, row_block0=0,
                                      state_layer=j)
            h0 = state_ssm[:, j].reshape(N_LAT, 2, HEADS * HEADDIM, STATE)
            y_lat = _ssd_scan(*scan_args, h0, None, n_seq=N_LAT, seq_len=L_LAT, row_block0=R_CTX // L_LAT,
                              state_layer=None)
            x = _ssd_out(x, y_ctx.reshape(R_CTX, D_INNER), y_lat.reshape(R_LAT, D_INNER), z, mod_l,
                         ssd_norm_w3, ssd_out_b, j)
        x = _moe(x, mod_l, norm2_w[layer].reshape(1, D), router_w[layer].T, exp_w_gate, exp_w_up, exp_w_down, fnw,
                 layer, final=(layer == DEPTH - 1))

    y_prompt, y_sample = x
    return (y_prompt.reshape(N_CTX, L_CTX, D), y_sample.reshape(N_LAT, L_LAT, D),
            states.reshape(N_CTX, DEPTH // 2, 2, HEADS, HEADDIM, STATE))
```
